```python
import math
import jax, jax.numpy as jnp
from jax import lax
import numpy as np

D_MODEL = 1024
BATCH = 4
SEQ = 8192
DEPTH = 1

M_HEADS = 4
M_HEAD_DIM = 128
M_WIDTH = M_HEADS * M_HEAD_DIM
M_CONV = 4
M_CHUNK = 64
A_HEADS = 4
A_QK_DIM = 64
A_V_DIM = 2 * A_QK_DIM
A_WIDTH = A_HEADS * A_V_DIM
A_QBLOCK = 128
N_BUCKETS = 32
MAX_DISTANCE = 128
D_FF = 2816
FFN_CONV = 3
RMS_EPS = 1e-6
IN_SIZES = (2 * M_WIDTH, M_WIDTH, M_WIDTH, M_HEADS, M_HEADS, 2 * A_HEADS * A_QK_DIM, 2 * A_HEADS * A_QK_DIM, A_WIDTH, D_MODEL, D_MODEL)
IN_COLS = sum(IN_SIZES)
IN_SPLITS = tuple(int(v) for v in np.cumsum(IN_SIZES)[:-1])

kernel_name = 'hybrid_mlstm_diffattn_convffn_block'


def rms_norm(x, g):
    xf = x.astype(jnp.float32)
    y = xf * lax.rsqrt(jnp.mean(xf * xf, axis=-1, keepdims=True) + RMS_EPS)
    return (y * g.astype(jnp.float32)).astype(x.dtype)


def causal_dwconv(x, w, b):
    k_w = w.shape[0]
    y = lax.conv_general_dilated(x, w[:, None, :].astype(x.dtype), window_strides=(1,), padding=[(k_w - 1, 0)],
                                 dimension_numbers=('NWC', 'WIO', 'NWC'), feature_group_count=x.shape[-1])
    return y + b.astype(x.dtype)


def t5_bucket(dist):
    n = jnp.maximum(dist, 0)
    max_exact = N_BUCKETS // 2
    nf = jnp.maximum(n, 1).astype(jnp.float32)
    large = max_exact + (jnp.log(nf / max_exact) / math.log(MAX_DISTANCE / max_exact) * (N_BUCKETS - max_exact)).astype(jnp.int32)
    large = jnp.minimum(large, N_BUCKETS - 1)
    return jnp.where(n < max_exact, n, large)


def mlstm_chunkwise(q, k, v, i_pre, f_pre):
    b_, h_, s_, d_ = q.shape
    L = M_CHUNK
    nc = s_ // L
    q = q.reshape(b_, h_, nc, L, d_)
    k = k.reshape(b_, h_, nc, L, d_)
    v = v.reshape(b_, h_, nc, L, d_)
    logf = jax.nn.log_sigmoid(f_pre).reshape(b_, h_, nc, L)
    logi = i_pre.reshape(b_, h_, nc, L)
    bcum = jnp.cumsum(logf, axis=-1)
    b_last = bcum[..., -1]
    g = b_last[..., None] - bcum + logi
    m_loc = jnp.max(g, axis=-1)
    w = jnp.exp(g - m_loc[..., None])
    c_loc = jnp.einsum('bhcs,bhcsd,bhcse->bhcde', w, v, k)
    n_loc = jnp.einsum('bhcs,bhcse->bhce', w, k)

    def step(carry, inp):
        c_st, n_st, m_st = carry
        cl, nl, ml, f_tot = inp
        m_new = jnp.maximum(f_tot + m_st, ml)
        a = jnp.exp(f_tot + m_st - m_new)
        bb = jnp.exp(ml - m_new)
        c_new = a[..., None, None] * c_st + bb[..., None, None] * cl
        n_new = a[..., None] * n_st + bb[..., None] * nl
        return (c_new, n_new, m_new), (c_st, n_st, m_st)

    init = (jnp.zeros((b_, h_, d_, d_), jnp.float32), jnp.zeros((b_, h_, d_), jnp.float32), jnp.zeros((b_, h_), jnp.float32))
    xs = (jnp.moveaxis(c_loc, 2, 0), jnp.moveaxis(n_loc, 2, 0), jnp.moveaxis(m_loc, 2, 0), jnp.moveaxis(b_last, 2, 0))
    _, (c_in, n_in, m_in) = lax.scan(step, init, xs)
    c_in = jnp.moveaxis(c_in, 0, 2)
    n_in = jnp.moveaxis(n_in, 0, 2)
    m_in = jnp.moveaxis(m_in, 0, 2)

    a_t = bcum + m_in[..., None]
    dmat = bcum[..., :, None] - bcum[..., None, :] + logi[..., None, :]
    causal = jnp.tril(jnp.ones((L, L), dtype=bool))
    dmat = jnp.where(causal, dmat, -jnp.inf)
    m_t = jnp.maximum(a_t, jnp.max(dmat, axis=-1))
    inter_w = jnp.exp(a_t - m_t)
    s = jnp.einsum('bhctd,bhcsd->bhcts', q, k) * jnp.exp(dmat - m_t[..., None])
    num = jnp.einsum('bhcts,bhcsd->bhctd', s, v) + inter_w[..., None] * jnp.einsum('bhcde,bhcte->bhctd', c_in, q)
    den_dot = jnp.sum(s, axis=-1) + inter_w * jnp.einsum('bhce,bhcte->bhct', n_in, q)
    den = jnp.maximum(jnp.abs(den_dot), jnp.exp(-m_t))
    return (num / den[..., None]).reshape(b_, h_, s_, d_)


def diff_attention(q, k, v, rel_bias, lam):
    b_, h_, _, s_, _ = q.shape
    dv = v.shape[-1]
    nqb = s_ // A_QBLOCK
    k_pos = jnp.arange(s_, dtype=jnp.int32)
    bias_tab = rel_bias.astype(jnp.float32).T

    def one_block(j):
        start = j * A_QBLOCK
        qb = lax.dynamic_slice_in_dim(q, start, A_QBLOCK, axis=3)
        logits = jnp.einsum('bhmqd,bhmkd->bhmqk', qb, k)
        dist = (start + jnp.arange(A_QBLOCK, dtype=jnp.int32))[:, None] - k_pos[None, :]
        bias = bias_tab[:, t5_bucket(dist)]
        logits = jnp.where(dist >= 0, logits + bias[None, :, None], -jnp.inf)
        p = jax.nn.softmax(logits, axis=-1)
        wts = p[:, :, 0] - lam * p[:, :, 1]
        return jnp.einsum('bhqk,bhkd->bhqd', wts, v)

    out = lax.map(one_block, jnp.arange(nqb, dtype=jnp.int32))
    return out.transpose(1, 0, 3, 2, 4).reshape(b_, s_, h_, dv)


def setup_inputs(seed: int = 0) -> dict:
    key = jax.random.key(seed)
    ks = jax.random.split(key, 24)
    f32 = jnp.float32

    def nrm(k, shape, s):
        return jax.random.normal(k, shape, f32) * s

    return {
        'x': nrm(ks[0], (BATCH, SEQ, D_MODEL), 1.0),
        'c': nrm(ks[1], (BATCH, D_MODEL), 1.0),
        'w_ada': nrm(ks[2], (DEPTH, D_MODEL, 6 * D_MODEL), D_MODEL ** -0.5),
        'b_ada': nrm(ks[3], (DEPTH, 6 * D_MODEL), 0.02),
        'norm1_g': 1.0 + nrm(ks[4], (DEPTH, D_MODEL), 0.1),
        'w_in': nrm(ks[5], (DEPTH, D_MODEL, IN_COLS), D_MODEL ** -0.5),
        'm_conv_w': nrm(ks[6], (DEPTH, M_CONV, 2 * M_WIDTH), M_CONV ** -0.5),
        'm_conv_b': nrm(ks[7], (DEPTH, 2 * M_WIDTH), 0.02),
        'm_igate_b': nrm(ks[8], (DEPTH, M_HEADS), 0.1),
        'm_fgate_b': jnp.linspace(3.0, 6.0, M_HEADS, dtype=f32)[None, :] + nrm(ks[9], (DEPTH, M_HEADS), 0.1),
        'm_norm_g': 1.0 + nrm(ks[10], (DEPTH, M_WIDTH), 0.1),
        'a_qnorm_g': 1.0 + nrm(ks[11], (DEPTH, A_QK_DIM), 0.1),
        'a_knorm_g': 1.0 + nrm(ks[12], (DEPTH, A_QK_DIM), 0.1),
        'a_lambda': nrm(ks[13], (DEPTH, 4, A_QK_DIM), 0.1),
        'a_norm_g': 1.0 + nrm(ks[14], (DEPTH, A_WIDTH), 0.1),
        'rel_bias': nrm(ks[15], (N_BUCKETS, A_HEADS), 0.5),
        'w_branch_m': nrm(ks[16], (DEPTH, M_WIDTH, D_MODEL), M_WIDTH ** -0.5),
        'w_branch_a': nrm(ks[17], (DEPTH, A_WIDTH, D_MODEL), A_WIDTH ** -0.5),
        'w_out': nrm(ks[18], (DEPTH, D_MODEL, D_MODEL), D_MODEL ** -0.5),
        'norm2_g': 1.0 + nrm(ks[19], (DEPTH, D_MODEL), 0.1),
        'w_up': nrm(ks[20], (DEPTH, D_MODEL, 2 * D_FF), D_MODEL ** -0.5),
        'ffn_conv_w': nrm(ks[21], (DEPTH, FFN_CONV, 2 * D_FF), FFN_CONV ** -0.5),
        'ffn_conv_b': nrm(ks[22], (DEPTH, 2 * D_FF), 0.02),
        'w_down': nrm(ks[23], (DEPTH, D_FF, D_MODEL), D_FF ** -0.5),
    }


def reference(x, c, w_ada, b_ada, norm1_g, w_in, m_conv_w, m_conv_b, m_igate_b, m_fgate_b, m_norm_g, a_qnorm_g, a_knorm_g, a_lambda, a_norm_g, rel_bias, w_branch_m, w_branch_a, w_out, norm2_g, w_up, ffn_conv_w, ffn_conv_b, w_down):
    f32 = jnp.float32
    b_, s_, _ = x.shape
    for l in range(DEPTH):
        mod = jax.nn.silu(c) @ w_ada[l] + b_ada[l]
        shift1, scale1, gate1, shift2, scale2, gate2 = jnp.split(mod[:, None, :], 6, axis=-1)

        h = rms_norm(x, norm1_g[l]) * (1.0 + scale1) + shift1
        proj = h @ w_in[l]
        mqk, mv, mo, mi, mf, aq, ak, av, g_m, g_a = jnp.split(proj, IN_SPLITS, axis=-1)

        mqk = jax.nn.silu(causal_dwconv(mqk, m_conv_w[l], m_conv_b[l]))
        mq, mk = jnp.split(mqk, 2, axis=-1)

        def to_heads(t):
            return t.astype(f32).reshape(b_, s_, M_HEADS, M_HEAD_DIM).transpose(0, 2, 1, 3)

        i_pre = (mi + m_igate_b[l]).astype(f32).transpose(0, 2, 1)
        f_pre = (mf + m_fgate_b[l]).astype(f32).transpose(0, 2, 1)
        hm = mlstm_chunkwise(to_heads(mq), to_heads(mk) * (M_HEAD_DIM ** -0.5), to_heads(mv), i_pre, f_pre)
        hm = rms_norm(hm.transpose(0, 2, 1, 3), m_norm_g[l].reshape(M_HEADS, M_HEAD_DIM)).reshape(b_, s_, M_WIDTH)
        hm = (jax.nn.sigmoid(mo.astype(f32)) * hm).astype(x.dtype)

        lam_init = 0.8 - 0.6 * math.exp(-0.3 * l)
        lamv = a_lambda[l].astype(f32)
        lam = jnp.exp(jnp.sum(lamv[0] * lamv[1])) - jnp.exp(jnp.sum(lamv[2] * lamv[3])) + lam_init
        qa = rms_norm(aq.reshape(b_, s_, 2, A_HEADS, A_QK_DIM), a_qnorm_g[l]).astype(f32).transpose(0, 3, 2, 1, 4) * (A_QK_DIM ** -0.5)
        ka = rms_norm(ak.reshape(b_, s_, 2, A_HEADS, A_QK_DIM), a_knorm_g[l]).astype(f32).transpose(0, 3, 2, 1, 4)
        va = av.astype(f32).reshape(b_, s_, A_HEADS, A_V_DIM).transpose(0, 2, 1, 3)
        ha = diff_attention(qa, ka, va, rel_bias, lam)
        ha = (rms_norm(ha, a_norm_g[l].reshape(A_HEADS, A_V_DIM)) * (1.0 - lam_init)).reshape(b_, s_, A_WIDTH).astype(x.dtype)

        y = jax.nn.sigmoid(g_m) * (hm @ w_branch_m[l]) + jax.nn.sigmoid(g_a) * (ha @ w_branch_a[l])
        x = x + gate1 * (y @ w_out[l])

        h2 = rms_norm(x, norm2_g[l]) * (1.0 + scale2) + shift2
        u = causal_dwconv(h2 @ w_up[l], ffn_conv_w[l], ffn_conv_b[l])
        val, gate = jnp.split(u, 2, axis=-1)
        x = x + gate2 * ((jax.nn.silu(gate) * val) @ w_down[l])
    return x
```

```python
import functools
import math

import jax
import jax.numpy as jnp
from jax import lax
from jax.experimental import pallas as pl
from jax.experimental.pallas import tpu as pltpu

D_MODEL = 1024
M_HEADS = 4
M_HEAD_DIM = 128
M_WIDTH = M_HEADS * M_HEAD_DIM
M_CONV = 4
A_HEADS = 4
A_QK_DIM = 64
A_V_DIM = 2 * A_QK_DIM
A_WIDTH = A_HEADS * A_V_DIM
N_BUCKETS = 32
MAX_DISTANCE = 128
D_FF = 2816
FFN_CONV = 3
RMS_EPS = 1e-6
LOG2E = math.log2(math.e)
NEG_BIG = -1e30

LANES = 128
SUBLANES = 8
VMEM_LIMIT = 56 * 1024 * 1024

TM_PROJ = 512
M_CHUNK = 256
A_TILE = 512
F_CHUNK = 256
N_FCHUNK = D_FF // F_CHUNK

bf16 = jnp.bfloat16
f32 = jnp.float32


def _dot(a, b):
    return jnp.dot(a, b, preferred_element_type=f32)


def _dot_nt(a, b):
    return lax.dot_general(a, b, (((1,), (1,)), ((), ())), preferred_element_type=f32)


def _sigmoid(x):
    return 1.0 / (1.0 + jnp.exp(-x))


def _const_spec(shape):
    nd = len(shape)
    return pl.BlockSpec(shape, lambda *_: (0,) * nd)


def _shift_rows(u, prev8, k):
    r = pltpu.roll(u, k, 0)
    rp = pltpu.roll(prev8, k, 0)
    row = lax.broadcasted_iota(jnp.int32, (SUBLANES, u.shape[1]), 0)
    first = jnp.where(row < k, rp, r[:SUBLANES])
    return jnp.concatenate([first, r[SUBLANES:]], axis=0)


def _adaln_kernel(c_ref, w_ref, b_ref, o_ref):
    c = c_ref[...]
    a = (c * _sigmoid(c)).astype(bf16)
    o_ref[...] = _dot(a, w_ref[...].astype(bf16)) + b_ref[...]


def _adaln(c8, w, b):
    n = w.shape[1]
    tn = 1536
    return pl.pallas_call(
        _adaln_kernel,
        grid=(n // tn,),
        in_specs=[_const_spec(c8.shape),
                  pl.BlockSpec((D_MODEL, tn), lambda j: (0, j)),
                  pl.BlockSpec((1, tn), lambda j: (0, j))],
        out_specs=pl.BlockSpec((c8.shape[0], tn), lambda j: (0, j)),
        out_shape=jax.ShapeDtypeStruct((c8.shape[0], n), f32),
        compiler_params=pltpu.CompilerParams(dimension_semantics=("arbitrary",),
                                             vmem_limit_bytes=VMEM_LIMIT),
        name="adaln",
    )(c8, w, b)


def _bias_kernel(tab_ref, o_ref):
    h = pl.program_id(0)
    t = pl.program_id(1)
    row = lax.broadcasted_iota(jnp.int32, (A_TILE, A_TILE), 0)
    col = lax.broadcasted_iota(jnp.int32, (A_TILE, A_TILE), 1)
    dist = row - col + (1 - t) * A_TILE
    n = jnp.maximum(dist, 0)
    max_exact = N_BUCKETS // 2
    nf = jnp.maximum(n, 1).astype(f32)
    large = max_exact + (jnp.log(nf / max_exact) / math.log(MAX_DISTANCE / max_exact)
                         * (N_BUCKETS - max_exact)).astype(jnp.int32)
    large = jnp.minimum(large, N_BUCKETS - 1)
    bucket = jnp.where(n < max_exact, n, large)
    far = tab_ref[h, N_BUCKETS - 1]
    val = jnp.zeros((A_TILE, A_TILE), f32)
    for b in range(N_BUCKETS - 1):
        val = jnp.where(bucket == b, tab_ref[h, b] - far, val)
    o_ref[0, 0] = jnp.where(dist >= 0, val * LOG2E, NEG_BIG)


def _bias_tiles(tab):
    return pl.pallas_call(
        _bias_kernel,
        grid=(A_HEADS, 2),
        in_specs=[pl.BlockSpec(memory_space=pltpu.SMEM)],
        out_specs=pl.BlockSpec((1, 1, A_TILE, A_TILE), lambda h, t: (h, t, 0, 0)),
        out_shape=jax.ShapeDtypeStruct((A_HEADS, 2, A_TILE, A_TILE), f32),
        compiler_params=pltpu.CompilerParams(dimension_semantics=("arbitrary", "arbitrary"),
                                             vmem_limit_bytes=VMEM_LIMIT),
        name="bias_tiles",
    )(tab)


C_MQK, C_MV, C_MO, C_AQ, C_AK, C_AV, C_GM, C_GA, C_GATE, C_END = (
    0, 1024, 1536, 2048, 2560, 3072, 3584, 4608, 5632, 5888)


def _inproj_kernel(x_ref, mod_ref, g_ref, w_ref, grp_ref, qg_ref, kg_ref,
                   mqk_ref, mv_ref, mo_ref, qn_ref, kn_ref, av_ref, gm_ref, ga_ref, gate_ref):
    x = x_ref[...]
    mod = mod_ref[0]
    ms = jnp.mean(x * x, axis=-1, keepdims=True)
    y = x * lax.rsqrt(ms + RMS_EPS) * g_ref[...]
    hb = (y * (1.0 + mod[1:2]) + mod[0:1]).astype(bf16)

    def proj(c0, c1):
        return _dot(hb, w_ref[:, c0:c1])

    def qknorm(a, gain_ref):
        msq = _dot((a * a).astype(bf16), grp_ref[...])
        return (a * lax.rsqrt(msq + RMS_EPS) * gain_ref[...]).astype(bf16)

    mqk_ref[:, 0:512] = proj(C_MQK, C_MQK + 512).astype(bf16)
    mqk_ref[:, 512:1024] = proj(C_MQK + 512, C_MV).astype(bf16)
    mv_ref[...] = proj(C_MV, C_MO).astype(bf16)
    mo_ref[...] = proj(C_MO, C_AQ).astype(bf16)
    qn_ref[...] = qknorm(proj(C_AQ, C_AK), qg_ref)
    kn_ref[...] = qknorm(proj(C_AK, C_AV), kg_ref)
    av_ref[...] = proj(C_AV, C_GM).astype(bf16)
    gm_ref[:, 0:512] = proj(C_GM, C_GM + 512).astype(bf16)
    gm_ref[:, 512:1024] = proj(C_GM + 512, C_GA).astype(bf16)
    ga_ref[:, 0:512] = proj(C_GA, C_GA + 512).astype(bf16)
    ga_ref[:, 512:1024] = proj(C_GA + 512, C_GATE).astype(bf16)
    gate_ref[...] = proj(C_GATE, C_END)


def _inproj(x2, mod3, g1, w, grp, qg, kg, tiles_per_batch):
    n = x2.shape[0]
    tm = TM_PROJ
    tok = lambda width: pl.BlockSpec((tm, width), lambda i: (i, 0))
    outs = [(1024, bf16), (512, bf16), (512, bf16), (512, bf16), (512, bf16), (512, bf16),
            (1024, bf16), (1024, bf16), (2 * LANES, f32)]
    return pl.pallas_call(
        _inproj_kernel,
        grid=(n // tm,),
        in_specs=[tok(D_MODEL),
                  pl.BlockSpec((1, 6, D_MODEL), lambda i: (i // tiles_per_batch, 0, 0)),
                  _const_spec(g1.shape), _const_spec(w.shape), _const_spec(grp.shape),
                  _const_spec(qg.shape), _const_spec(kg.shape)],
        out_specs=[tok(wd) for wd, _ in outs],
        out_shape=[jax.ShapeDtypeStruct((n, wd), dt) for wd, dt in outs],
        compiler_params=pltpu.CompilerParams(dimension_semantics=("arbitrary",),
                                             vmem_limit_bytes=VMEM_LIMIT),
        name="inproj",
    )(x2, mod3, g1, w, grp, qg, kg)


def _mlstm_kernel(mqk_ref, mv_ref, mo_ref, gate_ref, cw_ref, cb_ref, bi_ref, bfg_ref, mg_ref,
                  o_ref, c_st, m_st, tail_ref):
    L = M_CHUNK
    d = M_HEAD_DIM

    @pl.when(pl.program_id(1) == 0)
    def _():
        c_st[...] = jnp.zeros_like(c_st)
        m_st[...] = jnp.zeros_like(m_st)
        tail_ref[...] = jnp.zeros_like(tail_ref)

    x = mqk_ref[...].astype(f32)
    prev8 = tail_ref[...]
    cw = cw_ref[...]
    y = (cw[3:4] * x + cw[2:3] * _shift_rows(x, prev8, 1) + cw[1:2] * _shift_rows(x, prev8, 2)
         + cw[0:1] * _shift_rows(x, prev8, 3) + cb_ref[...])
    tail_ref[...] = x[L - SUBLANES:, :]
    qk = y * _sigmoid(y)

    gi = gate_ref[:, 0:LANES] + bi_ref[...]
    gf = gate_ref[:, LANES:2 * LANES] + bfg_ref[...]
    logf = jnp.minimum(gf, 0.0) - jnp.log(1.0 + jnp.exp(-jnp.abs(gf)))
    row = lax.broadcasted_iota(jnp.int32, (L, L), 0)
    col = lax.broadcasted_iota(jnp.int32, (L, L), 1)
    causal = row >= col
    tri = jnp.where(causal, 1.0, 0.0).astype(bf16)
    logf_hi = logf.astype(bf16)
    logf_lo = (logf - logf_hi.astype(f32)).astype(bf16)
    bcum = _dot(tri, logf_hi) + _dot(tri, logf_lo)
    r = gi - bcum
    r_t = r.T
    b_last = bcum[L - 1:L, :]
    g = b_last + r
    m_loc = jnp.max(g, axis=0, keepdims=True)
    ones = jnp.ones((L, d), bf16)

    for h in range(M_HEADS):
        hs = slice(h * d, (h + 1) * d)
        qh = qk[:, hs].astype(bf16)
        kh = qk[:, M_WIDTH + h * d:M_WIDTH + (h + 1) * d] * (d ** -0.5)
        vaug = jnp.concatenate([mv_ref[:, hs], ones], axis=1)
        m_in = m_st[h][0:1, 0:1]
        b_col = bcum[:, h:h + 1]
        dmat = jnp.where(causal, b_col + r_t[h:h + 1, :], NEG_BIG)
        a_t = b_col + m_in
        m_t = jnp.maximum(a_t, jnp.max(dmat, axis=1, keepdims=True))
        inter_w = jnp.exp(a_t - m_t)
        s = _dot_nt(qh, kh.astype(bf16)) * jnp.exp(dmat - m_t)
        tot = _dot(s.astype(bf16), vaug) + inter_w * _dot(qh, c_st[h].astype(bf16))
        den = jnp.maximum(jnp.abs(tot[:, d:]), jnp.exp(-m_t))
        hh = tot[:, :d] / den

        bl = b_last[:, h:h + 1]
        m_new = jnp.maximum(bl + m_in, m_loc[:, h:h + 1])
        w_col = jnp.exp(g[:, h:h + 1] - m_new)
        kw_t = (kh * w_col).T.astype(bf16)
        c_st[h] = jnp.exp(bl + m_in - m_new) * c_st[h] + _dot(kw_t, vaug)
        m_st[h] = jnp.broadcast_to(m_new, (SUBLANES, LANES))

        hn = hh * lax.rsqrt(jnp.mean(hh * hh, axis=-1, keepdims=True) + RMS_EPS) * mg_ref[:, hs]
        o_ref[:, hs] = (_sigmoid(mo_ref[:, hs].astype(f32)) * hn).astype(bf16)


def _mlstm(mqk, mv, mo, gates, cw, cb, bi, bfg, mg, batch, seq):
    L = M_CHUNK
    nc = seq // L
    tok = lambda width: pl.BlockSpec((L, width), lambda b, c: (b * nc + c, 0))
    return pl.pallas_call(
        _mlstm_kernel,
        grid=(batch, nc),
        in_specs=[tok(2 * M_WIDTH), tok(M_WIDTH), tok(M_WIDTH), tok(2 * LANES),
                  _const_spec(cw.shape), _const_spec(cb.shape), _const_spec(bi.shape),
                  _const_spec(bfg.shape), _const_spec(mg.shape)],
        out_specs=tok(M_WIDTH),
        out_shape=jax.ShapeDtypeStruct((batch * seq, M_WIDTH), bf16),
        scratch_shapes=[pltpu.VMEM((M_HEADS, M_HEAD_DIM, 2 * M_HEAD_DIM), f32),
                        pltpu.VMEM((M_HEADS, SUBLANES, LANES), f32),
                        pltpu.VMEM((SUBLANES, 2 * M_WIDTH), f32)],
        compiler_params=pltpu.CompilerParams(dimension_semantics=("arbitrary", "arbitrary"),
                                             vmem_limit_bytes=VMEM_LIMIT),
        name="mlstm",
    )(mqk, mv, mo, gates, cw, cb, bi, bfg, mg)


def _attn_kernel(q_ref, k_ref, v_ref, bias_ref, lam_ref, ng_ref, o_ref,
                 vaug_ref, q2_ref, acc_ref, m_ref, *, lam_init):
    T = A_TILE
    dv = A_V_DIM
    qi = pl.program_id(2)

    @pl.when(qi == 0)
    def _():
        vaug_ref[:, 0:dv] = v_ref[...]
        vaug_ref[:, dv:2 * dv] = jnp.ones((v_ref.shape[0], dv), bf16)

    q = q_ref[...]
    lane = lax.broadcasted_iota(jnp.int32, q.shape, 1)
    zero = jnp.zeros_like(q)
    q2_ref[0:T, :] = jnp.where(lane < A_QK_DIM, q, zero)
    q2_ref[T:2 * T, :] = jnp.where(lane >= A_QK_DIM, q, zero)
    acc_ref[...] = jnp.zeros_like(acc_ref)
    m_ref[...] = jnp.full_like(m_ref, NEG_BIG)

    def step(j, bias):
        start = pl.multiple_of(j * T, T)
        s = _dot_nt(q2_ref[...], k_ref[pl.ds(start, T), :])
        if bias is not None:
            s = s + jnp.concatenate([bias, bias], axis=0)
        m_old = m_ref[...]
        m_new = jnp.maximum(m_old, jnp.max(s, axis=1, keepdims=True))
        p = jnp.exp2(s - m_new).astype(bf16)
        acc_ref[...] = jnp.exp2(m_old - m_new) * acc_ref[...] + _dot(p, vaug_ref[pl.ds(start, T), :])
        m_ref[...] = m_new

    def far_body(j, carry):
        step(j, None)
        return carry

    lax.fori_loop(0, jnp.maximum(qi - 1, 0), far_body, 0)

    @pl.when(qi >= 1)
    def _():
        step(qi - 1, bias_ref[0, 0])

    step(qi, bias_ref[0, 1])

    al = lam_ref[...]
    lam = (jnp.exp(jnp.sum(al[0:1] * al[1:2], keepdims=True))
           - jnp.exp(jnp.sum(al[2:3] * al[3:4], keepdims=True)) + lam_init)
    acc = acc_ref[...]
    o = acc[:, 0:dv] / acc[:, dv:2 * dv]
    ha = o[0:T] - lam * o[T:2 * T]
    hn = ha * lax.rsqrt(jnp.mean(ha * ha, axis=-1, keepdims=True) + RMS_EPS) * ng_ref[...]
    o_ref[...] = (hn * (1.0 - lam_init)).astype(bf16)


def _diffattn(qn, kn, av, bias, lam_par, ng, batch, seq, lam_init):
    T = A_TILE
    nq = seq // T
    return pl.pallas_call(
        functools.partial(_attn_kernel, lam_init=lam_init),
        grid=(batch, A_HEADS, nq),
        in_specs=[pl.BlockSpec((T, LANES), lambda b, h, i: (b * nq + i, h)),
                  pl.BlockSpec((seq, LANES), lambda b, h, i: (b, h)),
                  pl.BlockSpec((seq, LANES), lambda b, h, i: (b, h)),
                  pl.BlockSpec((1, 2, T, T), lambda b, h, i: (h, 0, 0, 0)),
                  _const_spec(lam_par.shape),
                  pl.BlockSpec((1, LANES), lambda b, h, i: (0, h))],
        out_specs=pl.BlockSpec((T, LANES), lambda b, h, i: (b * nq + i, h)),
        out_shape=jax.ShapeDtypeStruct((batch * seq, A_WIDTH), bf16),
        scratch_shapes=[pltpu.VMEM((seq, 2 * A_V_DIM), bf16),
                        pltpu.VMEM((2 * T, LANES), bf16),
                        pltpu.VMEM((2 * T, 2 * A_V_DIM), f32),
                        pltpu.VMEM((2 * T, 1), f32)],
        compiler_params=pltpu.CompilerParams(
            dimension_semantics=("arbitrary", "arbitrary", "arbitrary"),
            vmem_limit_bytes=VMEM_LIMIT),
        name="diffattn",
    )(qn, kn, av, bias, lam_par, ng)


def _merge_kernel(x_ref, mod_ref, hm_ref, ha_ref, gm_ref, ga_ref, wm_ref, wa_ref, wo_ref, o_ref):
    ym = _dot(hm_ref[...], wm_ref[...])
    ya = _dot(ha_ref[...], wa_ref[...])
    y = _sigmoid(gm_ref[...].astype(f32)) * ym + _sigmoid(ga_ref[...].astype(f32)) * ya
    o_ref[...] = x_ref[...] + mod_ref[0][2:3] * _dot(y.astype(bf16), wo_ref[...])


def _merge(x2, mod3, hm, ha, gm, ga, wm, wa, wo, tiles_per_batch):
    n = x2.shape[0]
    tm = TM_PROJ
    tok = lambda width: pl.BlockSpec((tm, width), lambda i: (i, 0))
    return pl.pallas_call(
        _merge_kernel,
        grid=(n // tm,),
        in_specs=[tok(D_MODEL),
                  pl.BlockSpec((1, 6, D_MODEL), lambda i: (i // tiles_per_batch, 0, 0)),
                  tok(M_WIDTH), tok(A_WIDTH), tok(D_MODEL), tok(D_MODEL),
                  _const_spec(wm.shape), _const_spec(wa.shape), _const_spec(wo.shape)],
        out_specs=tok(D_MODEL),
        out_shape=jax.ShapeDtypeStruct((n, D_MODEL), f32),
        compiler_params=pltpu.CompilerParams(dimension_semantics=("arbitrary",),
                                             vmem_limit_bytes=VMEM_LIMIT),
        name="merge",
    )(x2, mod3, hm, ha, gm, ga, wm, wa, wo)


def _ffn_kernel(x_ref, mod_ref, g_ref, wv_ref, wg_ref, cwv_ref, cwg_ref, cbv_ref, cbg_ref, wd_ref,
                o_ref, hv_ref, hg_ref, acc_ref, *, tiles_per_batch):
    tm = x_ref.shape[0]

    @pl.when(pl.program_id(0) % tiles_per_batch == 0)
    def _():
        hv_ref[...] = jnp.zeros_like(hv_ref)
        hg_ref[...] = jnp.zeros_like(hg_ref)

    x = x_ref[...]
    mod = mod_ref[0]
    ms = jnp.mean(x * x, axis=-1, keepdims=True)
    y = x * lax.rsqrt(ms + RMS_EPS) * g_ref[...]
    hb = (y * (1.0 + mod[4:5]) + mod[3:4]).astype(bf16)

    def conv(u, prev8, w, b):
        return w[2:3] * u + w[1:2] * _shift_rows(u, prev8, 1) + w[0:1] * _shift_rows(u, prev8, 2) + b

    for j in range(N_FCHUNK):
        uv = _dot(hb, wv_ref[j])
        ug = _dot(hb, wg_ref[j])
        cv = conv(uv, hv_ref[j], cwv_ref[j], cbv_ref[j])
        cg = conv(ug, hg_ref[j], cwg_ref[j], cbg_ref[j])
        hv_ref[j] = uv[tm - SUBLANES:, :]
        hg_ref[j] = ug[tm - SUBLANES:, :]
        act = ((cg * _sigmoid(cg)) * cv).astype(bf16)
        contrib = _dot(act, wd_ref[j])
        if j == 0:
            acc_ref[...] = contrib
        else:
            acc_ref[...] += contrib
    o_ref[...] = x + mod[5:6] * acc_ref[...]


def _ffn(x2, mod3, g2, wv, wg, cwv, cwg, cbv, cbg, wd, tiles_per_batch):
    n = x2.shape[0]
    tm = TM_PROJ
    tok = lambda width: pl.BlockSpec((tm, width), lambda i: (i, 0))
    return pl.pallas_call(
        functools.partial(_ffn_kernel, tiles_per_batch=tiles_per_batch),
        grid=(n // tm,),
        in_specs=[tok(D_MODEL),
                  pl.BlockSpec((1, 6, D_MODEL), lambda i: (i // tiles_per_batch, 0, 0)),
                  _const_spec(g2.shape), _const_spec(wv.shape), _const_spec(wg.shape),
                  _const_spec(cwv.shape), _const_spec(cwg.shape), _const_spec(cbv.shape),
                  _const_spec(cbg.shape), _const_spec(wd.shape)],
        out_specs=tok(D_MODEL),
        out_shape=jax.ShapeDtypeStruct((n, D_MODEL), f32),
        scratch_shapes=[pltpu.VMEM((N_FCHUNK, SUBLANES, F_CHUNK), f32),
                        pltpu.VMEM((N_FCHUNK, SUBLANES, F_CHUNK), f32),
                        pltpu.VMEM((tm, D_MODEL), f32)],
        compiler_params=pltpu.CompilerParams(dimension_semantics=("arbitrary",),
                                             vmem_limit_bytes=VMEM_LIMIT),
        name="ffn",
    )(x2, mod3, g2, wv, wg, cwv, cwg, cbv, cbg, wd)


def _pad_cols(a, width):
    return jnp.pad(a, ((0, 0), (0, width - a.shape[1])))


def _layer(x2, c8, batch, seq, layer, w_ada, b_ada, norm1_g, w_in, m_conv_w, m_conv_b, m_igate_b,
           m_fgate_b, m_norm_g, a_qnorm_g, a_knorm_g, a_lambda, a_norm_g, bias, w_branch_m,
           w_branch_a, w_out, norm2_g, w_up, ffn_conv_w, ffn_conv_b, w_down):
    tiles_per_batch = seq // TM_PROJ
    mod3 = _adaln(c8, w_ada, b_ada.reshape(1, -1))[:batch].reshape(batch, 6, D_MODEL)

    o = 0
    parts = {}
    for name, size in (("mqk", 2 * M_WIDTH), ("mv", M_WIDTH), ("mo", M_WIDTH), ("mi", M_HEADS),
                       ("mf", M_HEADS), ("aq", A_WIDTH), ("ak", A_WIDTH), ("av", A_WIDTH),
                       ("gm", D_MODEL), ("ga", D_MODEL)):
        parts[name] = w_in[:, o:o + size]
        o += size

    def per_head(w):
        return w.reshape(D_MODEL, 2, A_HEADS, A_QK_DIM).transpose(0, 2, 1, 3).reshape(D_MODEL, A_WIDTH)

    w_cat = jnp.concatenate(
        [parts["mqk"], parts["mv"], parts["mo"], per_head(parts["aq"]), per_head(parts["ak"]),
         parts["av"], parts["gm"], parts["ga"], _pad_cols(parts["mi"], LANES),
         _pad_cols(parts["mf"], LANES)], axis=1).astype(bf16)
    gid = jnp.arange(A_WIDTH) // A_QK_DIM
    grp = jnp.where(gid[:, None] == gid[None, :], 1.0 / A_QK_DIM, 0.0).astype(bf16)
    qg = (jnp.tile(a_qnorm_g, A_WIDTH // A_QK_DIM) * (A_QK_DIM ** -0.5 * LOG2E)).reshape(1, A_WIDTH)
    kg = jnp.tile(a_knorm_g, A_WIDTH // A_QK_DIM).reshape(1, A_WIDTH)

    mqk, mv, mo, qn, kn, av, gm, ga, gates = _inproj(
        x2, mod3, norm1_g.reshape(1, -1), w_cat, grp, qg, kg, tiles_per_batch)

    hm = _mlstm(mqk, mv, mo, gates, m_conv_w, m_conv_b.reshape(1, -1),
                _pad_cols(m_igate_b.reshape(1, -1), LANES), _pad_cols(m_fgate_b.reshape(1, -1), LANES),
                m_norm_g.reshape(1, -1), batch, seq)

    lam_init = 0.8 - 0.6 * math.exp(-0.3 * layer)
    ha = _diffattn(qn, kn, av, bias, a_lambda, a_norm_g.reshape(1, -1), batch, seq, lam_init)

    x1 = _merge(x2, mod3, hm, ha, gm, ga, w_branch_m.astype(bf16), w_branch_a.astype(bf16),
                w_out.astype(bf16), tiles_per_batch)

    def chunks(a):
        return a.reshape(a.shape[0], N_FCHUNK, F_CHUNK).transpose(1, 0, 2)

    wv = chunks(w_up[:, :D_FF]).astype(bf16)
    wg = chunks(w_up[:, D_FF:]).astype(bf16)
    cwv = chunks(ffn_conv_w[:, :D_FF])
    cwg = chunks(ffn_conv_w[:, D_FF:])
    cbv = chunks(ffn_conv_b[None, :D_FF])
    cbg = chunks(ffn_conv_b[None, D_FF:])
    wd = w_down.reshape(N_FCHUNK, F_CHUNK, D_MODEL).astype(bf16)
    return _ffn(x1, mod3, norm2_g.reshape(1, -1), wv, wg, cwv, cwg, cbv, cbg, wd, tiles_per_batch)


def kernel(x, c, w_ada, b_ada, norm1_g, w_in, m_conv_w, m_conv_b, m_igate_b, m_fgate_b, m_norm_g,
           a_qnorm_g, a_knorm_g, a_lambda, a_norm_g, rel_bias, w_branch_m, w_branch_a, w_out, norm2_g,
           w_up, ffn_conv_w, ffn_conv_b, w_down):
    batch, seq, _ = x.shape
    depth = w_ada.shape[0]
    x2 = x.reshape(batch * seq, D_MODEL)
    c8 = jnp.pad(c, ((0, SUBLANES - batch), (0, 0)))
    bias = _bias_tiles(rel_bias.astype(f32).T)
    for l in range(depth):
        x2 = _layer(x2, c8, batch, seq, l, w_ada[l], b_ada[l], norm1_g[l], w_in[l], m_conv_w[l],
                    m_conv_b[l], m_igate_b[l], m_fgate_b[l], m_norm_g[l], a_qnorm_g[l], a_knorm_g[l],
                    a_lambda[l], a_norm_g[l], bias, w_branch_m[l], w_branch_a[l], w_out[l], norm2_g[l],
                    w_up[l], ffn_conv_w[l], ffn_conv_b[l], w_down[l])
    return x2.reshape(batch, seq, D_MODEL)
```

```python
import functools
import math

import jax
import jax.numpy as jnp
from jax import lax
from jax.experimental import pallas as pl
from jax.experimental.pallas import tpu as pltpu

D_MODEL = 1024
M_HEADS = 4
M_HEAD_DIM = 128
M_WIDTH = M_HEADS * M_HEAD_DIM
M_CONV = 4
A_HEADS = 4
A_QK_DIM = 64
A_V_DIM = 2 * A_QK_DIM
A_WIDTH = A_HEADS * A_V_DIM
N_BUCKETS = 32
MAX_DISTANCE = 128
D_FF = 2816
FFN_CONV = 3
RMS_EPS = 1e-6
LOG2E = math.log2(math.e)
NEG_BIG = -1e30

LANES = 128
SUBLANES = 8
VMEM_LIMIT = 56 * 1024 * 1024

TM_PROJ = 512
M_CHUNK = 256
A_TILE = 512
A_ROWS = 256
A_NORM_ROWS = 1024
A_SAFE_RANGE = 90.0
F_CHUNK = 256
N_FCHUNK = D_FF // F_CHUNK

bf16 = jnp.bfloat16
f32 = jnp.float32


def _dot(a, b):
    return jnp.dot(a, b, preferred_element_type=f32)


def _dot_nt(a, b):
    return lax.dot_general(a, b, (((1,), (1,)), ((), ())), preferred_element_type=f32)


def _sigmoid(x):
    return 1.0 / (1.0 + jnp.exp(-x))


def _const_spec(shape):
    nd = len(shape)
    return pl.BlockSpec(shape, lambda *_: (0,) * nd)


def _shift_rows(u, prev8, k):
    r = pltpu.roll(u, k, 0)
    rp = pltpu.roll(prev8, k, 0)
    row = lax.broadcasted_iota(jnp.int32, (SUBLANES, u.shape[1]), 0)
    first = jnp.where(row < k, rp, r[:SUBLANES])
    return jnp.concatenate([first, r[SUBLANES:]], axis=0)


def _adaln_kernel(c_ref, w_ref, b_ref, o_ref):
    c = c_ref[...]
    a = (c * _sigmoid(c)).astype(bf16)
    o_ref[...] = _dot(a, w_ref[...].astype(bf16)) + b_ref[...]


def _adaln(c8, w, b):
    n = w.shape[1]
    tn = 1536
    return pl.pallas_call(
        _adaln_kernel,
        grid=(n // tn,),
        in_specs=[_const_spec(c8.shape),
                  pl.BlockSpec((D_MODEL, tn), lambda j: (0, j)),
                  pl.BlockSpec((1, tn), lambda j: (0, j))],
        out_specs=pl.BlockSpec((c8.shape[0], tn), lambda j: (0, j)),
        out_shape=jax.ShapeDtypeStruct((c8.shape[0], n), f32),
        compiler_params=pltpu.CompilerParams(dimension_semantics=("arbitrary",),
                                             vmem_limit_bytes=VMEM_LIMIT),
        name="adaln",
    )(c8, w, b)


def _bias_kernel(tab_ref, o_ref):
    h = pl.program_id(0)
    t = pl.program_id(1)
    row = lax.broadcasted_iota(jnp.int32, (A_TILE, A_TILE), 0)
    col = lax.broadcasted_iota(jnp.int32, (A_TILE, A_TILE), 1)
    dist = row - col + (1 - t) * A_TILE
    n = jnp.maximum(dist, 0)
    max_exact = N_BUCKETS // 2
    nf = jnp.maximum(n, 1).astype(f32)
    large = max_exact + (jnp.log(nf / max_exact) / math.log(MAX_DISTANCE / max_exact)
                         * (N_BUCKETS - max_exact)).astype(jnp.int32)
    large = jnp.minimum(large, N_BUCKETS - 1)
    bucket = jnp.where(n < max_exact, n, large)
    far = tab_ref[h, N_BUCKETS - 1]
    val = jnp.zeros((A_TILE, A_TILE), f32)
    for b in range(N_BUCKETS - 1):
        val = jnp.where(bucket == b, tab_ref[h, b] - far, val)
    o_ref[0, 0] = jnp.where(dist >= 0, val * LOG2E, NEG_BIG)


def _bias_tiles(tab):
    return pl.pallas_call(
        _bias_kernel,
        grid=(A_HEADS, 2),
        in_specs=[pl.BlockSpec(memory_space=pltpu.SMEM)],
        out_specs=pl.BlockSpec((1, 1, A_TILE, A_TILE), lambda h, t: (h, t, 0, 0)),
        out_shape=jax.ShapeDtypeStruct((A_HEADS, 2, A_TILE, A_TILE), f32),
        compiler_params=pltpu.CompilerParams(dimension_semantics=("arbitrary", "arbitrary"),
                                             vmem_limit_bytes=VMEM_LIMIT),
        name="bias_tiles",
    )(tab)


C_MQK, C_MV, C_MO, C_AQ, C_AK, C_AV, C_GM, C_GA, C_GATE, C_END = (
    0, 1024, 1536, 2048, 2560, 3072, 3584, 4608, 5632, 5888)


def _inproj_kernel(x_ref, mod_ref, g_ref, w_ref, grp_ref, qg_ref, kg_ref,
                   mqk_ref, mv_ref, mo_ref, qn_ref, kn_ref, av_ref, gm_ref, ga_ref, gate_ref):
    x = x_ref[...]
    mod = mod_ref[0]
    ms = jnp.mean(x * x, axis=-1, keepdims=True)
    y = x * lax.rsqrt(ms + RMS_EPS) * g_ref[...]
    hb = (y * (1.0 + mod[1:2]) + mod[0:1]).astype(bf16)

    def proj(c0, c1):
        return _dot(hb, w_ref[:, c0:c1])

    def qknorm(a, gain_ref):
        msq = _dot((a * a).astype(bf16), grp_ref[...])
        return (a * lax.rsqrt(msq + RMS_EPS) * gain_ref[...]).astype(bf16)

    mqk_ref[:, 0:512] = proj(C_MQK, C_MQK + 512).astype(bf16)
    mqk_ref[:, 512:1024] = proj(C_MQK + 512, C_MV).astype(bf16)
    mv_ref[...] = proj(C_MV, C_MO).astype(bf16)
    mo_ref[...] = proj(C_MO, C_AQ).astype(bf16)
    qn_ref[...] = qknorm(proj(C_AQ, C_AK), qg_ref)
    kn_ref[...] = qknorm(proj(C_AK, C_AV), kg_ref)
    av_ref[...] = proj(C_AV, C_GM).astype(bf16)
    gm_ref[:, 0:512] = proj(C_GM, C_GM + 512).astype(bf16)
    gm_ref[:, 512:1024] = proj(C_GM + 512, C_GA).astype(bf16)
    ga_ref[:, 0:512] = proj(C_GA, C_GA + 512).astype(bf16)
    ga_ref[:, 512:1024] = proj(C_GA + 512, C_GATE).astype(bf16)
    gate_ref[...] = proj(C_GATE, C_END)


def _inproj(x2, mod3, g1, w, grp, qg, kg, tiles_per_batch):
    n = x2.shape[0]
    tm = TM_PROJ
    tok = lambda width: pl.BlockSpec((tm, width), lambda i: (i, 0))
    outs = [(1024, bf16), (512, bf16), (512, bf16), (512, bf16), (512, bf16), (512, bf16),
            (1024, bf16), (1024, bf16), (2 * LANES, f32)]
    return pl.pallas_call(
        _inproj_kernel,
        grid=(n // tm,),
        in_specs=[tok(D_MODEL),
                  pl.BlockSpec((1, 6, D_MODEL), lambda i: (i // tiles_per_batch, 0, 0)),
                  _const_spec(g1.shape), _const_spec(w.shape), _const_spec(grp.shape),
                  _const_spec(qg.shape), _const_spec(kg.shape)],
        out_specs=[tok(wd) for wd, _ in outs],
        out_shape=[jax.ShapeDtypeStruct((n, wd), dt) for wd, dt in outs],
        compiler_params=pltpu.CompilerParams(dimension_semantics=("arbitrary",),
                                             vmem_limit_bytes=VMEM_LIMIT),
        name="inproj",
    )(x2, mod3, g1, w, grp, qg, kg)


def _mlstm_kernel(mqk_ref, mv_ref, mo_ref, gate_ref, cw_ref, cb_ref, bi_ref, bfg_ref, mg_ref,
                  o_ref, c_st, m_st, tail_ref):
    L = M_CHUNK
    d = M_HEAD_DIM

    @pl.when(pl.program_id(1) == 0)
    def _():
        c_st[...] = jnp.zeros_like(c_st)
        m_st[...] = jnp.zeros_like(m_st)
        tail_ref[...] = jnp.zeros_like(tail_ref)

    x = mqk_ref[...].astype(f32)
    prev8 = tail_ref[...]
    cw = cw_ref[...]
    y = (cw[3:4] * x + cw[2:3] * _shift_rows(x, prev8, 1) + cw[1:2] * _shift_rows(x, prev8, 2)
         + cw[0:1] * _shift_rows(x, prev8, 3) + cb_ref[...])
    tail_ref[...] = x[L - SUBLANES:, :]
    qk = y * _sigmoid(y)

    gi = gate_ref[:, 0:LANES] + bi_ref[...]
    gf = gate_ref[:, LANES:2 * LANES] + bfg_ref[...]
    logf = jnp.minimum(gf, 0.0) - jnp.log(1.0 + jnp.exp(-jnp.abs(gf)))
    row = lax.broadcasted_iota(jnp.int32, (L, L), 0)
    col = lax.broadcasted_iota(jnp.int32, (L, L), 1)
    causal = row >= col
    tri = jnp.where(causal, 1.0, 0.0).astype(bf16)
    logf_hi = logf.astype(bf16)
    logf_lo = (logf - logf_hi.astype(f32)).astype(bf16)
    bcum = _dot(tri, logf_hi) + _dot(tri, logf_lo)
    r = gi - bcum
    r_t = r.T
    b_last = bcum[L - 1:L, :]
    g = b_last + r
    m_loc = jnp.max(g, axis=0, keepdims=True)
    ones = jnp.ones((L, d), bf16)

    for h in range(M_HEADS):
        hs = slice(h * d, (h + 1) * d)
        qh = qk[:, hs].astype(bf16)
        kh = qk[:, M_WIDTH + h * d:M_WIDTH + (h + 1) * d] * (d ** -0.5)
        vaug = jnp.concatenate([mv_ref[:, hs], ones], axis=1)
        m_in = m_st[h][0:1, 0:1]
        b_col = bcum[:, h:h + 1]
        dmat = jnp.where(causal, b_col + r_t[h:h + 1, :], NEG_BIG)
        a_t = b_col + m_in
        m_t = jnp.maximum(a_t, jnp.max(dmat, axis=1, keepdims=True))
        inter_w = jnp.exp(a_t - m_t)
        s = _dot_nt(qh, kh.astype(bf16)) * jnp.exp(dmat - m_t)
        tot = _dot(s.astype(bf16), vaug) + inter_w * _dot(qh, c_st[h].astype(bf16))
        den = jnp.maximum(jnp.abs(tot[:, d:]), jnp.exp(-m_t))
        hh = tot[:, :d] / den

        bl = b_last[:, h:h + 1]
        m_new = jnp.maximum(bl + m_in, m_loc[:, h:h + 1])
        w_col = jnp.exp(g[:, h:h + 1] - m_new)
        kw_t = (kh * w_col).T.astype(bf16)
        c_st[h] = jnp.exp(bl + m_in - m_new) * c_st[h] + _dot(kw_t, vaug)
        m_st[h] = jnp.broadcast_to(m_new, (SUBLANES, LANES))

        hn = hh * lax.rsqrt(jnp.mean(hh * hh, axis=-1, keepdims=True) + RMS_EPS) * mg_ref[:, hs]
        o_ref[:, hs] = (_sigmoid(mo_ref[:, hs].astype(f32)) * hn).astype(bf16)


def _mlstm(mqk, mv, mo, gates, cw, cb, bi, bfg, mg, batch, seq):
    L = M_CHUNK
    nc = seq // L
    tok = lambda width: pl.BlockSpec((L, width), lambda b, c: (b * nc + c, 0))
    return pl.pallas_call(
        _mlstm_kernel,
        grid=(batch, nc),
        in_specs=[tok(2 * M_WIDTH), tok(M_WIDTH), tok(M_WIDTH), tok(2 * LANES),
                  _const_spec(cw.shape), _const_spec(cb.shape), _const_spec(bi.shape),
                  _const_spec(bfg.shape), _const_spec(mg.shape)],
        out_specs=tok(M_WIDTH),
        out_shape=jax.ShapeDtypeStruct((batch * seq, M_WIDTH), bf16),
        scratch_shapes=[pltpu.VMEM((M_HEADS, M_HEAD_DIM, 2 * M_HEAD_DIM), f32),
                        pltpu.VMEM((M_HEADS, SUBLANES, LANES), f32),
                        pltpu.VMEM((SUBLANES, 2 * M_WIDTH), f32)],
        compiler_params=pltpu.CompilerParams(dimension_semantics=("arbitrary", "arbitrary"),
                                             vmem_limit_bytes=VMEM_LIMIT),
        name="mlstm",
    )(mqk, mv, mo, gates, cw, cb, bi, bfg, mg)


def _attn_kernel(tab_ref, q_ref, k_ref, v_ref, bias_ref, lam_ref, ng_ref, o_ref,
                 vaug_ref, q2_ref, acc_ref, m_ref, mfix_ref, kmax_ref, *, lam_init):
    T = A_TILE
    dv = A_V_DIM
    h = pl.program_id(1)
    qi = pl.program_id(2)
    sel = jnp.where((lax.broadcasted_iota(jnp.int32, (LANES, 2 * LANES), 0) < A_QK_DIM)
                    == (lax.broadcasted_iota(jnp.int32, (LANES, 2 * LANES), 1) < LANES), 1.0, 0.0).astype(bf16)

    def sq_norms(x):
        xf = x.astype(f32)
        return _dot((xf * xf).astype(bf16), sel)

    @pl.when(qi == 0)
    def _():
        vaug_ref[:, 0:dv] = v_ref[...]
        vaug_ref[:, dv:2 * dv] = jnp.ones((v_ref.shape[0], dv), bf16)
        kmax = jnp.zeros((1, 2 * LANES), f32)
        for c in range(k_ref.shape[0] // A_NORM_ROWS):
            kn2 = sq_norms(k_ref[c * A_NORM_ROWS:(c + 1) * A_NORM_ROWS, :])
            kmax = jnp.maximum(kmax, jnp.max(kn2, axis=0, keepdims=True))
        kmax_ref[...] = kmax

    q = q_ref[...]
    lane = lax.broadcasted_iota(jnp.int32, q.shape, 1)
    zero = jnp.zeros_like(q)
    q2_ref[0:T, :] = jnp.where(lane < A_QK_DIM, q, zero)
    q2_ref[T:2 * T, :] = jnp.where(lane >= A_QK_DIM, q, zero)
    acc_ref[...] = jnp.zeros_like(acc_ref)

    bmax = jnp.float32(0.0)
    bmin = jnp.float32(0.0)
    for b in range(N_BUCKETS - 1):
        rel = (tab_ref[h, b] - tab_ref[h, N_BUCKETS - 1]) * LOG2E
        bmax = jnp.maximum(bmax, rel)
        bmin = jnp.minimum(bmin, rel)
    bound = jnp.sqrt(sq_norms(q) * kmax_ref[...])
    mfix_ref[0:T, :] = bound[:, 0:LANES] + bmax
    mfix_ref[T:2 * T, :] = bound[:, LANES:2 * LANES] + bmax
    safe = 2.0 * jnp.max(bound) + (bmax - bmin) < A_SAFE_RANGE

    def step(j, bias, fixed_shift):
        start = pl.multiple_of(j * T, T)
        kt = k_ref[pl.ds(start, T), :]
        va = vaug_ref[pl.ds(start, T), :]
        for rb in range(2 * T // A_ROWS):
            rows = slice(rb * A_ROWS, (rb + 1) * A_ROWS)
            s = _dot_nt(q2_ref[rows, :], kt)
            if bias is not None:
                brow = (rb * A_ROWS) % T
                s = s + bias_ref[0, bias, brow:brow + A_ROWS, :]
            if fixed_shift:
                mrow = mfix_ref[rows, :]
                p = jnp.exp2(s - jnp.concatenate([mrow] * (T // LANES), axis=1)).astype(bf16)
                acc_ref[rows, :] += _dot(p, va)
            else:
                m_old = m_ref[rows, :]
                m_new = jnp.maximum(m_old, jnp.max(s, axis=1, keepdims=True))
                p = jnp.exp2(s - m_new).astype(bf16)
                acc_ref[rows, :] = jnp.exp2(m_old - m_new) * acc_ref[rows, :] + _dot(p, va)
                m_ref[rows, :] = m_new

    def sweep(fixed_shift):
        def far_body(j, carry):
            step(j, None, fixed_shift)
            return carry

        lax.fori_loop(0, jnp.maximum(qi - 1, 0), far_body, 0)

        @pl.when(qi >= 1)
        def _():
            step(qi - 1, 0, fixed_shift)

        step(qi, 1, fixed_shift)

    @pl.when(safe)
    def _():
        sweep(True)

    @pl.when(jnp.logical_not(safe))
    def _():
        m_ref[...] = jnp.full_like(m_ref, NEG_BIG)
        sweep(False)

    al = lam_ref[...]
    lam = (jnp.exp(jnp.sum(al[0:1] * al[1:2], keepdims=True))
           - jnp.exp(jnp.sum(al[2:3] * al[3:4], keepdims=True)) + lam_init)
    acc = acc_ref[...]
    o = acc[:, 0:dv] / acc[:, dv:2 * dv]
    ha = o[0:T] - lam * o[T:2 * T]
    hn = ha * lax.rsqrt(jnp.mean(ha * ha, axis=-1, keepdims=True) + RMS_EPS) * ng_ref[...]
    o_ref[...] = (hn * (1.0 - lam_init)).astype(bf16)


def _diffattn(tab, qn, kn, av, bias, lam_par, ng, batch, seq, lam_init):
    T = A_TILE
    nq = seq // T
    return pl.pallas_call(
        functools.partial(_attn_kernel, lam_init=lam_init),
        grid=(batch, A_HEADS, nq),
        in_specs=[pl.BlockSpec(memory_space=pltpu.SMEM),
                  pl.BlockSpec((T, LANES), lambda b, h, i: (b * nq + i, h)),
                  pl.BlockSpec((seq, LANES), lambda b, h, i: (b, h)),
                  pl.BlockSpec((seq, LANES), lambda b, h, i: (b, h)),
                  pl.BlockSpec((1, 2, T, T), lambda b, h, i: (h, 0, 0, 0)),
                  _const_spec(lam_par.shape),
                  pl.BlockSpec((1, LANES), lambda b, h, i: (0, h))],
        out_specs=pl.BlockSpec((T, LANES), lambda b, h, i: (b * nq + i, h)),
        out_shape=jax.ShapeDtypeStruct((batch * seq, A_WIDTH), bf16),
        scratch_shapes=[pltpu.VMEM((seq, 2 * A_V_DIM), bf16),
                        pltpu.VMEM((2 * T, LANES), bf16),
                        pltpu.VMEM((2 * T, 2 * A_V_DIM), f32),
                        pltpu.VMEM((2 * T, 1), f32),
                        pltpu.VMEM((2 * T, LANES), f32),
                        pltpu.VMEM((1, 2 * LANES), f32)],
        compiler_params=pltpu.CompilerParams(
            dimension_semantics=("arbitrary", "arbitrary", "arbitrary"),
            vmem_limit_bytes=VMEM_LIMIT),
        name="diffattn",
    )(tab, qn, kn, av, bias, lam_par, ng)


def _merge_kernel(x_ref, mod_ref, hm_ref, ha_ref, gm_ref, ga_ref, wm_ref, wa_ref, wo_ref, o_ref):
    ym = _dot(hm_ref[...], wm_ref[...])
    ya = _dot(ha_ref[...], wa_ref[...])
    y = _sigmoid(gm_ref[...].astype(f32)) * ym + _sigmoid(ga_ref[...].astype(f32)) * ya
    o_ref[...] = x_ref[...] + mod_ref[0][2:3] * _dot(y.astype(bf16), wo_ref[...])


def _merge(x2, mod3, hm, ha, gm, ga, wm, wa, wo, tiles_per_batch):
    n = x2.shape[0]
    tm = TM_PROJ
    tok = lambda width: pl.BlockSpec((tm, width), lambda i: (i, 0))
    return pl.pallas_call(
        _merge_kernel,
        grid=(n // tm,),
        in_specs=[tok(D_MODEL),
                  pl.BlockSpec((1, 6, D_MODEL), lambda i: (i // tiles_per_batch, 0, 0)),
                  tok(M_WIDTH), tok(A_WIDTH), tok(D_MODEL), tok(D_MODEL),
                  _const_spec(wm.shape), _const_spec(wa.shape), _const_spec(wo.shape)],
        out_specs=tok(D_MODEL),
        out_shape=jax.ShapeDtypeStruct((n, D_MODEL), f32),
        compiler_params=pltpu.CompilerParams(dimension_semantics=("arbitrary",),
                                             vmem_limit_bytes=VMEM_LIMIT),
        name="merge",
    )(x2, mod3, hm, ha, gm, ga, wm, wa, wo)


def _ffn_kernel(x_ref, mod_ref, g_ref, wv_ref, wg_ref, cwv_ref, cwg_ref, cbv_ref, cbg_ref, wd_ref,
                o_ref, hv_ref, hg_ref, acc_ref, *, tiles_per_batch):
    tm = x_ref.shape[0]

    @pl.when(pl.program_id(0) % tiles_per_batch == 0)
    def _():
        hv_ref[...] = jnp.zeros_like(hv_ref)
        hg_ref[...] = jnp.zeros_like(hg_ref)

    x = x_ref[...]
    mod = mod_ref[0]
    ms = jnp.mean(x * x, axis=-1, keepdims=True)
    y = x * lax.rsqrt(ms + RMS_EPS) * g_ref[...]
    hb = (y * (1.0 + mod[4:5]) + mod[3:4]).astype(bf16)

    def conv(u, prev8, w, b):
        return w[2:3] * u + w[1:2] * _shift_rows(u, prev8, 1) + w[0:1] * _shift_rows(u, prev8, 2) + b

    for j in range(N_FCHUNK):
        uv = _dot(hb, wv_ref[j])
        ug = _dot(hb, wg_ref[j])
        cv = conv(uv, hv_ref[j], cwv_ref[j], cbv_ref[j])
        cg = conv(ug, hg_ref[j], cwg_ref[j], cbg_ref[j])
        hv_ref[j] = uv[tm - SUBLANES:, :]
        hg_ref[j] = ug[tm - SUBLANES:, :]
        act = ((cg * _sigmoid(cg)) * cv).astype(bf16)
        contrib = _dot(act, wd_ref[j])
        if j == 0:
            acc_ref[...] = contrib
        else:
            acc_ref[...] += contrib
    o_ref[...] = x + mod[5:6] * acc_ref[...]


def _ffn(x2, mod3, g2, wv, wg, cwv, cwg, cbv, cbg, wd, tiles_per_batch):
    n = x2.shape[0]
    tm = TM_PROJ
    tok = lambda width: pl.BlockSpec((tm, width), lambda i: (i, 0))
    return pl.pallas_call(
        functools.partial(_ffn_kernel, tiles_per_batch=tiles_per_batch),
        grid=(n // tm,),
        in_specs=[tok(D_MODEL),
                  pl.BlockSpec((1, 6, D_MODEL), lambda i: (i // tiles_per_batch, 0, 0)),
                  _const_spec(g2.shape), _const_spec(wv.shape), _const_spec(wg.shape),
                  _const_spec(cwv.shape), _const_spec(cwg.shape), _const_spec(cbv.shape),
                  _const_spec(cbg.shape), _const_spec(wd.shape)],
        out_specs=tok(D_MODEL),
        out_shape=jax.ShapeDtypeStruct((n, D_MODEL), f32),
        scratch_shapes=[pltpu.VMEM((N_FCHUNK, SUBLANES, F_CHUNK), f32),
                        pltpu.VMEM((N_FCHUNK, SUBLANES, F_CHUNK), f32),
                        pltpu.VMEM((tm, D_MODEL), f32)],
        compiler_params=pltpu.CompilerParams(dimension_semantics=("arbitrary",),
                                             vmem_limit_bytes=VMEM_LIMIT),
        name="ffn",
    )(x2, mod3, g2, wv, wg, cwv, cwg, cbv, cbg, wd)


def _pad_cols(a, width):
    return jnp.pad(a, ((0, 0), (0, width - a.shape[1])))


def _layer(x2, c8, batch, seq, layer, w_ada, b_ada, norm1_g, w_in, m_conv_w, m_conv_b, m_igate_b,
           m_fgate_b, m_norm_g, a_qnorm_g, a_knorm_g, a_lambda, a_norm_g, tab, bias, w_branch_m,
           w_branch_a, w_out, norm2_g, w_up, ffn_conv_w, ffn_conv_b, w_down):
    tiles_per_batch = seq // TM_PROJ
    mod3 = _adaln(c8, w_ada, b_ada.reshape(1, -1))[:batch].reshape(batch, 6, D_MODEL)

    o = 0
    parts = {}
    for name, size in (("mqk", 2 * M_WIDTH), ("mv", M_WIDTH), ("mo", M_WIDTH), ("mi", M_HEADS),
                       ("mf", M_HEADS), ("aq", A_WIDTH), ("ak", A_WIDTH), ("av", A_WIDTH),
                       ("gm", D_MODEL), ("ga", D_MODEL)):
        parts[name] = w_in[:, o:o + size]
        o += size

    def per_head(w):
        return w.reshape(D_MODEL, 2, A_HEADS, A_QK_DIM).transpose(0, 2, 1, 3).reshape(D_MODEL, A_WIDTH)

    w_cat = jnp.concatenate(
        [parts["mqk"], parts["mv"], parts["mo"], per_head(parts["aq"]), per_head(parts["ak"]),
         parts["av"], parts["gm"], parts["ga"], _pad_cols(parts["mi"], LANES),
         _pad_cols(parts["mf"], LANES)], axis=1).astype(bf16)
    gid = jnp.arange(A_WIDTH) // A_QK_DIM
    grp = jnp.where(gid[:, None] == gid[None, :], 1.0 / A_QK_DIM, 0.0).astype(bf16)
    qg = (jnp.tile(a_qnorm_g, A_WIDTH // A_QK_DIM) * (A_QK_DIM ** -0.5 * LOG2E)).reshape(1, A_WIDTH)
    kg = jnp.tile(a_knorm_g, A_WIDTH // A_QK_DIM).reshape(1, A_WIDTH)

    mqk, mv, mo, qn, kn, av, gm, ga, gates = _inproj(
        x2, mod3, norm1_g.reshape(1, -1), w_cat, grp, qg, kg, tiles_per_batch)

    hm = _mlstm(mqk, mv, mo, gates, m_conv_w, m_conv_b.reshape(1, -1),
                _pad_cols(m_igate_b.reshape(1, -1), LANES), _pad_cols(m_fgate_b.reshape(1, -1), LANES),
                m_norm_g.reshape(1, -1), batch, seq)

    lam_init = 0.8 - 0.6 * math.exp(-0.3 * layer)
    ha = _diffattn(tab, qn, kn, av, bias, a_lambda, a_norm_g.reshape(1, -1), batch, seq, lam_init)

    x1 = _merge(x2, mod3, hm, ha, gm, ga, w_branch_m.astype(bf16), w_branch_a.astype(bf16),
                w_out.astype(bf16), tiles_per_batch)

    def chunks(a):
        return a.reshape(a.shape[0], N_FCHUNK, F_CHUNK).transpose(1, 0, 2)

    wv = chunks(w_up[:, :D_FF]).astype(bf16)
    wg = chunks(w_up[:, D_FF:]).astype(bf16)
    cwv = chunks(ffn_conv_w[:, :D_FF])
    cwg = chunks(ffn_conv_w[:, D_FF:])
    cbv = chunks(ffn_conv_b[None, :D_FF])
    cbg = chunks(ffn_conv_b[None, D_FF:])
    wd = w_down.reshape(N_FCHUNK, F_CHUNK, D_MODEL).astype(bf16)
    return _ffn(x1, mod3, norm2_g.reshape(1, -1), wv, wg, cwv, cwg, cbv, cbg, wd, tiles_per_batch)


def kernel(x, c, w_ada, b_ada, norm1_g, w_in, m_conv_w, m_conv_b, m_igate_b, m_fgate_b, m_norm_g,
           a_qnorm_g, a_knorm_g, a_lambda, a_norm_g, rel_bias, w_branch_m, w_branch_a, w_out, norm2_g,
           w_up, ffn_conv_w, ffn_conv_b, w_down):
    batch, seq, _ = x.shape
    depth = w_ada.shape[0]
    x2 = x.reshape(batch * seq, D_MODEL)
    c8 = jnp.pad(c, ((0, SUBLANES - batch), (0, 0)))
    tab = rel_bias.astype(f32).T
    bias = _bias_tiles(tab)
    for l in range(depth):
        x2 = _layer(x2, c8, batch, seq, l, w_ada[l], b_ada[l], norm1_g[l], w_in[l], m_conv_w[l],
                    m_conv_b[l], m_igate_b[l], m_fgate_b[l], m_norm_g[l], a_qnorm_g[l], a_knorm_g[l],
                    a_lambda[l], a_norm_g[l], tab, bias, w_branch_m[l], w_branch_a[l], w_out[l], norm2_g[l],
                    w_up[l], ffn_conv_w[l], ffn_conv_b[l], w_down[l])
    return x2.reshape(batch, seq, D_MODEL)
```

```python
import functools
import math

import jax
import jax.numpy as jnp
from jax import lax
from jax.experimental import pallas as pl
from jax.experimental.pallas import tpu as pltpu

D_MODEL = 1024
M_HEADS = 4
M_HEAD_DIM = 128
M_WIDTH = M_HEADS * M_HEAD_DIM
M_CONV = 4
A_HEADS = 4
A_QK_DIM = 64
A_V_DIM = 2 * A_QK_DIM
A_WIDTH = A_HEADS * A_V_DIM
N_BUCKETS = 32
MAX_DISTANCE = 128
D_FF = 2816
FFN_CONV = 3
RMS_EPS = 1e-6
LOG2E = math.log2(math.e)
NEG_BIG = -1e30

LANES = 128
SUBLANES = 8
VMEM_LIMIT = 56 * 1024 * 1024

TM_PROJ = 512
CONV_COLS = 256
M_CHUNK = 256
A_TILE = 512
A_ROWS = 256
A_NORM_ROWS = 1024
A_SAFE_RANGE = 90.0
F_CHUNK = 256
N_FCHUNK = D_FF // F_CHUNK
N_UBUF = 4

bf16 = jnp.bfloat16
f32 = jnp.float32


def _dot(a, b):
    return jnp.dot(a, b, preferred_element_type=f32)


def _dot_nt(a, b):
    return lax.dot_general(a, b, (((1,), (1,)), ((), ())), preferred_element_type=f32)


def _sigmoid(x):
    return 1.0 / (1.0 + jnp.exp(-x))


def _const_spec(shape):
    nd = len(shape)
    return pl.BlockSpec(shape, lambda *_: (0,) * nd)


def _shift_rows(u, prev8, k):
    r = pltpu.roll(u, k, 0)
    rp = pltpu.roll(prev8, k, 0)
    row = lax.broadcasted_iota(jnp.int32, (SUBLANES, u.shape[1]), 0)
    first = jnp.where(row < k, rp, r[:SUBLANES])
    return jnp.concatenate([first, r[SUBLANES:]], axis=0)


def _adaln_kernel(c_ref, w_ref, b_ref, o_ref):
    c = c_ref[...]
    a = (c * _sigmoid(c)).astype(bf16)
    o_ref[...] = _dot(a, w_ref[...].astype(bf16)) + b_ref[...]


def _adaln(c8, w, b):
    n = w.shape[1]
    tn = 1536
    return pl.pallas_call(
        _adaln_kernel,
        grid=(n // tn,),
        in_specs=[_const_spec(c8.shape),
                  pl.BlockSpec((D_MODEL, tn), lambda j: (0, j)),
                  pl.BlockSpec((1, tn), lambda j: (0, j))],
        out_specs=pl.BlockSpec((c8.shape[0], tn), lambda j: (0, j)),
        out_shape=jax.ShapeDtypeStruct((c8.shape[0], n), f32),
        compiler_params=pltpu.CompilerParams(dimension_semantics=("arbitrary",),
                                             vmem_limit_bytes=VMEM_LIMIT),
        name="adaln",
    )(c8, w, b)


def _bias_kernel(tab_ref, o_ref):
    h = pl.program_id(0)
    t = pl.program_id(1)
    row = lax.broadcasted_iota(jnp.int32, (A_TILE, A_TILE), 0)
    col = lax.broadcasted_iota(jnp.int32, (A_TILE, A_TILE), 1)
    dist = row - col + (1 - t) * A_TILE
    n = jnp.maximum(dist, 0)
    max_exact = N_BUCKETS // 2
    nf = jnp.maximum(n, 1).astype(f32)
    large = max_exact + (jnp.log(nf / max_exact) / math.log(MAX_DISTANCE / max_exact)
                         * (N_BUCKETS - max_exact)).astype(jnp.int32)
    large = jnp.minimum(large, N_BUCKETS - 1)
    bucket = jnp.where(n < max_exact, n, large)
    far = tab_ref[h, N_BUCKETS - 1]
    val = jnp.zeros((A_TILE, A_TILE), f32)
    for b in range(N_BUCKETS - 1):
        val = jnp.where(bucket == b, tab_ref[h, b] - far, val)
    o_ref[0] = jnp.where(dist >= 0, val * LOG2E, NEG_BIG)


def _bias_tiles(tab):
    return pl.pallas_call(
        _bias_kernel,
        grid=(A_HEADS, 2),
        in_specs=[pl.BlockSpec(memory_space=pltpu.SMEM)],
        out_specs=pl.BlockSpec((1, A_TILE, A_TILE), lambda h, t: (h, 0, t)),
        out_shape=jax.ShapeDtypeStruct((A_HEADS, A_TILE, 2 * A_TILE), f32),
        compiler_params=pltpu.CompilerParams(dimension_semantics=("arbitrary", "arbitrary"),
                                             vmem_limit_bytes=VMEM_LIMIT),
        name="bias_tiles",
    )(tab)


C_MQK, C_MV, C_MO, C_AQ, C_AK, C_AV, C_GM, C_GA, C_GATE, C_END = (
    0, 1024, 1536, 2048, 2560, 3072, 3584, 4608, 5632, 5888)


def _inproj_kernel(x_ref, mod_ref, g_ref, w_ref, grp_ref, qg_ref, kg_ref, cw_ref, cb_ref,
                   mqk_ref, mv_ref, mo_ref, qn_ref, kn_ref, av_ref, gm_ref, ga_ref, gate_ref, tail_ref, hb_ref,
                   *, tiles_per_batch):
    tm = x_ref.shape[0]

    @pl.when(pl.program_id(0) % tiles_per_batch == 0)
    def _():
        tail_ref[...] = jnp.zeros_like(tail_ref)

    x = x_ref[...]
    mod = mod_ref[0]
    ms = jnp.mean(x * x, axis=-1, keepdims=True)
    y = x * lax.rsqrt(ms + RMS_EPS) * g_ref[...]
    hb_ref[...] = (y * (1.0 + mod[1:2]) + mod[0:1]).astype(bf16)

    def proj(c0, c1):
        return _dot(hb_ref[...], w_ref[:, c0:c1])

    def qknorm(a, gain_ref):
        msq = _dot((a * a).astype(bf16), grp_ref[...])
        return (a * lax.rsqrt(msq + RMS_EPS) * gain_ref[...]).astype(bf16)

    def conv_silu(c0, c1, scale):
        cols = slice(c0, c1)
        u = proj(c0, c1)
        prev8 = tail_ref[:, cols]
        tail_ref[:, cols] = u[tm - SUBLANES:, :]
        cw = cw_ref[:, cols]
        y = (cw[3:4] * u + cw[2:3] * _shift_rows(u, prev8, 1) + cw[1:2] * _shift_rows(u, prev8, 2)
             + cw[0:1] * _shift_rows(u, prev8, 3) + cb_ref[:, cols])
        return (y * _sigmoid(y) * scale).astype(bf16)

    for c0 in range(C_MQK, C_MV, CONV_COLS):
        scale = 1.0 if c0 < C_MQK + M_WIDTH else M_HEAD_DIM ** -0.5
        mqk_ref[:, c0:c0 + CONV_COLS] = conv_silu(c0, c0 + CONV_COLS, scale)
    mv_ref[...] = proj(C_MV, C_MO).astype(bf16)
    mo_ref[...] = proj(C_MO, C_AQ).astype(bf16)
    qn_ref[...] = qknorm(proj(C_AQ, C_AK), qg_ref)
    kn_ref[...] = qknorm(proj(C_AK, C_AV), kg_ref)
    av_ref[...] = proj(C_AV, C_GM).astype(bf16)
    gm_ref[:, 0:512] = proj(C_GM, C_GM + 512).astype(bf16)
    gm_ref[:, 512:1024] = proj(C_GM + 512, C_GA).astype(bf16)
    ga_ref[:, 0:512] = proj(C_GA, C_GA + 512).astype(bf16)
    ga_ref[:, 512:1024] = proj(C_GA + 512, C_GATE).astype(bf16)
    gate_ref[...] = proj(C_GATE, C_END)


def _inproj(x2, mod3, g1, w, grp, qg, kg, cw, cb, tiles_per_batch):
    n = x2.shape[0]
    tm = TM_PROJ
    tok = lambda width: pl.BlockSpec((tm, width), lambda i: (i, 0))
    outs = [(1024, bf16), (512, bf16), (512, bf16), (512, bf16), (512, bf16), (512, bf16),
            (1024, bf16), (1024, bf16), (2 * LANES, f32)]
    return pl.pallas_call(
        functools.partial(_inproj_kernel, tiles_per_batch=tiles_per_batch),
        grid=(n // tm,),
        in_specs=[tok(D_MODEL),
                  pl.BlockSpec((1, 6, D_MODEL), lambda i: (i // tiles_per_batch, 0, 0)),
                  _const_spec(g1.shape), _const_spec(w.shape), _const_spec(grp.shape),
                  _const_spec(qg.shape), _const_spec(kg.shape), _const_spec(cw.shape), _const_spec(cb.shape)],
        out_specs=[tok(wd) for wd, _ in outs],
        out_shape=[jax.ShapeDtypeStruct((n, wd), dt) for wd, dt in outs],
        scratch_shapes=[pltpu.VMEM((SUBLANES, 2 * M_WIDTH), f32), pltpu.VMEM((tm, D_MODEL), bf16)],
        compiler_params=pltpu.CompilerParams(dimension_semantics=("arbitrary",),
                                             vmem_limit_bytes=VMEM_LIMIT),
        name="inproj",
    )(x2, mod3, g1, w, grp, qg, kg, cw, cb)


def _mlstm_kernel(mqk_ref, mv_ref, mo_ref, gate_ref, bi_ref, bfg_ref, mg_ref, o_ref, c_st, m_st):
    L = M_CHUNK
    d = M_HEAD_DIM

    @pl.when(pl.program_id(1) == 0)
    def _():
        c_st[...] = jnp.zeros_like(c_st)
        m_st[...] = jnp.zeros_like(m_st)

    gi = gate_ref[:, 0:LANES] + bi_ref[...]
    gf = gate_ref[:, LANES:2 * LANES] + bfg_ref[...]
    logf = jnp.minimum(gf, 0.0) - jnp.log(1.0 + jnp.exp(-jnp.abs(gf)))
    row = lax.broadcasted_iota(jnp.int32, (L, L), 0)
    col = lax.broadcasted_iota(jnp.int32, (L, L), 1)
    causal = row >= col
    tri = jnp.where(causal, 1.0, 0.0).astype(bf16)
    logf_hi = logf.astype(bf16)
    logf_lo = (logf - logf_hi.astype(f32)).astype(bf16)
    bcum = _dot(tri, logf_hi) + _dot(tri, logf_lo)
    r = gi - bcum
    r_t = r.T
    b_last = bcum[L - 1:L, :]
    g = b_last + r
    m_loc = jnp.max(g, axis=0, keepdims=True)
    ones = jnp.ones((L, d), bf16)

    for h in range(M_HEADS):
        hs = slice(h * d, (h + 1) * d)
        qh = mqk_ref[:, hs]
        kh = mqk_ref[:, M_WIDTH + h * d:M_WIDTH + (h + 1) * d]
        vaug = jnp.concatenate([mv_ref[:, hs], ones], axis=1)
        m_in = m_st[h][0:1, 0:1]
        b_col = bcum[:, h:h + 1]
        dmat = jnp.where(causal, b_col + r_t[h:h + 1, :], NEG_BIG)
        a_t = b_col + m_in
        m_t = jnp.maximum(a_t, jnp.max(dmat, axis=1, keepdims=True))
        inter_w = jnp.exp(a_t - m_t)
        s = _dot_nt(qh, kh) * jnp.exp(dmat - m_t)
        tot = _dot(s.astype(bf16), vaug) + inter_w * _dot(qh, c_st[h].astype(bf16))
        den = jnp.maximum(jnp.abs(tot[:, d:]), jnp.exp(-m_t))
        hh = tot[:, :d] / den

        bl = b_last[:, h:h + 1]
        m_new = jnp.maximum(bl + m_in, m_loc[:, h:h + 1])
        w_col = jnp.exp(g[:, h:h + 1] - m_new)
        kw_t = (kh.astype(f32) * w_col).T.astype(bf16)
        c_st[h] = jnp.exp(bl + m_in - m_new) * c_st[h] + _dot(kw_t, vaug)
        m_st[h] = jnp.broadcast_to(m_new, (SUBLANES, LANES))

        hn = hh * lax.rsqrt(jnp.mean(hh * hh, axis=-1, keepdims=True) + RMS_EPS) * mg_ref[:, hs]
        o_ref[:, hs] = (_sigmoid(mo_ref[:, hs].astype(f32)) * hn).astype(bf16)


def _mlstm(mqk, mv, mo, gates, bi, bfg, mg, batch, seq):
    L = M_CHUNK
    nc = seq // L
    tok = lambda width: pl.BlockSpec((L, width), lambda b, c: (b * nc + c, 0))
    return pl.pallas_call(
        _mlstm_kernel,
        grid=(batch, nc),
        in_specs=[tok(2 * M_WIDTH), tok(M_WIDTH), tok(M_WIDTH), tok(2 * LANES),
                  _const_spec(bi.shape), _const_spec(bfg.shape), _const_spec(mg.shape)],
        out_specs=tok(M_WIDTH),
        out_shape=jax.ShapeDtypeStruct((batch * seq, M_WIDTH), bf16),
        scratch_shapes=[pltpu.VMEM((M_HEADS, M_HEAD_DIM, 2 * M_HEAD_DIM), f32),
                        pltpu.VMEM((M_HEADS, SUBLANES, LANES), f32)],
        compiler_params=pltpu.CompilerParams(dimension_semantics=("arbitrary", "arbitrary"),
                                             vmem_limit_bytes=VMEM_LIMIT),
        name="mlstm",
    )(mqk, mv, mo, gates, bi, bfg, mg)


def _attn_kernel(tab_ref, q_ref, k_ref, v_ref, bias_ref, lam_ref, ng_ref, o_ref,
                 vaug_ref, q2_ref, acc_ref, m_ref, mfix_ref, kmax_ref, *, lam_init):
    T = A_TILE
    dv = A_V_DIM
    h = pl.program_id(1)
    qi = pl.program_id(2)
    sel = jnp.where((lax.broadcasted_iota(jnp.int32, (LANES, 2 * LANES), 0) < A_QK_DIM)
                    == (lax.broadcasted_iota(jnp.int32, (LANES, 2 * LANES), 1) < LANES), 1.0, 0.0).astype(bf16)

    def sq_norms(x):
        xf = x.astype(f32)
        return _dot((xf * xf).astype(bf16), sel)

    @pl.when(qi == 0)
    def _():
        vaug_ref[:, 0:dv] = v_ref[...]
        vaug_ref[:, dv:2 * dv] = jnp.ones((v_ref.shape[0], dv), bf16)
        kmax = jnp.zeros((1, 2 * LANES), f32)
        for c in range(k_ref.shape[0] // A_NORM_ROWS):
            kn2 = sq_norms(k_ref[c * A_NORM_ROWS:(c + 1) * A_NORM_ROWS, :])
            kmax = jnp.maximum(kmax, jnp.max(kn2, axis=0, keepdims=True))
        kmax_ref[...] = kmax

    q = q_ref[...]
    lane = lax.broadcasted_iota(jnp.int32, q.shape, 1)
    zero = jnp.zeros_like(q)
    q2_ref[0:T, :] = jnp.where(lane < A_QK_DIM, q, zero)
    q2_ref[T:2 * T, :] = jnp.where(lane >= A_QK_DIM, q, zero)
    acc_ref[...] = jnp.zeros_like(acc_ref)

    bmax = jnp.float32(0.0)
    bmin = jnp.float32(0.0)
    for b in range(N_BUCKETS - 1):
        rel = (tab_ref[h, b] - tab_ref[h, N_BUCKETS - 1]) * LOG2E
        bmax = jnp.maximum(bmax, rel)
        bmin = jnp.minimum(bmin, rel)
    bound = jnp.sqrt(sq_norms(q) * kmax_ref[...])
    mfix_ref[0:T, :] = bound[:, 0:LANES] + bmax
    mfix_ref[T:2 * T, :] = bound[:, LANES:2 * LANES] + bmax
    safe = 2.0 * jnp.max(bound) + (bmax - bmin) < A_SAFE_RANGE

    def step(start, nk, near, fixed_shift):
        start = pl.multiple_of(start, T)
        kt = k_ref[pl.ds(start, nk), :]
        va = vaug_ref[pl.ds(start, nk), :]
        for rb in range(2 * T // A_ROWS):
            rows = slice(rb * A_ROWS, (rb + 1) * A_ROWS)
            s = _dot_nt(q2_ref[rows, :], kt)
            if near:
                brow = (rb * A_ROWS) % T
                s = s + bias_ref[0, brow:brow + A_ROWS, 2 * T - nk:2 * T]
            if fixed_shift:
                mrow = mfix_ref[rows, :]
                p = jnp.exp2(s - jnp.concatenate([mrow] * (nk // LANES), axis=1)).astype(bf16)
                acc_ref[rows, :] += _dot(p, va)
            else:
                m_old = m_ref[rows, :]
                m_new = jnp.maximum(m_old, jnp.max(s, axis=1, keepdims=True))
                p = jnp.exp2(s - m_new).astype(bf16)
                acc_ref[rows, :] = jnp.exp2(m_old - m_new) * acc_ref[rows, :] + _dot(p, va)
                m_ref[rows, :] = m_new

    def sweep(fixed_shift):
        nfar = jnp.maximum(qi - 1, 0)
        npair = nfar // 2

        def far_body(j, carry):
            step(j * (2 * T), 2 * T, False, fixed_shift)
            return carry

        lax.fori_loop(0, npair, far_body, 0)

        @pl.when(nfar % 2 == 1)
        def _():
            step(npair * (2 * T), T, False, fixed_shift)

        @pl.when(qi >= 1)
        def _():
            step((qi - 1) * T, 2 * T, True, fixed_shift)

        @pl.when(qi == 0)
        def _():
            step(0, T, True, fixed_shift)

    @pl.when(safe)
    def _():
        sweep(True)

    @pl.when(jnp.logical_not(safe))
    def _():
        m_ref[...] = jnp.full_like(m_ref, NEG_BIG)
        sweep(False)

    al = lam_ref[...]
    lam = (jnp.exp(jnp.sum(al[0:1] * al[1:2], keepdims=True))
           - jnp.exp(jnp.sum(al[2:3] * al[3:4], keepdims=True)) + lam_init)
    acc = acc_ref[...]
    o = acc[:, 0:dv] / acc[:, dv:2 * dv]
    ha = o[0:T] - lam * o[T:2 * T]
    hn = ha * lax.rsqrt(jnp.mean(ha * ha, axis=-1, keepdims=True) + RMS_EPS) * ng_ref[...]
    o_ref[...] = (hn * (1.0 - lam_init)).astype(bf16)


def _diffattn(tab, qn, kn, av, bias, lam_par, ng, batch, seq, lam_init):
    T = A_TILE
    nq = seq // T
    return pl.pallas_call(
        functools.partial(_attn_kernel, lam_init=lam_init),
        grid=(batch, A_HEADS, nq),
        in_specs=[pl.BlockSpec(memory_space=pltpu.SMEM),
                  pl.BlockSpec((T, LANES), lambda b, h, i: (b * nq + i, h)),
                  pl.BlockSpec((seq, LANES), lambda b, h, i: (b, h)),
                  pl.BlockSpec((seq, LANES), lambda b, h, i: (b, h)),
                  pl.BlockSpec((1, T, 2 * T), lambda b, h, i: (h, 0, 0)),
                  _const_spec(lam_par.shape),
                  pl.BlockSpec((1, LANES), lambda b, h, i: (0, h))],
        out_specs=pl.BlockSpec((T, LANES), lambda b, h, i: (b * nq + i, h)),
        out_shape=jax.ShapeDtypeStruct((batch * seq, A_WIDTH), bf16),
        scratch_shapes=[pltpu.VMEM((seq, 2 * A_V_DIM), bf16),
                        pltpu.VMEM((2 * T, LANES), bf16),
                        pltpu.VMEM((2 * T, 2 * A_V_DIM), f32),
                        pltpu.VMEM((2 * T, 1), f32),
                        pltpu.VMEM((2 * T, LANES), f32),
                        pltpu.VMEM((1, 2 * LANES), f32)],
        compiler_params=pltpu.CompilerParams(
            dimension_semantics=("arbitrary", "arbitrary", "arbitrary"),
            vmem_limit_bytes=VMEM_LIMIT),
        name="diffattn",
    )(tab, qn, kn, av, bias, lam_par, ng)


def _merge_kernel(x_ref, mod_ref, hm_ref, ha_ref, gm_ref, ga_ref, wm_ref, wa_ref, wo_ref, o_ref):
    ym = _dot(hm_ref[...], wm_ref[...])
    ya = _dot(ha_ref[...], wa_ref[...])
    y = _sigmoid(gm_ref[...].astype(f32)) * ym + _sigmoid(ga_ref[...].astype(f32)) * ya
    o_ref[...] = x_ref[...] + mod_ref[0][2:3] * _dot(y.astype(bf16), wo_ref[...])


def _merge(x2, mod3, hm, ha, gm, ga, wm, wa, wo, tiles_per_batch):
    n = x2.shape[0]
    tm = TM_PROJ
    tok = lambda width: pl.BlockSpec((tm, width), lambda i: (i, 0))
    return pl.pallas_call(
        _merge_kernel,
        grid=(n // tm,),
        in_specs=[tok(D_MODEL),
                  pl.BlockSpec((1, 6, D_MODEL), lambda i: (i // tiles_per_batch, 0, 0)),
                  tok(M_WIDTH), tok(A_WIDTH), tok(D_MODEL), tok(D_MODEL),
                  _const_spec(wm.shape), _const_spec(wa.shape), _const_spec(wo.shape)],
        out_specs=tok(D_MODEL),
        out_shape=jax.ShapeDtypeStruct((n, D_MODEL), f32),
        compiler_params=pltpu.CompilerParams(dimension_semantics=("arbitrary",),
                                             vmem_limit_bytes=VMEM_LIMIT),
        name="merge",
    )(x2, mod3, hm, ha, gm, ga, wm, wa, wo)


def _ffn_kernel(x_ref, mod_ref, g_ref, wv_ref, wg_ref, cwv_ref, cwg_ref, cbv_ref, cbg_ref, wd_ref,
                o_ref, hv_ref, hg_ref, ubuf_ref, act_ref, *, tiles_per_batch):
    tm = x_ref.shape[0]

    @pl.when(pl.program_id(0) % tiles_per_batch == 0)
    def _():
        hv_ref[...] = jnp.zeros_like(hv_ref)
        hg_ref[...] = jnp.zeros_like(hg_ref)

    x = x_ref[...]
    mod = mod_ref[0]
    ms = jnp.mean(x * x, axis=-1, keepdims=True)
    y = x * lax.rsqrt(ms + RMS_EPS) * g_ref[...]
    hb = (y * (1.0 + mod[4:5]) + mod[3:4]).astype(bf16)

    def conv(u, halo, buf, w, b):
        buf[0:SUBLANES, :] = halo
        buf[SUBLANES:SUBLANES + tm, :] = u
        return (w[2:3] * u + w[1:2] * buf[SUBLANES - 1:SUBLANES - 1 + tm, :]
                + w[0:1] * buf[SUBLANES - 2:SUBLANES - 2 + tm, :] + b)

    for j in range(N_FCHUNK):
        cols = slice(j * F_CHUNK, (j + 1) * F_CHUNK)
        uv = _dot(hb, wv_ref[j])
        ug = _dot(hb, wg_ref[j])
        cv = conv(uv, hv_ref[j], ubuf_ref.at[(2 * j) % N_UBUF], cwv_ref[j], cbv_ref[j])
        cg = conv(ug, hg_ref[j], ubuf_ref.at[(2 * j + 1) % N_UBUF], cwg_ref[j], cbg_ref[j])
        hv_ref[j] = uv[tm - SUBLANES:, :]
        hg_ref[j] = ug[tm - SUBLANES:, :]
        act_ref[:, cols] = ((cg * _sigmoid(cg)) * cv).astype(bf16)
    o_ref[...] = x + mod[5:6] * _dot(act_ref[...], wd_ref[...])


def _ffn(x2, mod3, g2, wv, wg, cwv, cwg, cbv, cbg, wd, tiles_per_batch):
    n = x2.shape[0]
    tm = TM_PROJ
    tok = lambda width: pl.BlockSpec((tm, width), lambda i: (i, 0))
    return pl.pallas_call(
        functools.partial(_ffn_kernel, tiles_per_batch=tiles_per_batch),
        grid=(n // tm,),
        in_specs=[tok(D_MODEL),
                  pl.BlockSpec((1, 6, D_MODEL), lambda i: (i // tiles_per_batch, 0, 0)),
                  _const_spec(g2.shape), _const_spec(wv.shape), _const_spec(wg.shape),
                  _const_spec(cwv.shape), _const_spec(cwg.shape), _const_spec(cbv.shape),
                  _const_spec(cbg.shape), _const_spec(wd.shape)],
        out_specs=tok(D_MODEL),
        out_shape=jax.ShapeDtypeStruct((n, D_MODEL), f32),
        scratch_shapes=[pltpu.VMEM((N_FCHUNK, SUBLANES, F_CHUNK), f32),
                        pltpu.VMEM((N_FCHUNK, SUBLANES, F_CHUNK), f32),
                        pltpu.VMEM((N_UBUF, tm + SUBLANES, F_CHUNK), f32),
                        pltpu.VMEM((tm, D_FF), bf16)],
        compiler_params=pltpu.CompilerParams(dimension_semantics=("arbitrary",),
                                             vmem_limit_bytes=VMEM_LIMIT),
        name="ffn",
    )(x2, mod3, g2, wv, wg, cwv, cwg, cbv, cbg, wd)


def _pad_cols(a, width):
    return jnp.pad(a, ((0, 0), (0, width - a.shape[1])))


def _layer(x2, c8, batch, seq, layer, w_ada, b_ada, norm1_g, w_in, m_conv_w, m_conv_b, m_igate_b,
           m_fgate_b, m_norm_g, a_qnorm_g, a_knorm_g, a_lambda, a_norm_g, tab, bias, w_branch_m,
           w_branch_a, w_out, norm2_g, w_up, ffn_conv_w, ffn_conv_b, w_down):
    tiles_per_batch = seq // TM_PROJ
    mod3 = _adaln(c8, w_ada, b_ada.reshape(1, -1))[:batch].reshape(batch, 6, D_MODEL)

    o = 0
    parts = {}
    for name, size in (("mqk", 2 * M_WIDTH), ("mv", M_WIDTH), ("mo", M_WIDTH), ("mi", M_HEADS),
                       ("mf", M_HEADS), ("aq", A_WIDTH), ("ak", A_WIDTH), ("av", A_WIDTH),
                       ("gm", D_MODEL), ("ga", D_MODEL)):
        parts[name] = w_in[:, o:o + size]
        o += size

    def per_head(w):
        return w.reshape(D_MODEL, 2, A_HEADS, A_QK_DIM).transpose(0, 2, 1, 3).reshape(D_MODEL, A_WIDTH)

    w_cat = jnp.concatenate(
        [parts["mqk"], parts["mv"], parts["mo"], per_head(parts["aq"]), per_head(parts["ak"]),
         parts["av"], parts["gm"], parts["ga"], _pad_cols(parts["mi"], LANES),
         _pad_cols(parts["mf"], LANES)], axis=1).astype(bf16)
    gid = jnp.arange(A_WIDTH) // A_QK_DIM
    grp = jnp.where(gid[:, None] == gid[None, :], 1.0 / A_QK_DIM, 0.0).astype(bf16)
    qg = (jnp.tile(a_qnorm_g, A_WIDTH // A_QK_DIM) * (A_QK_DIM ** -0.5 * LOG2E)).reshape(1, A_WIDTH)
    kg = jnp.tile(a_knorm_g, A_WIDTH // A_QK_DIM).reshape(1, A_WIDTH)

    mqk, mv, mo, qn, kn, av, gm, ga, gates = _inproj(
        x2, mod3, norm1_g.reshape(1, -1), w_cat, grp, qg, kg, m_conv_w, m_conv_b.reshape(1, -1),
        tiles_per_batch)

    hm = _mlstm(mqk, mv, mo, gates,
                _pad_cols(m_igate_b.reshape(1, -1), LANES), _pad_cols(m_fgate_b.reshape(1, -1), LANES),
                m_norm_g.reshape(1, -1), batch, seq)

    lam_init = 0.8 - 0.6 * math.exp(-0.3 * layer)
    ha = _diffattn(tab, qn, kn, av, bias, a_lambda, a_norm_g.reshape(1, -1), batch, seq, lam_init)

    x1 = _merge(x2, mod3, hm, ha, gm, ga, w_branch_m.astype(bf16), w_branch_a.astype(bf16),
                w_out.astype(bf16), tiles_per_batch)

    def chunks(a):
        return a.reshape(a.shape[0], N_FCHUNK, F_CHUNK).transpose(1, 0, 2)

    wv = chunks(w_up[:, :D_FF]).astype(bf16)
    wg = chunks(w_up[:, D_FF:]).astype(bf16)
    cwv = chunks(ffn_conv_w[:, :D_FF])
    cwg = chunks(ffn_conv_w[:, D_FF:])
    cbv = chunks(ffn_conv_b[None, :D_FF])
    cbg = chunks(ffn_conv_b[None, D_FF:])
    wd = w_down.astype(bf16)
    return _ffn(x1, mod3, norm2_g.reshape(1, -1), wv, wg, cwv, cwg, cbv, cbg, wd, tiles_per_batch)


def kernel(x, c, w_ada, b_ada, norm1_g, w_in, m_conv_w, m_conv_b, m_igate_b, m_fgate_b, m_norm_g,
           a_qnorm_g, a_knorm_g, a_lambda, a_norm_g, rel_bias, w_branch_m, w_branch_a, w_out, norm2_g,
           w_up, ffn_conv_w, ffn_conv_b, w_down):
    batch, seq, _ = x.shape
    depth = w_ada.shape[0]
    x2 = x.reshape(batch * seq, D_MODEL)
    c8 = jnp.pad(c, ((0, SUBLANES - batch), (0, 0)))
    tab = rel_bias.astype(f32).T
    bias = _bias_tiles(tab)
    for l in range(depth):
        x2 = _layer(x2, c8, batch, seq, l, w_ada[l], b_ada[l], norm1_g[l], w_in[l], m_conv_w[l],
                    m_conv_b[l], m_igate_b[l], m_fgate_b[l], m_norm_g[l], a_qnorm_g[l], a_knorm_g[l],
                    a_lambda[l], a_norm_g[l], tab, bias, w_branch_m[l], w_branch_a[l], w_out[l], norm2_g[l],
                    w_up[l], ffn_conv_w[l], ffn_conv_b[l], w_down[l])
    return x2.reshape(batch, seq, D_MODEL)
```

```python
import functools
import math

import jax
import jax.numpy as jnp
from jax import lax
from jax.experimental import pallas as pl
from jax.experimental.pallas import tpu as pltpu

D_MODEL = 1024
M_HEADS = 4
M_HEAD_DIM = 128
M_WIDTH = M_HEADS * M_HEAD_DIM
M_CONV = 4
A_HEADS = 4
A_QK_DIM = 64
A_V_DIM = 2 * A_QK_DIM
A_WIDTH = A_HEADS * A_V_DIM
N_BUCKETS = 32
MAX_DISTANCE = 128
D_FF = 2816
FFN_CONV = 3
RMS_EPS = 1e-6
LOG2E = math.log2(math.e)
NEG_BIG = -1e30

LANES = 128
SUBLANES = 8
VMEM_LIMIT = 56 * 1024 * 1024

TM_PROJ = 512
M_CHUNK = 256
A_TILE = 512
A_ROWS = 256
A_NORM_ROWS = 1024
A_SAFE_RANGE = 90.0
F_CHUNK = 256
N_FCHUNK = D_FF // F_CHUNK
N_UBUF = 4

bf16 = jnp.bfloat16
f32 = jnp.float32


def _dot(a, b):
    return jnp.dot(a, b, preferred_element_type=f32)


def _dot_nt(a, b):
    return lax.dot_general(a, b, (((1,), (1,)), ((), ())), preferred_element_type=f32)


def _sigmoid(x):
    return 1.0 / (1.0 + jnp.exp(-x))


def _const_spec(shape):
    nd = len(shape)
    return pl.BlockSpec(shape, lambda *_: (0,) * nd)


def _shift_rows(u, prev8, k):
    r = pltpu.roll(u, k, 0)
    rp = pltpu.roll(prev8, k, 0)
    row = lax.broadcasted_iota(jnp.int32, (SUBLANES, u.shape[1]), 0)
    first = jnp.where(row < k, rp, r[:SUBLANES])
    return jnp.concatenate([first, r[SUBLANES:]], axis=0)


def _adaln_kernel(c_ref, w_ref, b_ref, o_ref):
    c = c_ref[...]
    a = (c * _sigmoid(c)).astype(bf16)
    o_ref[...] = _dot(a, w_ref[...].astype(bf16)) + b_ref[...]


def _adaln(c8, w, b):
    n = w.shape[1]
    tn = 1536
    return pl.pallas_call(
        _adaln_kernel,
        grid=(n // tn,),
        in_specs=[_const_spec(c8.shape),
                  pl.BlockSpec((D_MODEL, tn), lambda j: (0, j)),
                  pl.BlockSpec((1, tn), lambda j: (0, j))],
        out_specs=pl.BlockSpec((c8.shape[0], tn), lambda j: (0, j)),
        out_shape=jax.ShapeDtypeStruct((c8.shape[0], n), f32),
        compiler_params=pltpu.CompilerParams(dimension_semantics=("arbitrary",),
                                             vmem_limit_bytes=VMEM_LIMIT),
        name="adaln",
    )(c8, w, b)


def _bias_kernel(tab_ref, o_ref):
    h = pl.program_id(0)
    t = pl.program_id(1)
    row = lax.broadcasted_iota(jnp.int32, (A_TILE, A_TILE), 0)
    col = lax.broadcasted_iota(jnp.int32, (A_TILE, A_TILE), 1)
    dist = row - col + (1 - t) * A_TILE
    n = jnp.maximum(dist, 0)
    max_exact = N_BUCKETS // 2
    nf = jnp.maximum(n, 1).astype(f32)
    large = max_exact + (jnp.log(nf / max_exact) / math.log(MAX_DISTANCE / max_exact)
                         * (N_BUCKETS - max_exact)).astype(jnp.int32)
    large = jnp.minimum(large, N_BUCKETS - 1)
    bucket = jnp.where(n < max_exact, n, large)
    far = tab_ref[h, N_BUCKETS - 1]
    val = jnp.zeros((A_TILE, A_TILE), f32)
    for b in range(N_BUCKETS - 1):
        val = jnp.where(bucket == b, tab_ref[h, b] - far, val)
    o_ref[0] = jnp.where(dist >= 0, val * LOG2E, NEG_BIG)


def _bias_tiles(tab):
    return pl.pallas_call(
        _bias_kernel,
        grid=(A_HEADS, 2),
        in_specs=[pl.BlockSpec(memory_space=pltpu.SMEM)],
        out_specs=pl.BlockSpec((1, A_TILE, A_TILE), lambda h, t: (h, 0, t)),
        out_shape=jax.ShapeDtypeStruct((A_HEADS, A_TILE, 2 * A_TILE), f32),
        compiler_params=pltpu.CompilerParams(dimension_semantics=("arbitrary", "arbitrary"),
                                             vmem_limit_bytes=VMEM_LIMIT),
        name="bias_tiles",
    )(tab)


C_MQK, C_MV, C_MO, C_AQ, C_AK, C_AV, C_GM, C_GA, C_GATE, C_END = (
    0, 1024, 1536, 2048, 2560, 3072, 3584, 4608, 5632, 5888)


def _inproj_kernel(x_ref, mod_ref, g_ref, w_ref, grp_ref, qg_ref, kg_ref,
                   mqk_ref, mv_ref, mo_ref, qn_ref, kn_ref, av_ref, gm_ref, ga_ref, gate_ref):
    x = x_ref[...]
    mod = mod_ref[0]
    ms = jnp.mean(x * x, axis=-1, keepdims=True)
    y = x * lax.rsqrt(ms + RMS_EPS) * g_ref[...]
    hb = (y * (1.0 + mod[1:2]) + mod[0:1]).astype(bf16)

    def proj(c0, c1):
        return _dot(hb, w_ref[:, c0:c1])

    def qknorm(a, gain_ref):
        sq = (a * a).astype(bf16)
        gw = grp_ref.shape[0]
        msq = jnp.concatenate([_dot(sq[:, c:c + gw], grp_ref[...]) for c in range(0, a.shape[1], gw)], axis=1)
        return (a * lax.rsqrt(msq + RMS_EPS) * gain_ref[...]).astype(bf16)

    mqk_ref[:, 0:512] = proj(C_MQK, C_MQK + 512).astype(bf16)
    mqk_ref[:, 512:1024] = proj(C_MQK + 512, C_MV).astype(bf16)
    mv_ref[...] = proj(C_MV, C_MO).astype(bf16)
    mo_ref[...] = proj(C_MO, C_AQ).astype(bf16)
    qn_ref[...] = qknorm(proj(C_AQ, C_AK), qg_ref)
    kn_ref[...] = qknorm(proj(C_AK, C_AV), kg_ref)
    av_ref[...] = proj(C_AV, C_GM).astype(bf16)
    gm_ref[:, 0:512] = proj(C_GM, C_GM + 512).astype(bf16)
    gm_ref[:, 512:1024] = proj(C_GM + 512, C_GA).astype(bf16)
    ga_ref[:, 0:512] = proj(C_GA, C_GA + 512).astype(bf16)
    ga_ref[:, 512:1024] = proj(C_GA + 512, C_GATE).astype(bf16)
    gate_ref[...] = proj(C_GATE, C_END)


def _inproj(x2, mod3, g1, w, grp, qg, kg, tiles_per_batch):
    n = x2.shape[0]
    tm = TM_PROJ
    tok = lambda width: pl.BlockSpec((tm, width), lambda i: (i, 0))
    outs = [(1024, bf16), (512, bf16), (512, bf16), (512, bf16), (512, bf16), (512, bf16),
            (1024, bf16), (1024, bf16), (2 * LANES, f32)]
    return pl.pallas_call(
        _inproj_kernel,
        grid=(n // tm,),
        in_specs=[tok(D_MODEL),
                  pl.BlockSpec((1, 6, D_MODEL), lambda i: (i // tiles_per_batch, 0, 0)),
                  _const_spec(g1.shape), _const_spec(w.shape), _const_spec(grp.shape),
                  _const_spec(qg.shape), _const_spec(kg.shape)],
        out_specs=[tok(wd) for wd, _ in outs],
        out_shape=[jax.ShapeDtypeStruct((n, wd), dt) for wd, dt in outs],
        compiler_params=pltpu.CompilerParams(dimension_semantics=("arbitrary",),
                                             vmem_limit_bytes=VMEM_LIMIT),
        name="inproj",
    )(x2, mod3, g1, w, grp, qg, kg)


def _mlstm_kernel(mqk_ref, mv_ref, mo_ref, gate_ref, cw_ref, cb_ref, bi_ref, bfg_ref, mg_ref,
                  o_ref, c_st, m_st, tail_ref):
    L = M_CHUNK
    d = M_HEAD_DIM

    @pl.when(pl.program_id(1) == 0)
    def _():
        c_st[...] = jnp.zeros_like(c_st)
        m_st[...] = jnp.zeros_like(m_st)
        tail_ref[...] = jnp.zeros_like(tail_ref)

    x = mqk_ref[...].astype(f32)
    prev8 = tail_ref[...]
    cw = cw_ref[...]
    y = (cw[3:4] * x + cw[2:3] * _shift_rows(x, prev8, 1) + cw[1:2] * _shift_rows(x, prev8, 2)
         + cw[0:1] * _shift_rows(x, prev8, 3) + cb_ref[...])
    tail_ref[...] = x[L - SUBLANES:, :]
    qk = y * _sigmoid(y)

    gi = gate_ref[:, 0:LANES] + bi_ref[...]
    gf = gate_ref[:, LANES:2 * LANES] + bfg_ref[...]
    logf = jnp.minimum(gf, 0.0) - jnp.log(1.0 + jnp.exp(-jnp.abs(gf)))
    row = lax.broadcasted_iota(jnp.int32, (L, L), 0)
    col = lax.broadcasted_iota(jnp.int32, (L, L), 1)
    causal = row >= col
    tri = jnp.where(causal, 1.0, 0.0).astype(bf16)
    logf_hi = logf.astype(bf16)
    logf_lo = (logf - logf_hi.astype(f32)).astype(bf16)
    bcum = _dot(tri, logf_hi) + _dot(tri, logf_lo)
    r = gi - bcum
    r_t = r.T
    b_last = bcum[L - 1:L, :]
    g = b_last + r
    m_loc = jnp.max(g, axis=0, keepdims=True)
    ones = jnp.ones((L, d), bf16)

    for h in range(M_HEADS):
        hs = slice(h * d, (h + 1) * d)
        qh = qk[:, hs].astype(bf16)
        kh = qk[:, M_WIDTH + h * d:M_WIDTH + (h + 1) * d] * (d ** -0.5)
        vaug = jnp.concatenate([mv_ref[:, hs], ones], axis=1)
        m_in = m_st[h][0:1, 0:1]
        b_col = bcum[:, h:h + 1]
        dmat = jnp.where(causal, b_col + r_t[h:h + 1, :], NEG_BIG)
        a_t = b_col + m_in
        m_t = jnp.maximum(a_t, jnp.max(dmat, axis=1, keepdims=True))
        inter_w = jnp.exp(a_t - m_t)
        s = _dot_nt(qh, kh.astype(bf16)) * jnp.exp(dmat - m_t)
        tot = _dot(s.astype(bf16), vaug) + inter_w * _dot(qh, c_st[h].astype(bf16))
        den = jnp.maximum(jnp.abs(tot[:, d:]), jnp.exp(-m_t))
        hh = tot[:, :d] / den

        bl = b_last[:, h:h + 1]
        m_new = jnp.maximum(bl + m_in, m_loc[:, h:h + 1])
        w_col = jnp.exp(g[:, h:h + 1] - m_new)
        kw_t = (kh * w_col).T.astype(bf16)
        c_st[h] = jnp.exp(bl + m_in - m_new) * c_st[h] + _dot(kw_t, vaug)
        m_st[h] = jnp.broadcast_to(m_new, (SUBLANES, LANES))

        hn = hh * lax.rsqrt(jnp.mean(hh * hh, axis=-1, keepdims=True) + RMS_EPS) * mg_ref[:, hs]
        o_ref[:, hs] = (_sigmoid(mo_ref[:, hs].astype(f32)) * hn).astype(bf16)


def _mlstm(mqk, mv, mo, gates, cw, cb, bi, bfg, mg, batch, seq):
    L = M_CHUNK
    nc = seq // L
    tok = lambda width: pl.BlockSpec((L, width), lambda b, c: (b * nc + c, 0))
    return pl.pallas_call(
        _mlstm_kernel,
        grid=(batch, nc),
        in_specs=[tok(2 * M_WIDTH), tok(M_WIDTH), tok(M_WIDTH), tok(2 * LANES),
                  _const_spec(cw.shape), _const_spec(cb.shape), _const_spec(bi.shape),
                  _const_spec(bfg.shape), _const_spec(mg.shape)],
        out_specs=tok(M_WIDTH),
        out_shape=jax.ShapeDtypeStruct((batch * seq, M_WIDTH), bf16),
        scratch_shapes=[pltpu.VMEM((M_HEADS, M_HEAD_DIM, 2 * M_HEAD_DIM), f32),
                        pltpu.VMEM((M_HEADS, SUBLANES, LANES), f32),
                        pltpu.VMEM((SUBLANES, 2 * M_WIDTH), f32)],
        compiler_params=pltpu.CompilerParams(dimension_semantics=("arbitrary", "arbitrary"),
                                             vmem_limit_bytes=VMEM_LIMIT),
        name="mlstm",
    )(mqk, mv, mo, gates, cw, cb, bi, bfg, mg)


def _attn_kernel(tab_ref, q_ref, k_ref, v_ref, bias_ref, lam_ref, ng_ref, qg_ref, o_ref,
                 vaug_ref, q2_ref, acc_ref, m_ref, mfix_ref, kmax_ref, safe_ref, *, lam_init):
    T = A_TILE
    dv = A_V_DIM
    h = pl.program_id(1)
    qi = pl.program_id(2)
    sel = jnp.where((lax.broadcasted_iota(jnp.int32, (LANES, 2 * LANES), 0) < A_QK_DIM)
                    == (lax.broadcasted_iota(jnp.int32, (LANES, 2 * LANES), 1) < LANES), 1.0, 0.0).astype(bf16)

    def sq_norms(x):
        xf = x.astype(f32)
        return _dot((xf * xf).astype(bf16), sel)

    bmax = jnp.float32(0.0)
    bmin = jnp.float32(0.0)
    for b in range(N_BUCKETS - 1):
        rel = (tab_ref[h, b] - tab_ref[h, N_BUCKETS - 1]) * LOG2E
        bmax = jnp.maximum(bmax, rel)
        bmin = jnp.minimum(bmin, rel)

    @pl.when(qi == 0)
    def _():
        vaug_ref[:, 0:dv] = v_ref[...]
        vaug_ref[:, dv:2 * dv] = jnp.ones((v_ref.shape[0], dv), bf16)
        kmax = jnp.zeros((1, 2 * LANES), f32)
        for c in range(k_ref.shape[0] // A_NORM_ROWS):
            kn2 = sq_norms(k_ref[c * A_NORM_ROWS:(c + 1) * A_NORM_ROWS, :])
            kmax = jnp.maximum(kmax, jnp.max(kn2, axis=0, keepdims=True))
        kmax_ref[...] = kmax
        qmax = math.sqrt(A_QK_DIM) * jnp.max(jnp.abs(qg_ref[...]))
        span = 2.0 * qmax * jnp.sqrt(jnp.max(kmax)) + (bmax - bmin)
        safe_ref[0] = (span < A_SAFE_RANGE).astype(jnp.int32)

    q = q_ref[...]
    lane = lax.broadcasted_iota(jnp.int32, q.shape, 1)
    zero = jnp.zeros_like(q)
    q2_ref[0:T, :] = jnp.where(lane < A_QK_DIM, q, zero)
    q2_ref[T:2 * T, :] = jnp.where(lane >= A_QK_DIM, q, zero)
    acc_ref[...] = jnp.zeros_like(acc_ref)

    bound = jnp.sqrt(sq_norms(q) * kmax_ref[...])
    mfix_ref[0:T, :] = bound[:, 0:LANES] + bmax
    mfix_ref[T:2 * T, :] = bound[:, LANES:2 * LANES] + bmax
    safe = safe_ref[0] == 1

    def step(start, nk, near, fixed_shift):
        start = pl.multiple_of(start, T)
        for rb in range(2 * T // A_ROWS):
            rows = slice(rb * A_ROWS, (rb + 1) * A_ROWS)
            brow = (rb * A_ROWS) % T
            nkb = nk - (T - brow - A_ROWS) if near else nk
            kt = k_ref[pl.ds(start, nkb), :]
            va = vaug_ref[pl.ds(start, nkb), :]
            s = _dot_nt(q2_ref[rows, :], kt)
            if near:
                s = s + bias_ref[0, brow:brow + A_ROWS, 2 * T - nk:2 * T - nk + nkb]
            if fixed_shift:
                mrow = mfix_ref[rows, :]
                p = jnp.exp2(s - jnp.concatenate([mrow] * (nkb // LANES), axis=1)).astype(bf16)
                acc_ref[rows, :] += _dot(p, va)
            else:
                m_old = m_ref[rows, :]
                m_new = jnp.maximum(m_old, jnp.max(s, axis=1, keepdims=True))
                p = jnp.exp2(s - m_new).astype(bf16)
                acc_ref[rows, :] = jnp.exp2(m_old - m_new) * acc_ref[rows, :] + _dot(p, va)
                m_ref[rows, :] = m_new

    def sweep(fixed_shift):
        nfar = jnp.maximum(qi - 1, 0)
        npair = nfar // 2

        def far_body(j, carry):
            step(j * (2 * T), 2 * T, False, fixed_shift)
            return carry

        lax.fori_loop(0, npair, far_body, 0)

        @pl.when(nfar % 2 == 1)
        def _():
            step(npair * (2 * T), T, False, fixed_shift)

        @pl.when(qi >= 1)
        def _():
            step((qi - 1) * T, 2 * T, True, fixed_shift)

        @pl.when(qi == 0)
        def _():
            step(0, T, True, fixed_shift)

    @pl.when(safe)
    def _():
        sweep(True)

    @pl.when(jnp.logical_not(safe))
    def _():
        m_ref[...] = jnp.full_like(m_ref, NEG_BIG)
        sweep(False)

    al = lam_ref[...]
    lam = (jnp.exp(jnp.sum(al[0:1] * al[1:2], keepdims=True))
           - jnp.exp(jnp.sum(al[2:3] * al[3:4], keepdims=True)) + lam_init)
    acc = acc_ref[...]
    o = acc[:, 0:dv] / acc[:, dv:2 * dv]
    ha = o[0:T] - lam * o[T:2 * T]
    hn = ha * lax.rsqrt(jnp.mean(ha * ha, axis=-1, keepdims=True) + RMS_EPS) * ng_ref[...]
    o_ref[...] = (hn * (1.0 - lam_init)).astype(bf16)


def _diffattn(tab, qn, kn, av, bias, lam_par, ng, qg, batch, seq, lam_init):
    T = A_TILE
    nq = seq // T
    return pl.pallas_call(
        functools.partial(_attn_kernel, lam_init=lam_init),
        grid=(batch, A_HEADS, nq),
        in_specs=[pl.BlockSpec(memory_space=pltpu.SMEM),
                  pl.BlockSpec((T, LANES), lambda b, h, i: (b * nq + i, h)),
                  pl.BlockSpec((seq, LANES), lambda b, h, i: (b, h)),
                  pl.BlockSpec((seq, LANES), lambda b, h, i: (b, h)),
                  pl.BlockSpec((1, T, 2 * T), lambda b, h, i: (h, 0, 0)),
                  _const_spec(lam_par.shape),
                  pl.BlockSpec((1, LANES), lambda b, h, i: (0, h)),
                  pl.BlockSpec((1, LANES), lambda b, h, i: (0, h))],
        out_specs=pl.BlockSpec((T, LANES), lambda b, h, i: (b * nq + i, h)),
        out_shape=jax.ShapeDtypeStruct((batch * seq, A_WIDTH), bf16),
        scratch_shapes=[pltpu.VMEM((seq, 2 * A_V_DIM), bf16),
                        pltpu.VMEM((2 * T, LANES), bf16),
                        pltpu.VMEM((2 * T, 2 * A_V_DIM), f32),
                        pltpu.VMEM((2 * T, 1), f32),
                        pltpu.VMEM((2 * T, LANES), f32),
                        pltpu.VMEM((1, 2 * LANES), f32),
                        pltpu.SMEM((1,), jnp.int32)],
        compiler_params=pltpu.CompilerParams(
            dimension_semantics=("arbitrary", "arbitrary", "arbitrary"),
            vmem_limit_bytes=VMEM_LIMIT),
        name="diffattn",
    )(tab, qn, kn, av, bias, lam_par, ng, qg)


def _merge_kernel(x_ref, mod_ref, hm_ref, ha_ref, gm_ref, ga_ref, wm_ref, wa_ref, wo_ref, o_ref):
    ym = _dot(hm_ref[...], wm_ref[...])
    ya = _dot(ha_ref[...], wa_ref[...])
    y = _sigmoid(gm_ref[...].astype(f32)) * ym + _sigmoid(ga_ref[...].astype(f32)) * ya
    o_ref[...] = x_ref[...] + mod_ref[0][2:3] * _dot(y.astype(bf16), wo_ref[...])


def _merge(x2, mod3, hm, ha, gm, ga, wm, wa, wo, tiles_per_batch):
    n = x2.shape[0]
    tm = TM_PROJ
    tok = lambda width: pl.BlockSpec((tm, width), lambda i: (i, 0))
    return pl.pallas_call(
        _merge_kernel,
        grid=(n // tm,),
        in_specs=[tok(D_MODEL),
                  pl.BlockSpec((1, 6, D_MODEL), lambda i: (i // tiles_per_batch, 0, 0)),
                  tok(M_WIDTH), tok(A_WIDTH), tok(D_MODEL), tok(D_MODEL),
                  _const_spec(wm.shape), _const_spec(wa.shape), _const_spec(wo.shape)],
        out_specs=tok(D_MODEL),
        out_shape=jax.ShapeDtypeStruct((n, D_MODEL), f32),
        compiler_params=pltpu.CompilerParams(dimension_semantics=("arbitrary",),
                                             vmem_limit_bytes=VMEM_LIMIT),
        name="merge",
    )(x2, mod3, hm, ha, gm, ga, wm, wa, wo)


def _ffn_kernel(x_ref, mod_ref, g_ref, wv_ref, wg_ref, cwv_ref, cwg_ref, cbv_ref, cbg_ref, wd_ref,
                o_ref, hv_ref, hg_ref, ubuf_ref, act_ref, *, tiles_per_batch):
    tm = x_ref.shape[0]

    @pl.when(pl.program_id(0) % tiles_per_batch == 0)
    def _():
        hv_ref[...] = jnp.zeros_like(hv_ref)
        hg_ref[...] = jnp.zeros_like(hg_ref)

    x = x_ref[...]
    mod = mod_ref[0]
    ms = jnp.mean(x * x, axis=-1, keepdims=True)
    y = x * lax.rsqrt(ms + RMS_EPS) * g_ref[...]
    hb = (y * (1.0 + mod[4:5]) + mod[3:4]).astype(bf16)

    def conv(u, halo, buf, w, b):
        buf[0:SUBLANES, :] = halo
        buf[SUBLANES:SUBLANES + tm, :] = u
        return (w[2:3] * u + w[1:2] * buf[SUBLANES - 1:SUBLANES - 1 + tm, :]
                + w[0:1] * buf[SUBLANES - 2:SUBLANES - 2 + tm, :] + b)

    for j in range(N_FCHUNK):
        cols = slice(j * F_CHUNK, (j + 1) * F_CHUNK)
        uv = _dot(hb, wv_ref[j])
        ug = _dot(hb, wg_ref[j])
        cv = conv(uv, hv_ref[j], ubuf_ref.at[(2 * j) % N_UBUF], cwv_ref[j], cbv_ref[j])
        cg = conv(ug, hg_ref[j], ubuf_ref.at[(2 * j + 1) % N_UBUF], cwg_ref[j], cbg_ref[j])
        hv_ref[j] = uv[tm - SUBLANES:, :]
        hg_ref[j] = ug[tm - SUBLANES:, :]
        act_ref[:, cols] = ((cg * _sigmoid(cg)) * cv).astype(bf16)
    o_ref[...] = x + mod[5:6] * _dot(act_ref[...], wd_ref[...])


def _ffn(x2, mod3, g2, wv, wg, cwv, cwg, cbv, cbg, wd, tiles_per_batch):
    n = x2.shape[0]
    tm = TM_PROJ
    tok = lambda width: pl.BlockSpec((tm, width), lambda i: (i, 0))
    return pl.pallas_call(
        functools.partial(_ffn_kernel, tiles_per_batch=tiles_per_batch),
        grid=(n // tm,),
        in_specs=[tok(D_MODEL),
                  pl.BlockSpec((1, 6, D_MODEL), lambda i: (i // tiles_per_batch, 0, 0)),
                  _const_spec(g2.shape), _const_spec(wv.shape), _const_spec(wg.shape),
                  _const_spec(cwv.shape), _const_spec(cwg.shape), _const_spec(cbv.shape),
                  _const_spec(cbg.shape), _const_spec(wd.shape)],
        out_specs=tok(D_MODEL),
        out_shape=jax.ShapeDtypeStruct((n, D_MODEL), f32),
        scratch_shapes=[pltpu.VMEM((N_FCHUNK, SUBLANES, F_CHUNK), f32),
                        pltpu.VMEM((N_FCHUNK, SUBLANES, F_CHUNK), f32),
                        pltpu.VMEM((N_UBUF, tm + SUBLANES, F_CHUNK), f32),
                        pltpu.VMEM((tm, D_FF), bf16)],
        compiler_params=pltpu.CompilerParams(dimension_semantics=("arbitrary",),
                                             vmem_limit_bytes=VMEM_LIMIT),
        name="ffn",
    )(x2, mod3, g2, wv, wg, cwv, cwg, cbv, cbg, wd)


def _pad_cols(a, width):
    return jnp.pad(a, ((0, 0), (0, width - a.shape[1])))


def _layer(x2, c8, batch, seq, layer, w_ada, b_ada, norm1_g, w_in, m_conv_w, m_conv_b, m_igate_b,
           m_fgate_b, m_norm_g, a_qnorm_g, a_knorm_g, a_lambda, a_norm_g, tab, bias, w_branch_m,
           w_branch_a, w_out, norm2_g, w_up, ffn_conv_w, ffn_conv_b, w_down):
    tiles_per_batch = seq // TM_PROJ
    mod3 = _adaln(c8, w_ada, b_ada.reshape(1, -1))[:batch].reshape(batch, 6, D_MODEL)

    o = 0
    parts = {}
    for name, size in (("mqk", 2 * M_WIDTH), ("mv", M_WIDTH), ("mo", M_WIDTH), ("mi", M_HEADS),
                       ("mf", M_HEADS), ("aq", A_WIDTH), ("ak", A_WIDTH), ("av", A_WIDTH),
                       ("gm", D_MODEL), ("ga", D_MODEL)):
        parts[name] = w_in[:, o:o + size]
        o += size

    def per_head(w):
        return w.reshape(D_MODEL, 2, A_HEADS, A_QK_DIM).transpose(0, 2, 1, 3).reshape(D_MODEL, A_WIDTH)

    w_cat = jnp.concatenate(
        [parts["mqk"], parts["mv"], parts["mo"], per_head(parts["aq"]), per_head(parts["ak"]),
         parts["av"], parts["gm"], parts["ga"], _pad_cols(parts["mi"], LANES),
         _pad_cols(parts["mf"], LANES)], axis=1).astype(bf16)
    gid = jnp.arange(2 * LANES) // A_QK_DIM
    grp = jnp.where(gid[:, None] == gid[None, :], 1.0 / A_QK_DIM, 0.0).astype(bf16)
    qg = (jnp.tile(a_qnorm_g, A_WIDTH // A_QK_DIM) * (A_QK_DIM ** -0.5 * LOG2E)).reshape(1, A_WIDTH)
    kg = jnp.tile(a_knorm_g, A_WIDTH // A_QK_DIM).reshape(1, A_WIDTH)

    mqk, mv, mo, qn, kn, av, gm, ga, gates = _inproj(
        x2, mod3, norm1_g.reshape(1, -1), w_cat, grp, qg, kg, tiles_per_batch)

    hm = _mlstm(mqk, mv, mo, gates, m_conv_w, m_conv_b.reshape(1, -1),
                _pad_cols(m_igate_b.reshape(1, -1), LANES), _pad_cols(m_fgate_b.reshape(1, -1), LANES),
                m_norm_g.reshape(1, -1), batch, seq)

    lam_init = 0.8 - 0.6 * math.exp(-0.3 * layer)
    ha = _diffattn(tab, qn, kn, av, bias, a_lambda, a_norm_g.reshape(1, -1), qg, batch, seq, lam_init)

    x1 = _merge(x2, mod3, hm, ha, gm, ga, w_branch_m.astype(bf16), w_branch_a.astype(bf16),
                w_out.astype(bf16), tiles_per_batch)

    def chunks(a):
        return a.reshape(a.shape[0], N_FCHUNK, F_CHUNK).transpose(1, 0, 2)

    wv = chunks(w_up[:, :D_FF]).astype(bf16)
    wg = chunks(w_up[:, D_FF:]).astype(bf16)
    cwv = chunks(ffn_conv_w[:, :D_FF])
    cwg = chunks(ffn_conv_w[:, D_FF:])
    cbv = chunks(ffn_conv_b[None, :D_FF])
    cbg = chunks(ffn_conv_b[None, D_FF:])
    wd = w_down.astype(bf16)
    return _ffn(x1, mod3, norm2_g.reshape(1, -1), wv, wg, cwv, cwg, cbv, cbg, wd, tiles_per_batch)


def kernel(x, c, w_ada, b_ada, norm1_g, w_in, m_conv_w, m_conv_b, m_igate_b, m_fgate_b, m_norm_g,
           a_qnorm_g, a_knorm_g, a_lambda, a_norm_g, rel_bias, w_branch_m, w_branch_a, w_out, norm2_g,
           w_up, ffn_conv_w, ffn_conv_b, w_down):
    batch, seq, _ = x.shape
    depth = w_ada.shape[0]
    x2 = x.reshape(batch * seq, D_MODEL)
    c8 = jnp.pad(c, ((0, SUBLANES - batch), (0, 0)))
    tab = rel_bias.astype(f32).T
    bias = _bias_tiles(tab)
    for l in range(depth):
        x2 = _layer(x2, c8, batch, seq, l, w_ada[l], b_ada[l], norm1_g[l], w_in[l], m_conv_w[l],
                    m_conv_b[l], m_igate_b[l], m_fgate_b[l], m_norm_g[l], a_qnorm_g[l], a_knorm_g[l],
                    a_lambda[l], a_norm_g[l], tab, bias, w_branch_m[l], w_branch_a[l], w_out[l], norm2_g[l],
                    w_up[l], ffn_conv_w[l], ffn_conv_b[l], w_down[l])
    return x2.reshape(batch, seq, D_MODEL)
```

```python
import functools
import math

import jax
import jax.numpy as jnp
from jax import lax
from jax.experimental import pallas as pl
from jax.experimental.pallas import tpu as pltpu

D_MODEL = 1024
M_HEADS = 4
M_HEAD_DIM = 128
M_WIDTH = M_HEADS * M_HEAD_DIM
M_CONV = 4
A_HEADS = 4
A_QK_DIM = 64
A_V_DIM = 2 * A_QK_DIM
A_WIDTH = A_HEADS * A_V_DIM
N_BUCKETS = 32
MAX_DISTANCE = 128
D_FF = 2816
FFN_CONV = 3
RMS_EPS = 1e-6
LOG2E = math.log2(math.e)
NEG_BIG = -1e30

LANES = 128
SUBLANES = 8
VMEM_LIMIT = 56 * 1024 * 1024

TM_PROJ = 512
M_CHUNK = 256
A_TILE = 512
A_ROWS = 256
A_HPS = 2
A_NORM_ROWS = 1024
A_SAFE_RANGE = 90.0
F_CHUNK = 256
N_FCHUNK = D_FF // F_CHUNK
N_UBUF = 4

bf16 = jnp.bfloat16
f32 = jnp.float32


def _dot(a, b):
    return jnp.dot(a, b, preferred_element_type=f32)


def _dot_nt(a, b):
    return lax.dot_general(a, b, (((1,), (1,)), ((), ())), preferred_element_type=f32)


def _sigmoid(x):
    return 1.0 / (1.0 + jnp.exp(-x))


def _const_spec(shape):
    nd = len(shape)
    return pl.BlockSpec(shape, lambda *_: (0,) * nd)


def _shift_rows(u, prev8, k):
    r = pltpu.roll(u, k, 0)
    rp = pltpu.roll(prev8, k, 0)
    row = lax.broadcasted_iota(jnp.int32, (SUBLANES, u.shape[1]), 0)
    first = jnp.where(row < k, rp, r[:SUBLANES])
    return jnp.concatenate([first, r[SUBLANES:]], axis=0)


def _adaln_kernel(c_ref, w_ref, b_ref, o_ref):
    c = c_ref[...]
    a = (c * _sigmoid(c)).astype(bf16)
    o_ref[...] = _dot(a, w_ref[...].astype(bf16)) + b_ref[...]


def _adaln(c8, w, b):
    n = w.shape[1]
    tn = 1536
    return pl.pallas_call(
        _adaln_kernel,
        grid=(n // tn,),
        in_specs=[_const_spec(c8.shape),
                  pl.BlockSpec((D_MODEL, tn), lambda j: (0, j)),
                  pl.BlockSpec((1, tn), lambda j: (0, j))],
        out_specs=pl.BlockSpec((c8.shape[0], tn), lambda j: (0, j)),
        out_shape=jax.ShapeDtypeStruct((c8.shape[0], n), f32),
        compiler_params=pltpu.CompilerParams(dimension_semantics=("arbitrary",),
                                             vmem_limit_bytes=VMEM_LIMIT),
        name="adaln",
    )(c8, w, b)


def _bias_kernel(tab_ref, o_ref):
    h = pl.program_id(0)
    t = pl.program_id(1)
    row = lax.broadcasted_iota(jnp.int32, (A_TILE, A_TILE), 0)
    col = lax.broadcasted_iota(jnp.int32, (A_TILE, A_TILE), 1)
    dist = row - col + (1 - t) * A_TILE
    n = jnp.maximum(dist, 0)
    max_exact = N_BUCKETS // 2
    nf = jnp.maximum(n, 1).astype(f32)
    large = max_exact + (jnp.log(nf / max_exact) / math.log(MAX_DISTANCE / max_exact)
                         * (N_BUCKETS - max_exact)).astype(jnp.int32)
    large = jnp.minimum(large, N_BUCKETS - 1)
    bucket = jnp.where(n < max_exact, n, large)
    far = tab_ref[h, N_BUCKETS - 1]
    val = jnp.zeros((A_TILE, A_TILE), f32)
    for b in range(N_BUCKETS - 1):
        val = jnp.where(bucket == b, tab_ref[h, b] - far, val)
    o_ref[0] = jnp.where(dist >= 0, val * LOG2E, NEG_BIG)


def _bias_tiles(tab):
    return pl.pallas_call(
        _bias_kernel,
        grid=(A_HEADS, 2),
        in_specs=[pl.BlockSpec(memory_space=pltpu.SMEM)],
        out_specs=pl.BlockSpec((1, A_TILE, A_TILE), lambda h, t: (h, 0, t)),
        out_shape=jax.ShapeDtypeStruct((A_HEADS, A_TILE, 2 * A_TILE), f32),
        compiler_params=pltpu.CompilerParams(dimension_semantics=("arbitrary", "arbitrary"),
                                             vmem_limit_bytes=VMEM_LIMIT),
        name="bias_tiles",
    )(tab)


C_MQK, C_MV, C_MO, C_AQ, C_AK, C_AV, C_GM, C_GA, C_GATE, C_END = (
    0, 1024, 1536, 2048, 2560, 3072, 3584, 4608, 5632, 5888)


def _inproj_kernel(x_ref, mod_ref, g_ref, w_ref, grp_ref, qg_ref, kg_ref,
                   mqk_ref, mv_ref, mo_ref, qn_ref, kn_ref, av_ref, gm_ref, ga_ref, gate_ref):
    x = x_ref[...]
    mod = mod_ref[0]
    ms = jnp.mean(x * x, axis=-1, keepdims=True)
    y = x * lax.rsqrt(ms + RMS_EPS) * g_ref[...]
    hb = (y * (1.0 + mod[1:2]) + mod[0:1]).astype(bf16)

    def proj(c0, c1):
        return _dot(hb, w_ref[:, c0:c1])

    def qknorm(a, gain_ref):
        sq = (a * a).astype(bf16)
        gw = grp_ref.shape[0]
        msq = jnp.concatenate([_dot(sq[:, c:c + gw], grp_ref[...]) for c in range(0, a.shape[1], gw)], axis=1)
        return (a * lax.rsqrt(msq + RMS_EPS) * gain_ref[...]).astype(bf16)

    mqk_ref[:, 0:512] = proj(C_MQK, C_MQK + 512).astype(bf16)
    mqk_ref[:, 512:1024] = proj(C_MQK + 512, C_MV).astype(bf16)
    mv_ref[...] = proj(C_MV, C_MO).astype(bf16)
    mo_ref[...] = proj(C_MO, C_AQ).astype(bf16)
    qn_ref[...] = qknorm(proj(C_AQ, C_AK), qg_ref)
    kn_ref[...] = qknorm(proj(C_AK, C_AV), kg_ref)
    av_ref[...] = proj(C_AV, C_GM).astype(bf16)
    gm_ref[:, 0:512] = proj(C_GM, C_GM + 512).astype(bf16)
    gm_ref[:, 512:1024] = proj(C_GM + 512, C_GA).astype(bf16)
    ga_ref[:, 0:512] = proj(C_GA, C_GA + 512).astype(bf16)
    ga_ref[:, 512:1024] = proj(C_GA + 512, C_GATE).astype(bf16)
    gate_ref[...] = proj(C_GATE, C_END)


def _inproj(x2, mod3, g1, w, grp, qg, kg, tiles_per_batch):
    n = x2.shape[0]
    tm = TM_PROJ
    tok = lambda width: pl.BlockSpec((tm, width), lambda i: (i, 0))
    outs = [(1024, bf16), (512, bf16), (512, bf16), (512, bf16), (512, bf16), (512, bf16),
            (1024, bf16), (1024, bf16), (2 * LANES, f32)]
    return pl.pallas_call(
        _inproj_kernel,
        grid=(n // tm,),
        in_specs=[tok(D_MODEL),
                  pl.BlockSpec((1, 6, D_MODEL), lambda i: (i // tiles_per_batch, 0, 0)),
                  _const_spec(g1.shape), _const_spec(w.shape), _const_spec(grp.shape),
                  _const_spec(qg.shape), _const_spec(kg.shape)],
        out_specs=[tok(wd) for wd, _ in outs],
        out_shape=[jax.ShapeDtypeStruct((n, wd), dt) for wd, dt in outs],
        compiler_params=pltpu.CompilerParams(dimension_semantics=("arbitrary",),
                                             vmem_limit_bytes=VMEM_LIMIT),
        name="inproj",
    )(x2, mod3, g1, w, grp, qg, kg)


def _mlstm_kernel(mqk_ref, mv_ref, mo_ref, gate_ref, cw_ref, cb_ref, bi_ref, bfg_ref, mg_ref,
                  o_ref, c_st, m_st, tail_ref):
    L = M_CHUNK
    d = M_HEAD_DIM

    @pl.when(pl.program_id(1) == 0)
    def _():
        c_st[...] = jnp.zeros_like(c_st)
        m_st[...] = jnp.zeros_like(m_st)
        tail_ref[...] = jnp.zeros_like(tail_ref)

    x = mqk_ref[...].astype(f32)
    prev8 = tail_ref[...]
    cw = cw_ref[...]
    y = (cw[3:4] * x + cw[2:3] * _shift_rows(x, prev8, 1) + cw[1:2] * _shift_rows(x, prev8, 2)
         + cw[0:1] * _shift_rows(x, prev8, 3) + cb_ref[...])
    tail_ref[...] = x[L - SUBLANES:, :]
    qk = y * _sigmoid(y)

    gi = gate_ref[:, 0:LANES] + bi_ref[...]
    gf = gate_ref[:, LANES:2 * LANES] + bfg_ref[...]
    logf = jnp.minimum(gf, 0.0) - jnp.log(1.0 + jnp.exp(-jnp.abs(gf)))
    row = lax.broadcasted_iota(jnp.int32, (L, L), 0)
    col = lax.broadcasted_iota(jnp.int32, (L, L), 1)
    causal = row >= col
    tri = jnp.where(causal, 1.0, 0.0).astype(bf16)
    logf_hi = logf.astype(bf16)
    logf_lo = (logf - logf_hi.astype(f32)).astype(bf16)
    bcum = _dot(tri, logf_hi) + _dot(tri, logf_lo)
    r = gi - bcum
    r_t = r.T
    b_last = bcum[L - 1:L, :]
    g = b_last + r
    m_loc = jnp.max(g, axis=0, keepdims=True)
    ones = jnp.ones((L, d), bf16)

    for h in range(M_HEADS):
        hs = slice(h * d, (h + 1) * d)
        qh = qk[:, hs].astype(bf16)
        kh = qk[:, M_WIDTH + h * d:M_WIDTH + (h + 1) * d] * (d ** -0.5)
        vaug = jnp.concatenate([mv_ref[:, hs], ones], axis=1)
        m_in = m_st[h][0:1, 0:1]
        b_col = bcum[:, h:h + 1]
        dmat = jnp.where(causal, b_col + r_t[h:h + 1, :], NEG_BIG)
        a_t = b_col + m_in
        m_t = jnp.maximum(a_t, jnp.max(dmat, axis=1, keepdims=True))
        inter_w = jnp.exp(a_t - m_t)
        s = _dot_nt(qh, kh.astype(bf16)) * jnp.exp(dmat - m_t)
        tot = _dot(s.astype(bf16), vaug) + inter_w * _dot(qh, c_st[h].astype(bf16))
        den = jnp.maximum(jnp.abs(tot[:, d:]), jnp.exp(-m_t))
        hh = tot[:, :d] / den

        bl = b_last[:, h:h + 1]
        m_new = jnp.maximum(bl + m_in, m_loc[:, h:h + 1])
        w_col = jnp.exp(g[:, h:h + 1] - m_new)
        kw_t = (kh * w_col).T.astype(bf16)
        c_st[h] = jnp.exp(bl + m_in - m_new) * c_st[h] + _dot(kw_t, vaug)
        m_st[h] = jnp.broadcast_to(m_new, (SUBLANES, LANES))

        hn = hh * lax.rsqrt(jnp.mean(hh * hh, axis=-1, keepdims=True) + RMS_EPS) * mg_ref[:, hs]
        o_ref[:, hs] = (_sigmoid(mo_ref[:, hs].astype(f32)) * hn).astype(bf16)


def _mlstm(mqk, mv, mo, gates, cw, cb, bi, bfg, mg, batch, seq):
    L = M_CHUNK
    nc = seq // L
    tok = lambda width: pl.BlockSpec((L, width), lambda b, c: (b * nc + c, 0))
    return pl.pallas_call(
        _mlstm_kernel,
        grid=(batch, nc),
        in_specs=[tok(2 * M_WIDTH), tok(M_WIDTH), tok(M_WIDTH), tok(2 * LANES),
                  _const_spec(cw.shape), _const_spec(cb.shape), _const_spec(bi.shape),
                  _const_spec(bfg.shape), _const_spec(mg.shape)],
        out_specs=tok(M_WIDTH),
        out_shape=jax.ShapeDtypeStruct((batch * seq, M_WIDTH), bf16),
        scratch_shapes=[pltpu.VMEM((M_HEADS, M_HEAD_DIM, 2 * M_HEAD_DIM), f32),
                        pltpu.VMEM((M_HEADS, SUBLANES, LANES), f32),
                        pltpu.VMEM((SUBLANES, 2 * M_WIDTH), f32)],
        compiler_params=pltpu.CompilerParams(dimension_semantics=("arbitrary", "arbitrary"),
                                             vmem_limit_bytes=VMEM_LIMIT),
        name="mlstm",
    )(mqk, mv, mo, gates, cw, cb, bi, bfg, mg)


def _attn_kernel(tab_ref, q_ref, k_ref, v_ref, bias_ref, lam_ref, ng_ref, qg_ref, o_ref,
                 vaug_ref, q2_ref, acc_ref, m_ref, mfix_ref, kmax_ref, safe_ref, *, lam_init):
    T = A_TILE
    dv = A_V_DIM
    hg = pl.program_id(1)
    qi = pl.program_id(2)
    heads = range(A_HPS)
    sel = jnp.where((lax.broadcasted_iota(jnp.int32, (LANES, 2 * LANES), 0) < A_QK_DIM)
                    == (lax.broadcasted_iota(jnp.int32, (LANES, 2 * LANES), 1) < LANES), 1.0, 0.0).astype(bf16)

    def sq_norms(x):
        xf = x.astype(f32)
        return _dot((xf * xf).astype(bf16), sel)

    def lanes_of(j):
        return slice(j * LANES, (j + 1) * LANES)

    bmax, bmin = [], []
    for j in heads:
        hi = jnp.float32(0.0)
        lo = jnp.float32(0.0)
        for b in range(N_BUCKETS - 1):
            rel = (tab_ref[hg * A_HPS + j, b] - tab_ref[hg * A_HPS + j, N_BUCKETS - 1]) * LOG2E
            hi = jnp.maximum(hi, rel)
            lo = jnp.minimum(lo, rel)
        bmax.append(hi)
        bmin.append(lo)

    @pl.when(qi == 0)
    def _():
        ok = None
        for j in heads:
            vaug_ref[j, :, 0:dv] = v_ref[:, lanes_of(j)]
            vaug_ref[j, :, dv:2 * dv] = jnp.ones((v_ref.shape[0], dv), bf16)
            kmax = jnp.zeros((1, 2 * LANES), f32)
            for c in range(k_ref.shape[0] // A_NORM_ROWS):
                kn2 = sq_norms(k_ref[c * A_NORM_ROWS:(c + 1) * A_NORM_ROWS, lanes_of(j)])
                kmax = jnp.maximum(kmax, jnp.max(kn2, axis=0, keepdims=True))
            kmax_ref[j] = kmax
            qmax = math.sqrt(A_QK_DIM) * jnp.max(jnp.abs(qg_ref[:, lanes_of(j)]))
            span = 2.0 * qmax * jnp.sqrt(jnp.max(kmax)) + (bmax[j] - bmin[j])
            ok_j = span < A_SAFE_RANGE
            ok = ok_j if ok is None else jnp.logical_and(ok, ok_j)
        safe_ref[0] = ok.astype(jnp.int32)

    for j in heads:
        q = q_ref[:, lanes_of(j)]
        lane = lax.broadcasted_iota(jnp.int32, q.shape, 1)
        zero = jnp.zeros_like(q)
        q2_ref[j, 0:T, :] = jnp.where(lane < A_QK_DIM, q, zero)
        q2_ref[j, T:2 * T, :] = jnp.where(lane >= A_QK_DIM, q, zero)
        bound = jnp.sqrt(sq_norms(q) * kmax_ref[j])
        mfix_ref[j, 0:T, :] = bound[:, 0:LANES] + bmax[j]
        mfix_ref[j, T:2 * T, :] = bound[:, LANES:2 * LANES] + bmax[j]
    acc_ref[...] = jnp.zeros_like(acc_ref)
    safe = safe_ref[0] == 1

    def step(start, nk, near, fixed_shift):
        start = pl.multiple_of(start, T)
        for rb in range(2 * T // A_ROWS):
            rows = slice(rb * A_ROWS, (rb + 1) * A_ROWS)
            brow = (rb * A_ROWS) % T
            nkb = nk - (T - brow - A_ROWS) if near else nk
            for j in heads:
                kt = k_ref[pl.ds(start, nkb), lanes_of(j)]
                va = vaug_ref[j, pl.ds(start, nkb), :]
                s = _dot_nt(q2_ref[j, rows, :], kt)
                if near:
                    s = s + bias_ref[j, brow:brow + A_ROWS, 2 * T - nk:2 * T - nk + nkb]
                if fixed_shift:
                    mrow = mfix_ref[j, rows, :]
                    p = jnp.exp2(s - jnp.concatenate([mrow] * (nkb // LANES), axis=1)).astype(bf16)
                    acc_ref[j, rows, :] += _dot(p, va)
                else:
                    m_old = m_ref[j, rows, :]
                    m_new = jnp.maximum(m_old, jnp.max(s, axis=1, keepdims=True))
                    p = jnp.exp2(s - m_new).astype(bf16)
                    acc_ref[j, rows, :] = jnp.exp2(m_old - m_new) * acc_ref[j, rows, :] + _dot(p, va)
                    m_ref[j, rows, :] = m_new

    def sweep(fixed_shift):
        nfar = jnp.maximum(qi - 1, 0)
        npair = nfar // 2

        def far_body(i, carry):
            step(i * (2 * T), 2 * T, False, fixed_shift)
            return carry

        lax.fori_loop(0, npair, far_body, 0)

        @pl.when(nfar % 2 == 1)
        def _():
            step(npair * (2 * T), T, False, fixed_shift)

        @pl.when(qi >= 1)
        def _():
            step((qi - 1) * T, 2 * T, True, fixed_shift)

        @pl.when(qi == 0)
        def _():
            step(0, T, True, fixed_shift)

    @pl.when(safe)
    def _():
        sweep(True)

    @pl.when(jnp.logical_not(safe))
    def _():
        m_ref[...] = jnp.full_like(m_ref, NEG_BIG)
        sweep(False)

    al = lam_ref[...]
    lam = (jnp.exp(jnp.sum(al[0:1] * al[1:2], keepdims=True))
           - jnp.exp(jnp.sum(al[2:3] * al[3:4], keepdims=True)) + lam_init)
    for j in heads:
        acc = acc_ref[j]
        o = acc[:, 0:dv] / acc[:, dv:2 * dv]
        ha = o[0:T] - lam * o[T:2 * T]
        hn = ha * lax.rsqrt(jnp.mean(ha * ha, axis=-1, keepdims=True) + RMS_EPS) * ng_ref[:, lanes_of(j)]
        o_ref[:, lanes_of(j)] = (hn * (1.0 - lam_init)).astype(bf16)


def _diffattn(tab, qn, kn, av, bias, lam_par, ng, qg, batch, seq, lam_init):
    T = A_TILE
    nq = seq // T
    wide = A_HPS * LANES
    return pl.pallas_call(
        functools.partial(_attn_kernel, lam_init=lam_init),
        grid=(batch, A_HEADS // A_HPS, nq),
        in_specs=[pl.BlockSpec(memory_space=pltpu.SMEM),
                  pl.BlockSpec((T, wide), lambda b, h, i: (b * nq + i, h)),
                  pl.BlockSpec((seq, wide), lambda b, h, i: (b, h)),
                  pl.BlockSpec((seq, wide), lambda b, h, i: (b, h)),
                  pl.BlockSpec((A_HPS, T, 2 * T), lambda b, h, i: (h, 0, 0)),
                  _const_spec(lam_par.shape),
                  pl.BlockSpec((1, wide), lambda b, h, i: (0, h)),
                  pl.BlockSpec((1, wide), lambda b, h, i: (0, h))],
        out_specs=pl.BlockSpec((T, wide), lambda b, h, i: (b * nq + i, h)),
        out_shape=jax.ShapeDtypeStruct((batch * seq, A_WIDTH), bf16),
        scratch_shapes=[pltpu.VMEM((A_HPS, seq, 2 * A_V_DIM), bf16),
                        pltpu.VMEM((A_HPS, 2 * T, LANES), bf16),
                        pltpu.VMEM((A_HPS, 2 * T, 2 * A_V_DIM), f32),
                        pltpu.VMEM((A_HPS, 2 * T, 1), f32),
                        pltpu.VMEM((A_HPS, 2 * T, LANES), f32),
                        pltpu.VMEM((A_HPS, 1, 2 * LANES), f32),
                        pltpu.SMEM((1,), jnp.int32)],
        compiler_params=pltpu.CompilerParams(
            dimension_semantics=("arbitrary", "arbitrary", "arbitrary"),
            vmem_limit_bytes=VMEM_LIMIT),
        name="diffattn",
    )(tab, qn, kn, av, bias, lam_par, ng, qg)


def _merge_kernel(x_ref, mod_ref, hm_ref, ha_ref, gm_ref, ga_ref, wm_ref, wa_ref, wo_ref, o_ref):
    ym = _dot(hm_ref[...], wm_ref[...])
    ya = _dot(ha_ref[...], wa_ref[...])
    y = _sigmoid(gm_ref[...].astype(f32)) * ym + _sigmoid(ga_ref[...].astype(f32)) * ya
    o_ref[...] = x_ref[...] + mod_ref[0][2:3] * _dot(y.astype(bf16), wo_ref[...])


def _merge(x2, mod3, hm, ha, gm, ga, wm, wa, wo, tiles_per_batch):
    n = x2.shape[0]
    tm = TM_PROJ
    tok = lambda width: pl.BlockSpec((tm, width), lambda i: (i, 0))
    return pl.pallas_call(
        _merge_kernel,
        grid=(n // tm,),
        in_specs=[tok(D_MODEL),
                  pl.BlockSpec((1, 6, D_MODEL), lambda i: (i // tiles_per_batch, 0, 0)),
                  tok(M_WIDTH), tok(A_WIDTH), tok(D_MODEL), tok(D_MODEL),
                  _const_spec(wm.shape), _const_spec(wa.shape), _const_spec(wo.shape)],
        out_specs=tok(D_MODEL),
        out_shape=jax.ShapeDtypeStruct((n, D_MODEL), f32),
        compiler_params=pltpu.CompilerParams(dimension_semantics=("arbitrary",),
                                             vmem_limit_bytes=VMEM_LIMIT),
        name="merge",
    )(x2, mod3, hm, ha, gm, ga, wm, wa, wo)


def _ffn_kernel(x_ref, mod_ref, g_ref, wv_ref, wg_ref, cwv_ref, cwg_ref, cbv_ref, cbg_ref, wd_ref,
                o_ref, hv_ref, hg_ref, ubuf_ref, act_ref, *, tiles_per_batch):
    tm = x_ref.shape[0]

    @pl.when(pl.program_id(0) % tiles_per_batch == 0)
    def _():
        hv_ref[...] = jnp.zeros_like(hv_ref)
        hg_ref[...] = jnp.zeros_like(hg_ref)

    x = x_ref[...]
    mod = mod_ref[0]
    ms = jnp.mean(x * x, axis=-1, keepdims=True)
    y = x * lax.rsqrt(ms + RMS_EPS) * g_ref[...]
    hb = (y * (1.0 + mod[4:5]) + mod[3:4]).astype(bf16)

    def conv(u, halo, buf, w, b):
        buf[0:SUBLANES, :] = halo
        buf[SUBLANES:SUBLANES + tm, :] = u
        return (w[2:3] * u + w[1:2] * buf[SUBLANES - 1:SUBLANES - 1 + tm, :]
                + w[0:1] * buf[SUBLANES - 2:SUBLANES - 2 + tm, :] + b)

    for j in range(N_FCHUNK):
        cols = slice(j * F_CHUNK, (j + 1) * F_CHUNK)
        uv = _dot(hb, wv_ref[j])
        ug = _dot(hb, wg_ref[j])
        cv = conv(uv, hv_ref[j], ubuf_ref.at[(2 * j) % N_UBUF], cwv_ref[j], cbv_ref[j])
        cg = conv(ug, hg_ref[j], ubuf_ref.at[(2 * j + 1) % N_UBUF], cwg_ref[j], cbg_ref[j])
        hv_ref[j] = uv[tm - SUBLANES:, :]
        hg_ref[j] = ug[tm - SUBLANES:, :]
        act_ref[:, cols] = ((cg * _sigmoid(cg)) * cv).astype(bf16)
    o_ref[...] = x + mod[5:6] * _dot(act_ref[...], wd_ref[...])


def _ffn(x2, mod3, g2, wv, wg, cwv, cwg, cbv, cbg, wd, tiles_per_batch):
    n = x2.shape[0]
    tm = TM_PROJ
    tok = lambda width: pl.BlockSpec((tm, width), lambda i: (i, 0))
    return pl.pallas_call(
        functools.partial(_ffn_kernel, tiles_per_batch=tiles_per_batch),
        grid=(n // tm,),
        in_specs=[tok(D_MODEL),
                  pl.BlockSpec((1, 6, D_MODEL), lambda i: (i // tiles_per_batch, 0, 0)),
                  _const_spec(g2.shape), _const_spec(wv.shape), _const_spec(wg.shape),
                  _const_spec(cwv.shape), _const_spec(cwg.shape), _const_spec(cbv.shape),
                  _const_spec(cbg.shape), _const_spec(wd.shape)],
        out_specs=tok(D_MODEL),
        out_shape=jax.ShapeDtypeStruct((n, D_MODEL), f32),
        scratch_shapes=[pltpu.VMEM((N_FCHUNK, SUBLANES, F_CHUNK), f32),
                        pltpu.VMEM((N_FCHUNK, SUBLANES, F_CHUNK), f32),
                        pltpu.VMEM((N_UBUF, tm + SUBLANES, F_CHUNK), f32),
                        pltpu.VMEM((tm, D_FF), bf16)],
        compiler_params=pltpu.CompilerParams(dimension_semantics=("arbitrary",),
                                             vmem_limit_bytes=VMEM_LIMIT),
        name="ffn",
    )(x2, mod3, g2, wv, wg, cwv, cwg, cbv, cbg, wd)


def _pad_cols(a, width):
    return jnp.pad(a, ((0, 0), (0, width - a.shape[1])))


def _layer(x2, c8, batch, seq, layer, w_ada, b_ada, norm1_g, w_in, m_conv_w, m_conv_b, m_igate_b,
           m_fgate_b, m_norm_g, a_qnorm_g, a_knorm_g, a_lambda, a_norm_g, tab, bias, w_branch_m,
           w_branch_a, w_out, norm2_g, w_up, ffn_conv_w, ffn_conv_b, w_down):
    tiles_per_batch = seq // TM_PROJ
    mod3 = _adaln(c8, w_ada, b_ada.reshape(1, -1))[:batch].reshape(batch, 6, D_MODEL)

    o = 0
    parts = {}
    for name, size in (("mqk", 2 * M_WIDTH), ("mv", M_WIDTH), ("mo", M_WIDTH), ("mi", M_HEADS),
                       ("mf", M_HEADS), ("aq", A_WIDTH), ("ak", A_WIDTH), ("av", A_WIDTH),
                       ("gm", D_MODEL), ("ga", D_MODEL)):
        parts[name] = w_in[:, o:o + size]
        o += size

    def per_head(w):
        return w.reshape(D_MODEL, 2, A_HEADS, A_QK_DIM).transpose(0, 2, 1, 3).reshape(D_MODEL, A_WIDTH)

    w_cat = jnp.concatenate(
        [parts["mqk"], parts["mv"], parts["mo"], per_head(parts["aq"]), per_head(parts["ak"]),
         parts["av"], parts["gm"], parts["ga"], _pad_cols(parts["mi"], LANES),
         _pad_cols(parts["mf"], LANES)], axis=1).astype(bf16)
    gid = jnp.arange(2 * LANES) // A_QK_DIM
    grp = jnp.where(gid[:, None] == gid[None, :], 1.0 / A_QK_DIM, 0.0).astype(bf16)
    qg = (jnp.tile(a_qnorm_g, A_WIDTH // A_QK_DIM) * (A_QK_DIM ** -0.5 * LOG2E)).reshape(1, A_WIDTH)
    kg = jnp.tile(a_knorm_g, A_WIDTH // A_QK_DIM).reshape(1, A_WIDTH)

    mqk, mv, mo, qn, kn, av, gm, ga, gates = _inproj(
        x2, mod3, norm1_g.reshape(1, -1), w_cat, grp, qg, kg, tiles_per_batch)

    hm = _mlstm(mqk, mv, mo, gates, m_conv_w, m_conv_b.reshape(1, -1),
                _pad_cols(m_igate_b.reshape(1, -1), LANES), _pad_cols(m_fgate_b.reshape(1, -1), LANES),
                m_norm_g.reshape(1, -1), batch, seq)

    lam_init = 0.8 - 0.6 * math.exp(-0.3 * layer)
    ha = _diffattn(tab, qn, kn, av, bias, a_lambda, a_norm_g.reshape(1, -1), qg, batch, seq, lam_init)

    x1 = _merge(x2, mod3, hm, ha, gm, ga, w_branch_m.astype(bf16), w_branch_a.astype(bf16),
                w_out.astype(bf16), tiles_per_batch)

    def chunks(a):
        return a.reshape(a.shape[0], N_FCHUNK, F_CHUNK).transpose(1, 0, 2)

    wv = chunks(w_up[:, :D_FF]).astype(bf16)
    wg = chunks(w_up[:, D_FF:]).astype(bf16)
    cwv = chunks(ffn_conv_w[:, :D_FF])
    cwg = chunks(ffn_conv_w[:, D_FF:])
    cbv = chunks(ffn_conv_b[None, :D_FF])
    cbg = chunks(ffn_conv_b[None, D_FF:])
    wd = w_down.astype(bf16)
    return _ffn(x1, mod3, norm2_g.reshape(1, -1), wv, wg, cwv, cwg, cbv, cbg, wd, tiles_per_batch)


def kernel(x, c, w_ada, b_ada, norm1_g, w_in, m_conv_w, m_conv_b, m_igate_b, m_fgate_b, m_norm_g,
           a_qnorm_g, a_knorm_g, a_lambda, a_norm_g, rel_bias, w_branch_m, w_branch_a, w_out, norm2_g,
           w_up, ffn_conv_w, ffn_conv_b, w_down):
    batch, seq, _ = x.shape
    depth = w_ada.shape[0]
    x2 = x.reshape(batch * seq, D_MODEL)
    c8 = jnp.pad(c, ((0, SUBLANES - batch), (0, 0)))
    tab = rel_bias.astype(f32).T
    bias = _bias_tiles(tab)
    for l in range(depth):
        x2 = _layer(x2, c8, batch, seq, l, w_ada[l], b_ada[l], norm1_g[l], w_in[l], m_conv_w[l],
                    m_conv_b[l], m_igate_b[l], m_fgate_b[l], m_norm_g[l], a_qnorm_g[l], a_knorm_g[l],
                    a_lambda[l], a_norm_g[l], tab, bias, w_branch_m[l], w_branch_a[l], w_out[l], norm2_g[l],
                    w_up[l], ffn_conv_w[l], ffn_conv_b[l], w_down[l])
    return x2.reshape(batch, seq, D_MODEL)
```

```python
import functools
import math

import jax
import jax.numpy as jnp
from jax import lax
from jax.experimental import pallas as pl
from jax.experimental.pallas import tpu as pltpu

D_MODEL = 1024
M_HEADS = 4
M_HEAD_DIM = 128
M_WIDTH = M_HEADS * M_HEAD_DIM
M_CONV = 4
A_HEADS = 4
A_QK_DIM = 64
A_V_DIM = 2 * A_QK_DIM
A_WIDTH = A_HEADS * A_V_DIM
N_BUCKETS = 32
MAX_DISTANCE = 128
D_FF = 2816
FFN_CONV = 3
RMS_EPS = 1e-6
LOG2E = math.log2(math.e)
NEG_BIG = -1e30

LANES = 128
SUBLANES = 8
VMEM_LIMIT = 56 * 1024 * 1024

TM_PROJ = 512
M_CHUNK = 256
A_TILE = 512
A_ROWS = 256
A_HPS = 2
A_NORM_ROWS = 1024
A_SAFE_RANGE = 90.0
F_CHUNK = 256
N_FCHUNK = D_FF // F_CHUNK
N_UBUF = 4

bf16 = jnp.bfloat16
f32 = jnp.float32


def _dot(a, b):
    return jnp.dot(a, b, preferred_element_type=f32)


def _dot_nt(a, b):
    return lax.dot_general(a, b, (((1,), (1,)), ((), ())), preferred_element_type=f32)


def _sigmoid(x):
    return 1.0 / (1.0 + jnp.exp(-x))


def _const_spec(shape):
    nd = len(shape)
    return pl.BlockSpec(shape, lambda *_: (0,) * nd)


def _shift_rows(u, prev8, k):
    r = pltpu.roll(u, k, 0)
    rp = pltpu.roll(prev8, k, 0)
    row = lax.broadcasted_iota(jnp.int32, (SUBLANES, u.shape[1]), 0)
    first = jnp.where(row < k, rp, r[:SUBLANES])
    return jnp.concatenate([first, r[SUBLANES:]], axis=0)


def _adaln_kernel(c_ref, w_ref, b_ref, o_ref):
    c = c_ref[...]
    a = (c * _sigmoid(c)).astype(bf16)
    o_ref[...] = _dot(a, w_ref[...].astype(bf16)) + b_ref[...]


def _adaln(c8, w, b):
    n = w.shape[1]
    tn = 1536
    return pl.pallas_call(
        _adaln_kernel,
        grid=(n // tn,),
        in_specs=[_const_spec(c8.shape),
                  pl.BlockSpec((D_MODEL, tn), lambda j: (0, j)),
                  pl.BlockSpec((1, tn), lambda j: (0, j))],
        out_specs=pl.BlockSpec((c8.shape[0], tn), lambda j: (0, j)),
        out_shape=jax.ShapeDtypeStruct((c8.shape[0], n), f32),
        compiler_params=pltpu.CompilerParams(dimension_semantics=("arbitrary",),
                                             vmem_limit_bytes=VMEM_LIMIT),
        name="adaln",
    )(c8, w, b)


def _bias_kernel(tab_ref, o_ref):
    h = pl.program_id(0)
    t = pl.program_id(1)
    row = lax.broadcasted_iota(jnp.int32, (A_TILE, A_TILE), 0)
    col = lax.broadcasted_iota(jnp.int32, (A_TILE, A_TILE), 1)
    dist = row - col + (1 - t) * A_TILE
    n = jnp.maximum(dist, 0)
    max_exact = N_BUCKETS // 2
    nf = jnp.maximum(n, 1).astype(f32)
    large = max_exact + (jnp.log(nf / max_exact) / math.log(MAX_DISTANCE / max_exact)
                         * (N_BUCKETS - max_exact)).astype(jnp.int32)
    large = jnp.minimum(large, N_BUCKETS - 1)
    bucket = jnp.where(n < max_exact, n, large)
    far = tab_ref[h, N_BUCKETS - 1]
    val = jnp.zeros((A_TILE, A_TILE), f32)
    for b in range(N_BUCKETS - 1):
        val = jnp.where(bucket == b, tab_ref[h, b] - far, val)
    o_ref[0] = jnp.where(dist >= 0, val * LOG2E, NEG_BIG)


def _bias_tiles(tab):
    return pl.pallas_call(
        _bias_kernel,
        grid=(A_HEADS, 2),
        in_specs=[pl.BlockSpec(memory_space=pltpu.SMEM)],
        out_specs=pl.BlockSpec((1, A_TILE, A_TILE), lambda h, t: (h, 0, t)),
        out_shape=jax.ShapeDtypeStruct((A_HEADS, A_TILE, 2 * A_TILE), f32),
        compiler_params=pltpu.CompilerParams(dimension_semantics=("arbitrary", "arbitrary"),
                                             vmem_limit_bytes=VMEM_LIMIT),
        name="bias_tiles",
    )(tab)


C_MQK, C_MV, C_MO, C_AQ, C_AK, C_AV, C_GM, C_GA, C_GATE, C_END = (
    0, 1024, 1536, 2048, 2560, 3072, 3584, 4608, 5632, 5888)


def _inproj_kernel(x_ref, mod_ref, g_ref, w_ref, grp_ref, qg_ref, kg_ref,
                   mqk_ref, mv_ref, mo_ref, qn_ref, kn_ref, av_ref, gm_ref, ga_ref, gate_ref):
    x = x_ref[...]
    mod = mod_ref[0]
    ms = jnp.mean(x * x, axis=-1, keepdims=True)
    y = x * lax.rsqrt(ms + RMS_EPS) * g_ref[...]
    hb = (y * (1.0 + mod[1:2]) + mod[0:1]).astype(bf16)

    def proj(c0, c1):
        return _dot(hb, w_ref[:, c0:c1])

    def qknorm(a, gain_ref):
        sq = (a * a).astype(bf16)
        gw = grp_ref.shape[0]
        msq = jnp.concatenate([_dot(sq[:, c:c + gw], grp_ref[...]) for c in range(0, a.shape[1], gw)], axis=1)
        return (a * lax.rsqrt(msq + RMS_EPS) * gain_ref[...]).astype(bf16)

    mqk_ref[:, 0:512] = proj(C_MQK, C_MQK + 512).astype(bf16)
    mqk_ref[:, 512:1024] = proj(C_MQK + 512, C_MV).astype(bf16)
    mv_ref[...] = proj(C_MV, C_MO).astype(bf16)
    mo_ref[...] = proj(C_MO, C_AQ).astype(bf16)
    qn_ref[...] = qknorm(proj(C_AQ, C_AK), qg_ref)
    kn_ref[...] = qknorm(proj(C_AK, C_AV), kg_ref)
    av_ref[...] = proj(C_AV, C_GM).astype(bf16)
    gm_ref[:, 0:512] = proj(C_GM, C_GM + 512).astype(bf16)
    gm_ref[:, 512:1024] = proj(C_GM + 512, C_GA).astype(bf16)
    ga_ref[:, 0:512] = proj(C_GA, C_GA + 512).astype(bf16)
    ga_ref[:, 512:1024] = proj(C_GA + 512, C_GATE).astype(bf16)
    gate_ref[...] = proj(C_GATE, C_END)


def _inproj(x2, mod3, g1, w, grp, qg, kg, tiles_per_batch):
    n = x2.shape[0]
    tm = TM_PROJ
    tok = lambda width: pl.BlockSpec((tm, width), lambda i: (i, 0))
    outs = [(1024, bf16), (512, bf16), (512, bf16), (512, bf16), (512, bf16), (512, bf16),
            (1024, bf16), (1024, bf16), (2 * LANES, f32)]
    return pl.pallas_call(
        _inproj_kernel,
        grid=(n // tm,),
        in_specs=[tok(D_MODEL),
                  pl.BlockSpec((1, 6, D_MODEL), lambda i: (i // tiles_per_batch, 0, 0)),
                  _const_spec(g1.shape), _const_spec(w.shape), _const_spec(grp.shape),
                  _const_spec(qg.shape), _const_spec(kg.shape)],
        out_specs=[tok(wd) for wd, _ in outs],
        out_shape=[jax.ShapeDtypeStruct((n, wd), dt) for wd, dt in outs],
        compiler_params=pltpu.CompilerParams(dimension_semantics=("arbitrary",),
                                             vmem_limit_bytes=VMEM_LIMIT),
        name="inproj",
    )(x2, mod3, g1, w, grp, qg, kg)


def _mlstm_kernel(mqk_ref, mv_ref, mo_ref, gate_ref, cw_ref, cb_ref, bi_ref, bfg_ref, mg_ref,
                  o_ref, c_st, m_st, tail_ref):
    nb = mqk_ref.shape[0]
    L = M_CHUNK
    d = M_HEAD_DIM

    @pl.when(pl.program_id(0) == 0)
    def _():
        c_st[...] = jnp.zeros_like(c_st)
        m_st[...] = jnp.zeros_like(m_st)
        tail_ref[...] = jnp.zeros_like(tail_ref)

    row = lax.broadcasted_iota(jnp.int32, (L, L), 0)
    col = lax.broadcasted_iota(jnp.int32, (L, L), 1)
    causal = row >= col
    tri = jnp.where(causal, 1.0, 0.0).astype(bf16)
    ones = jnp.ones((L, d), bf16)
    cw = cw_ref[...]

    for b in range(nb):
        x = mqk_ref[b].astype(f32)
        prev8 = tail_ref[b]
        y = (cw[3:4] * x + cw[2:3] * _shift_rows(x, prev8, 1) + cw[1:2] * _shift_rows(x, prev8, 2)
             + cw[0:1] * _shift_rows(x, prev8, 3) + cb_ref[...])
        tail_ref[b] = x[L - SUBLANES:, :]
        qk = y * _sigmoid(y)

        gi = gate_ref[b, :, 0:LANES] + bi_ref[...]
        gf = gate_ref[b, :, LANES:2 * LANES] + bfg_ref[...]
        logf = jnp.minimum(gf, 0.0) - jnp.log(1.0 + jnp.exp(-jnp.abs(gf)))
        logf_hi = logf.astype(bf16)
        logf_lo = (logf - logf_hi.astype(f32)).astype(bf16)
        bcum = _dot(tri, logf_hi) + _dot(tri, logf_lo)
        r = gi - bcum
        r_t = r.T
        b_last = bcum[L - 1:L, :]
        g = b_last + r
        m_loc = jnp.max(g, axis=0, keepdims=True)

        for h in range(M_HEADS):
            st = b * M_HEADS + h
            hs = slice(h * d, (h + 1) * d)
            qh = qk[:, hs].astype(bf16)
            kh = qk[:, M_WIDTH + h * d:M_WIDTH + (h + 1) * d] * (d ** -0.5)
            vaug = jnp.concatenate([mv_ref[b, :, hs], ones], axis=1)
            m_in = m_st[st][0:1, 0:1]
            b_col = bcum[:, h:h + 1]
            dmat = jnp.where(causal, b_col + r_t[h:h + 1, :], NEG_BIG)
            a_t = b_col + m_in
            m_t = jnp.maximum(a_t, jnp.max(dmat, axis=1, keepdims=True))
            inter_w = jnp.exp(a_t - m_t)
            s = _dot_nt(qh, kh.astype(bf16)) * jnp.exp(dmat - m_t)
            tot = _dot(s.astype(bf16), vaug) + inter_w * _dot(qh, c_st[st].astype(bf16))
            den = jnp.maximum(jnp.abs(tot[:, d:]), jnp.exp(-m_t))
            hh = tot[:, :d] / den

            bl = b_last[:, h:h + 1]
            m_new = jnp.maximum(bl + m_in, m_loc[:, h:h + 1])
            w_col = jnp.exp(g[:, h:h + 1] - m_new)
            kw_t = (kh * w_col).T.astype(bf16)
            c_st[st] = jnp.exp(bl + m_in - m_new) * c_st[st] + _dot(kw_t, vaug)
            m_st[st] = jnp.broadcast_to(m_new, (SUBLANES, LANES))

            hn = hh * lax.rsqrt(jnp.mean(hh * hh, axis=-1, keepdims=True) + RMS_EPS) * mg_ref[:, hs]
            o_ref[b, :, hs] = (_sigmoid(mo_ref[b, :, hs].astype(f32)) * hn).astype(bf16)


def _mlstm(mqk, mv, mo, gates, cw, cb, bi, bfg, mg, batch, seq):
    L = M_CHUNK
    tok = lambda width: pl.BlockSpec((batch, L, width), lambda c: (0, c, 0))
    per_batch = lambda a: a.reshape(batch, seq, a.shape[-1])
    out = pl.pallas_call(
        _mlstm_kernel,
        grid=(seq // L,),
        in_specs=[tok(2 * M_WIDTH), tok(M_WIDTH), tok(M_WIDTH), tok(2 * LANES),
                  _const_spec(cw.shape), _const_spec(cb.shape), _const_spec(bi.shape),
                  _const_spec(bfg.shape), _const_spec(mg.shape)],
        out_specs=tok(M_WIDTH),
        out_shape=jax.ShapeDtypeStruct((batch, seq, M_WIDTH), bf16),
        scratch_shapes=[pltpu.VMEM((batch * M_HEADS, M_HEAD_DIM, 2 * M_HEAD_DIM), f32),
                        pltpu.VMEM((batch * M_HEADS, SUBLANES, LANES), f32),
                        pltpu.VMEM((batch, SUBLANES, 2 * M_WIDTH), f32)],
        compiler_params=pltpu.CompilerParams(dimension_semantics=("arbitrary",),
                                             vmem_limit_bytes=VMEM_LIMIT),
        name="mlstm",
    )(per_batch(mqk), per_batch(mv), per_batch(mo), per_batch(gates), cw, cb, bi, bfg, mg)
    return out.reshape(batch * seq, M_WIDTH)


def _attn_kernel(tab_ref, q_ref, k_ref, v_ref, bias_ref, lam_ref, ng_ref, qg_ref, o_ref,
                 vaug_ref, q2_ref, acc_ref, m_ref, mfix_ref, kmax_ref, safe_ref, *, lam_init):
    T = A_TILE
    dv = A_V_DIM
    hg = pl.program_id(1)
    qi = pl.program_id(2)
    heads = range(A_HPS)
    sel = jnp.where((lax.broadcasted_iota(jnp.int32, (LANES, 2 * LANES), 0) < A_QK_DIM)
                    == (lax.broadcasted_iota(jnp.int32, (LANES, 2 * LANES), 1) < LANES), 1.0, 0.0).astype(bf16)

    def sq_norms(x):
        xf = x.astype(f32)
        return _dot((xf * xf).astype(bf16), sel)

    def lanes_of(j):
        return slice(j * LANES, (j + 1) * LANES)

    bmax, bmin = [], []
    for j in heads:
        hi = jnp.float32(0.0)
        lo = jnp.float32(0.0)
        for b in range(N_BUCKETS - 1):
            rel = (tab_ref[hg * A_HPS + j, b] - tab_ref[hg * A_HPS + j, N_BUCKETS - 1]) * LOG2E
            hi = jnp.maximum(hi, rel)
            lo = jnp.minimum(lo, rel)
        bmax.append(hi)
        bmin.append(lo)

    @pl.when(qi == 0)
    def _():
        ok = None
        for j in heads:
            vaug_ref[j, :, 0:dv] = v_ref[:, lanes_of(j)]
            vaug_ref[j, :, dv:2 * dv] = jnp.ones((v_ref.shape[0], dv), bf16)
            kmax = jnp.zeros((1, 2 * LANES), f32)
            for c in range(k_ref.shape[0] // A_NORM_ROWS):
                kn2 = sq_norms(k_ref[c * A_NORM_ROWS:(c + 1) * A_NORM_ROWS, lanes_of(j)])
                kmax = jnp.maximum(kmax, jnp.max(kn2, axis=0, keepdims=True))
            kmax_ref[j] = kmax
            qmax = math.sqrt(A_QK_DIM) * jnp.max(jnp.abs(qg_ref[:, lanes_of(j)]))
            span = 2.0 * qmax * jnp.sqrt(jnp.max(kmax)) + (bmax[j] - bmin[j])
            ok_j = span < A_SAFE_RANGE
            ok = ok_j if ok is None else jnp.logical_and(ok, ok_j)
        safe_ref[0] = ok.astype(jnp.int32)

    for j in heads:
        q = q_ref[:, lanes_of(j)]
        lane = lax.broadcasted_iota(jnp.int32, q.shape, 1)
        zero = jnp.zeros_like(q)
        q2_ref[j, 0:T, :] = jnp.where(lane < A_QK_DIM, q, zero)
        q2_ref[j, T:2 * T, :] = jnp.where(lane >= A_QK_DIM, q, zero)
        bound = jnp.sqrt(sq_norms(q) * kmax_ref[j])
        mfix_ref[j, 0:T, :] = bound[:, 0:LANES] + bmax[j]
        mfix_ref[j, T:2 * T, :] = bound[:, LANES:2 * LANES] + bmax[j]
    acc_ref[...] = jnp.zeros_like(acc_ref)
    safe = safe_ref[0] == 1

    def step(start, nk, near, fixed_shift):
        start = pl.multiple_of(start, T)
        for rb in range(2 * T // A_ROWS):
            rows = slice(rb * A_ROWS, (rb + 1) * A_ROWS)
            brow = (rb * A_ROWS) % T
            nkb = nk - (T - brow - A_ROWS) if near else nk
            for j in heads:
                kt = k_ref[pl.ds(start, nkb), lanes_of(j)]
                va = vaug_ref[j, pl.ds(start, nkb), :]
                s = _dot_nt(q2_ref[j, rows, :], kt)
                if near:
                    s = s + bias_ref[j, brow:brow + A_ROWS, 2 * T - nk:2 * T - nk + nkb]
                if fixed_shift:
                    mrow = mfix_ref[j, rows, :]
                    p = jnp.exp2(s - jnp.concatenate([mrow] * (nkb // LANES), axis=1)).astype(bf16)
                    acc_ref[j, rows, :] += _dot(p, va)
                else:
                    m_old = m_ref[j, rows, :]
                    m_new = jnp.maximum(m_old, jnp.max(s, axis=1, keepdims=True))
                    p = jnp.exp2(s - m_new).astype(bf16)
                    acc_ref[j, rows, :] = jnp.exp2(m_old - m_new) * acc_ref[j, rows, :] + _dot(p, va)
                    m_ref[j, rows, :] = m_new

    def sweep(fixed_shift):
        nfar = jnp.maximum(qi - 1, 0)
        npair = nfar // 2

        def far_body(i, carry):
            step(i * (2 * T), 2 * T, False, fixed_shift)
            return carry

        lax.fori_loop(0, npair, far_body, 0)

        @pl.when(nfar % 2 == 1)
        def _():
            step(npair * (2 * T), T, False, fixed_shift)

        @pl.when(qi >= 1)
        def _():
            step((qi - 1) * T, 2 * T, True, fixed_shift)

        @pl.when(qi == 0)
        def _():
            step(0, T, True, fixed_shift)

    @pl.when(safe)
    def _():
        sweep(True)

    @pl.when(jnp.logical_not(safe))
    def _():
        m_ref[...] = jnp.full_like(m_ref, NEG_BIG)
        sweep(False)

    al = lam_ref[...]
    lam = (jnp.exp(jnp.sum(al[0:1] * al[1:2], keepdims=True))
           - jnp.exp(jnp.sum(al[2:3] * al[3:4], keepdims=True)) + lam_init)
    for j in heads:
        acc = acc_ref[j]
        o = acc[:, 0:dv] / acc[:, dv:2 * dv]
        ha = o[0:T] - lam * o[T:2 * T]
        hn = ha * lax.rsqrt(jnp.mean(ha * ha, axis=-1, keepdims=True) + RMS_EPS) * ng_ref[:, lanes_of(j)]
        o_ref[:, lanes_of(j)] = (hn * (1.0 - lam_init)).astype(bf16)


def _diffattn(tab, qn, kn, av, bias, lam_par, ng, qg, batch, seq, lam_init):
    T = A_TILE
    nq = seq // T
    wide = A_HPS * LANES
    return pl.pallas_call(
        functools.partial(_attn_kernel, lam_init=lam_init),
        grid=(batch, A_HEADS // A_HPS, nq),
        in_specs=[pl.BlockSpec(memory_space=pltpu.SMEM),
                  pl.BlockSpec((T, wide), lambda b, h, i: (b * nq + i, h)),
                  pl.BlockSpec((seq, wide), lambda b, h, i: (b, h)),
                  pl.BlockSpec((seq, wide), lambda b, h, i: (b, h)),
                  pl.BlockSpec((A_HPS, T, 2 * T), lambda b, h, i: (h, 0, 0)),
                  _const_spec(lam_par.shape),
                  pl.BlockSpec((1, wide), lambda b, h, i: (0, h)),
                  pl.BlockSpec((1, wide), lambda b, h, i: (0, h))],
        out_specs=pl.BlockSpec((T, wide), lambda b, h, i: (b * nq + i, h)),
        out_shape=jax.ShapeDtypeStruct((batch * seq, A_WIDTH), bf16),
        scratch_shapes=[pltpu.VMEM((A_HPS, seq, 2 * A_V_DIM), bf16),
                        pltpu.VMEM((A_HPS, 2 * T, LANES), bf16),
                        pltpu.VMEM((A_HPS, 2 * T, 2 * A_V_DIM), f32),
                        pltpu.VMEM((A_HPS, 2 * T, 1), f32),
                        pltpu.VMEM((A_HPS, 2 * T, LANES), f32),
                        pltpu.VMEM((A_HPS, 1, 2 * LANES), f32),
                        pltpu.SMEM((1,), jnp.int32)],
        compiler_params=pltpu.CompilerParams(
            dimension_semantics=("arbitrary", "arbitrary", "arbitrary"),
            vmem_limit_bytes=VMEM_LIMIT),
        name="diffattn",
    )(tab, qn, kn, av, bias, lam_par, ng, qg)


def _merge_kernel(x_ref, mod_ref, hm_ref, ha_ref, gm_ref, ga_ref, wm_ref, wa_ref, wo_ref, o_ref):
    ym = _dot(hm_ref[...], wm_ref[...])
    ya = _dot(ha_ref[...], wa_ref[...])
    y = _sigmoid(gm_ref[...].astype(f32)) * ym + _sigmoid(ga_ref[...].astype(f32)) * ya
    o_ref[...] = x_ref[...] + mod_ref[0][2:3] * _dot(y.astype(bf16), wo_ref[...])


def _merge(x2, mod3, hm, ha, gm, ga, wm, wa, wo, tiles_per_batch):
    n = x2.shape[0]
    tm = TM_PROJ
    tok = lambda width: pl.BlockSpec((tm, width), lambda i: (i, 0))
    return pl.pallas_call(
        _merge_kernel,
        grid=(n // tm,),
        in_specs=[tok(D_MODEL),
                  pl.BlockSpec((1, 6, D_MODEL), lambda i: (i // tiles_per_batch, 0, 0)),
                  tok(M_WIDTH), tok(A_WIDTH), tok(D_MODEL), tok(D_MODEL),
                  _const_spec(wm.shape), _const_spec(wa.shape), _const_spec(wo.shape)],
        out_specs=tok(D_MODEL),
        out_shape=jax.ShapeDtypeStruct((n, D_MODEL), f32),
        compiler_params=pltpu.CompilerParams(dimension_semantics=("arbitrary",),
                                             vmem_limit_bytes=VMEM_LIMIT),
        name="merge",
    )(x2, mod3, hm, ha, gm, ga, wm, wa, wo)


def _ffn_kernel(x_ref, mod_ref, g_ref, wup_ref, cw_ref, cb_ref, wd_ref,
                o_ref, hv_ref, hg_ref, ubuf_ref, act_ref, *, tiles_per_batch):
    tm = x_ref.shape[0]

    @pl.when(pl.program_id(0) % tiles_per_batch == 0)
    def _():
        hv_ref[...] = jnp.zeros_like(hv_ref)
        hg_ref[...] = jnp.zeros_like(hg_ref)

    x = x_ref[...]
    mod = mod_ref[0]
    ms = jnp.mean(x * x, axis=-1, keepdims=True)
    y = x * lax.rsqrt(ms + RMS_EPS) * g_ref[...]
    hb = (y * (1.0 + mod[4:5]) + mod[3:4]).astype(bf16)

    def conv(u, halo, buf, w, b):
        buf[0:SUBLANES, :] = halo
        buf[SUBLANES:SUBLANES + tm, :] = u
        return (w[2:3] * u + w[1:2] * buf[SUBLANES - 1:SUBLANES - 1 + tm, :]
                + w[0:1] * buf[SUBLANES - 2:SUBLANES - 2 + tm, :] + b)

    for j in range(N_FCHUNK):
        cols = slice(j * F_CHUNK, (j + 1) * F_CHUNK)
        gcols = slice(D_FF + j * F_CHUNK, D_FF + (j + 1) * F_CHUNK)
        uv = _dot(hb, wup_ref[:, cols])
        ug = _dot(hb, wup_ref[:, gcols])
        cv = conv(uv, hv_ref[j], ubuf_ref.at[(2 * j) % N_UBUF], cw_ref[:, cols], cb_ref[:, cols])
        cg = conv(ug, hg_ref[j], ubuf_ref.at[(2 * j + 1) % N_UBUF], cw_ref[:, gcols], cb_ref[:, gcols])
        hv_ref[j] = uv[tm - SUBLANES:, :]
        hg_ref[j] = ug[tm - SUBLANES:, :]
        act_ref[:, cols] = ((cg * _sigmoid(cg)) * cv).astype(bf16)
    o_ref[...] = x + mod[5:6] * _dot(act_ref[...], wd_ref[...])


def _ffn(x2, mod3, g2, wup, cw, cb, wd, tiles_per_batch):
    n = x2.shape[0]
    tm = TM_PROJ
    tok = lambda width: pl.BlockSpec((tm, width), lambda i: (i, 0))
    return pl.pallas_call(
        functools.partial(_ffn_kernel, tiles_per_batch=tiles_per_batch),
        grid=(n // tm,),
        in_specs=[tok(D_MODEL),
                  pl.BlockSpec((1, 6, D_MODEL), lambda i: (i // tiles_per_batch, 0, 0)),
                  _const_spec(g2.shape), _const_spec(wup.shape), _const_spec(cw.shape),
                  _const_spec(cb.shape), _const_spec(wd.shape)],
        out_specs=tok(D_MODEL),
        out_shape=jax.ShapeDtypeStruct((n, D_MODEL), f32),
        scratch_shapes=[pltpu.VMEM((N_FCHUNK, SUBLANES, F_CHUNK), f32),
                        pltpu.VMEM((N_FCHUNK, SUBLANES, F_CHUNK), f32),
                        pltpu.VMEM((N_UBUF, tm + SUBLANES, F_CHUNK), f32),
                        pltpu.VMEM((tm, D_FF), bf16)],
        compiler_params=pltpu.CompilerParams(dimension_semantics=("arbitrary",),
                                             vmem_limit_bytes=VMEM_LIMIT),
        name="ffn",
    )(x2, mod3, g2, wup, cw, cb, wd)


def _pad_cols(a, width):
    return jnp.pad(a, ((0, 0), (0, width - a.shape[1])))


def _layer(x2, c8, batch, seq, layer, w_ada, b_ada, norm1_g, w_in, m_conv_w, m_conv_b, m_igate_b,
           m_fgate_b, m_norm_g, a_qnorm_g, a_knorm_g, a_lambda, a_norm_g, tab, bias, w_branch_m,
           w_branch_a, w_out, norm2_g, w_up, ffn_conv_w, ffn_conv_b, w_down):
    tiles_per_batch = seq // TM_PROJ
    mod3 = _adaln(c8, w_ada, b_ada.reshape(1, -1))[:batch].reshape(batch, 6, D_MODEL)

    o = 0
    parts = {}
    for name, size in (("mqk", 2 * M_WIDTH), ("mv", M_WIDTH), ("mo", M_WIDTH), ("mi", M_HEADS),
                       ("mf", M_HEADS), ("aq", A_WIDTH), ("ak", A_WIDTH), ("av", A_WIDTH),
                       ("gm", D_MODEL), ("ga", D_MODEL)):
        parts[name] = w_in[:, o:o + size]
        o += size

    def per_head(w):
        return w.reshape(D_MODEL, 2, A_HEADS, A_QK_DIM).transpose(0, 2, 1, 3).reshape(D_MODEL, A_WIDTH)

    w_cat = jnp.concatenate(
        [parts["mqk"], parts["mv"], parts["mo"], per_head(parts["aq"]), per_head(parts["ak"]),
         parts["av"], parts["gm"], parts["ga"], _pad_cols(parts["mi"], LANES),
         _pad_cols(parts["mf"], LANES)], axis=1).astype(bf16)
    gid = jnp.arange(2 * LANES) // A_QK_DIM
    grp = jnp.where(gid[:, None] == gid[None, :], 1.0 / A_QK_DIM, 0.0).astype(bf16)
    qg = (jnp.tile(a_qnorm_g, A_WIDTH // A_QK_DIM) * (A_QK_DIM ** -0.5 * LOG2E)).reshape(1, A_WIDTH)
    kg = jnp.tile(a_knorm_g, A_WIDTH // A_QK_DIM).reshape(1, A_WIDTH)

    mqk, mv, mo, qn, kn, av, gm, ga, gates = _inproj(
        x2, mod3, norm1_g.reshape(1, -1), w_cat, grp, qg, kg, tiles_per_batch)

    hm = _mlstm(mqk, mv, mo, gates, m_conv_w, m_conv_b.reshape(1, -1),
                _pad_cols(m_igate_b.reshape(1, -1), LANES), _pad_cols(m_fgate_b.reshape(1, -1), LANES),
                m_norm_g.reshape(1, -1), batch, seq)

    lam_init = 0.8 - 0.6 * math.exp(-0.3 * layer)
    ha = _diffattn(tab, qn, kn, av, bias, a_lambda, a_norm_g.reshape(1, -1), qg, batch, seq, lam_init)

    x1 = _merge(x2, mod3, hm, ha, gm, ga, w_branch_m.astype(bf16), w_branch_a.astype(bf16),
                w_out.astype(bf16), tiles_per_batch)

    return _ffn(x1, mod3, norm2_g.reshape(1, -1), w_up.astype(bf16), ffn_conv_w, ffn_conv_b.reshape(1, -1),
                w_down.astype(bf16), tiles_per_batch)


def kernel(x, c, w_ada, b_ada, norm1_g, w_in, m_conv_w, m_conv_b, m_igate_b, m_fgate_b, m_norm_g,
           a_qnorm_g, a_knorm_g, a_lambda, a_norm_g, rel_bias, w_branch_m, w_branch_a, w_out, norm2_g,
           w_up, ffn_conv_w, ffn_conv_b, w_down):
    batch, seq, _ = x.shape
    depth = w_ada.shape[0]
    x2 = x.reshape(batch * seq, D_MODEL)
    c8 = jnp.pad(c, ((0, SUBLANES - batch), (0, 0)))
    tab = rel_bias.astype(f32).T
    bias = _bias_tiles(tab)
    for l in range(depth):
        x2 = _layer(x2, c8, batch, seq, l, w_ada[l], b_ada[l], norm1_g[l], w_in[l], m_conv_w[l],
                    m_conv_b[l], m_igate_b[l], m_fgate_b[l], m_norm_g[l], a_qnorm_g[l], a_knorm_g[l],
                    a_lambda[l], a_norm_g[l], tab, bias, w_branch_m[l], w_branch_a[l], w_out[l], norm2_g[l],
                    w_up[l], ffn_conv_w[l], ffn_conv_b[l], w_down[l])
    return x2.reshape(batch, seq, D_MODEL)
```

```python
import functools
import math

import jax
import jax.numpy as jnp
from jax import lax
from jax.experimental import pallas as pl
from jax.experimental.pallas import tpu as pltpu

D_MODEL = 1024
M_HEADS = 4
M_HEAD_DIM = 128
M_WIDTH = M_HEADS * M_HEAD_DIM
M_CONV = 4
A_HEADS = 4
A_QK_DIM = 64
A_V_DIM = 2 * A_QK_DIM
A_WIDTH = A_HEADS * A_V_DIM
N_BUCKETS = 32
MAX_DISTANCE = 128
D_FF = 2816
FFN_CONV = 3
RMS_EPS = 1e-6
LOG2E = math.log2(math.e)
NEG_BIG = -1e30

LANES = 128
SUBLANES = 8
VMEM_LIMIT = 56 * 1024 * 1024

TM_PROJ = 512
M_CHUNK = 256
A_TILE = 512
A_ROWS = 256
A_HPS = 4
A_NORM_ROWS = 1024
A_SAFE_RANGE = 90.0
F_CHUNK = 256
N_FCHUNK = D_FF // F_CHUNK
N_UBUF = 4

bf16 = jnp.bfloat16
f32 = jnp.float32


def _dot(a, b):
    return jnp.dot(a, b, preferred_element_type=f32)


def _dot_nt(a, b):
    return lax.dot_general(a, b, (((1,), (1,)), ((), ())), preferred_element_type=f32)


def _sigmoid(x):
    return 1.0 / (1.0 + jnp.exp(-x))


def _const_spec(shape):
    nd = len(shape)
    return pl.BlockSpec(shape, lambda *_: (0,) * nd)


def _shift_rows(u, prev8, k):
    r = pltpu.roll(u, k, 0)
    rp = pltpu.roll(prev8, k, 0)
    row = lax.broadcasted_iota(jnp.int32, (SUBLANES, u.shape[1]), 0)
    first = jnp.where(row < k, rp, r[:SUBLANES])
    return jnp.concatenate([first, r[SUBLANES:]], axis=0)


def _adaln_kernel(c_ref, w_ref, b_ref, o_ref):
    c = c_ref[...]
    a = (c * _sigmoid(c)).astype(bf16)
    o_ref[...] = _dot(a, w_ref[...].astype(bf16)) + b_ref[...]


def _adaln(c8, w, b):
    n = w.shape[1]
    tn = 1536
    return pl.pallas_call(
        _adaln_kernel,
        grid=(n // tn,),
        in_specs=[_const_spec(c8.shape),
                  pl.BlockSpec((D_MODEL, tn), lambda j: (0, j)),
                  pl.BlockSpec((1, tn), lambda j: (0, j))],
        out_specs=pl.BlockSpec((c8.shape[0], tn), lambda j: (0, j)),
        out_shape=jax.ShapeDtypeStruct((c8.shape[0], n), f32),
        compiler_params=pltpu.CompilerParams(dimension_semantics=("arbitrary",),
                                             vmem_limit_bytes=VMEM_LIMIT),
        name="adaln",
    )(c8, w, b)


def _bias_kernel(tab_ref, o_ref):
    h = pl.program_id(0)
    t = pl.program_id(1)
    row = lax.broadcasted_iota(jnp.int32, (A_TILE, A_TILE), 0)
    col = lax.broadcasted_iota(jnp.int32, (A_TILE, A_TILE), 1)
    dist = row - col + (1 - t) * A_TILE
    n = jnp.maximum(dist, 0)
    max_exact = N_BUCKETS // 2
    nf = jnp.maximum(n, 1).astype(f32)
    large = max_exact + (jnp.log(nf / max_exact) / math.log(MAX_DISTANCE / max_exact)
                         * (N_BUCKETS - max_exact)).astype(jnp.int32)
    large = jnp.minimum(large, N_BUCKETS - 1)
    bucket = jnp.where(n < max_exact, n, large)
    far = tab_ref[h, N_BUCKETS - 1]
    val = jnp.zeros((A_TILE, A_TILE), f32)
    for b in range(N_BUCKETS - 1):
        val = jnp.where(bucket == b, tab_ref[h, b] - far, val)
    o_ref[0] = jnp.where(dist >= 0, val * LOG2E, NEG_BIG)


def _bias_tiles(tab):
    return pl.pallas_call(
        _bias_kernel,
        grid=(A_HEADS, 2),
        in_specs=[pl.BlockSpec(memory_space=pltpu.SMEM)],
        out_specs=pl.BlockSpec((1, A_TILE, A_TILE), lambda h, t: (h, 0, t)),
        out_shape=jax.ShapeDtypeStruct((A_HEADS, A_TILE, 2 * A_TILE), f32),
        compiler_params=pltpu.CompilerParams(dimension_semantics=("arbitrary", "arbitrary"),
                                             vmem_limit_bytes=VMEM_LIMIT),
        name="bias_tiles",
    )(tab)


C_MQK, C_MV, C_MO, C_AQ, C_AK, C_AV, C_GM, C_GA, C_GATE, C_END = (
    0, 1024, 1536, 2048, 2560, 3072, 3584, 4608, 5632, 5888)


def _inproj_kernel(x_ref, mod_ref, g_ref, w_ref, grp_ref, qg_ref, kg_ref,
                   mqk_ref, mv_ref, mo_ref, qn_ref, kn_ref, av_ref, gm_ref, ga_ref, gate_ref):
    x = x_ref[...]
    mod = mod_ref[0]
    ms = jnp.mean(x * x, axis=-1, keepdims=True)
    y = x * lax.rsqrt(ms + RMS_EPS) * g_ref[...]
    hb = (y * (1.0 + mod[1:2]) + mod[0:1]).astype(bf16)

    def proj(c0, c1):
        return _dot(hb, w_ref[:, c0:c1])

    def qknorm(a, gain_ref):
        sq = (a * a).astype(bf16)
        gw = grp_ref.shape[0]
        msq = jnp.concatenate([_dot(sq[:, c:c + gw], grp_ref[...]) for c in range(0, a.shape[1], gw)], axis=1)
        return (a * lax.rsqrt(msq + RMS_EPS) * gain_ref[...]).astype(bf16)

    mqk_ref[:, 0:512] = proj(C_MQK, C_MQK + 512).astype(bf16)
    mqk_ref[:, 512:1024] = proj(C_MQK + 512, C_MV).astype(bf16)
    mv_ref[...] = proj(C_MV, C_MO).astype(bf16)
    mo_ref[...] = proj(C_MO, C_AQ).astype(bf16)
    qn_ref[...] = qknorm(proj(C_AQ, C_AK), qg_ref)
    kn_ref[...] = qknorm(proj(C_AK, C_AV), kg_ref)
    av_ref[...] = proj(C_AV, C_GM).astype(bf16)
    gm_ref[:, 0:512] = proj(C_GM, C_GM + 512).astype(bf16)
    gm_ref[:, 512:1024] = proj(C_GM + 512, C_GA).astype(bf16)
    ga_ref[:, 0:512] = proj(C_GA, C_GA + 512).astype(bf16)
    ga_ref[:, 512:1024] = proj(C_GA + 512, C_GATE).astype(bf16)
    gate_ref[...] = proj(C_GATE, C_END)


def _inproj(x2, mod3, g1, w, grp, qg, kg, tiles_per_batch):
    n = x2.shape[0]
    tm = TM_PROJ
    tok = lambda width: pl.BlockSpec((tm, width), lambda i: (i, 0))
    outs = [(1024, bf16), (512, bf16), (512, bf16), (512, bf16), (512, bf16), (512, bf16),
            (1024, bf16), (1024, bf16), (2 * LANES, f32)]
    return pl.pallas_call(
        _inproj_kernel,
        grid=(n // tm,),
        in_specs=[tok(D_MODEL),
                  pl.BlockSpec((1, 6, D_MODEL), lambda i: (i // tiles_per_batch, 0, 0)),
                  _const_spec(g1.shape), _const_spec(w.shape), _const_spec(grp.shape),
                  _const_spec(qg.shape), _const_spec(kg.shape)],
        out_specs=[tok(wd) for wd, _ in outs],
        out_shape=[jax.ShapeDtypeStruct((n, wd), dt) for wd, dt in outs],
        compiler_params=pltpu.CompilerParams(dimension_semantics=("arbitrary",),
                                             vmem_limit_bytes=VMEM_LIMIT),
        name="inproj",
    )(x2, mod3, g1, w, grp, qg, kg)


def _mlstm_kernel(mqk_ref, mv_ref, mo_ref, gate_ref, cw_ref, cb_ref, bi_ref, bfg_ref, mg_ref,
                  o_ref, c_st, m_st, tail_ref):
    nb = mqk_ref.shape[0]
    L = M_CHUNK
    d = M_HEAD_DIM

    @pl.when(pl.program_id(0) == 0)
    def _():
        c_st[...] = jnp.zeros_like(c_st)
        m_st[...] = jnp.zeros_like(m_st)
        tail_ref[...] = jnp.zeros_like(tail_ref)

    row = lax.broadcasted_iota(jnp.int32, (L, L), 0)
    col = lax.broadcasted_iota(jnp.int32, (L, L), 1)
    causal = row >= col
    tri = jnp.where(causal, 1.0, 0.0).astype(bf16)
    ones = jnp.ones((L, d), bf16)
    cw = cw_ref[...]

    for b in range(nb):
        x = mqk_ref[b].astype(f32)
        prev8 = tail_ref[b]
        y = (cw[3:4] * x + cw[2:3] * _shift_rows(x, prev8, 1) + cw[1:2] * _shift_rows(x, prev8, 2)
             + cw[0:1] * _shift_rows(x, prev8, 3) + cb_ref[...])
        tail_ref[b] = x[L - SUBLANES:, :]
        qk = y * _sigmoid(y)

        gi = gate_ref[b, :, 0:LANES] + bi_ref[...]
        gf = gate_ref[b, :, LANES:2 * LANES] + bfg_ref[...]
        logf = jnp.minimum(gf, 0.0) - jnp.log(1.0 + jnp.exp(-jnp.abs(gf)))
        logf_hi = logf.astype(bf16)
        logf_lo = (logf - logf_hi.astype(f32)).astype(bf16)
        bcum = _dot(tri, logf_hi) + _dot(tri, logf_lo)
        r = gi - bcum
        r_t = r.T
        b_last = bcum[L - 1:L, :]
        g = b_last + r
        m_loc = jnp.max(g, axis=0, keepdims=True)

        for h in range(M_HEADS):
            st = b * M_HEADS + h
            hs = slice(h * d, (h + 1) * d)
            qh = qk[:, hs].astype(bf16)
            kh = qk[:, M_WIDTH + h * d:M_WIDTH + (h + 1) * d] * (d ** -0.5)
            vaug = jnp.concatenate([mv_ref[b, :, hs], ones], axis=1)
            m_in = m_st[st][0:1, 0:1]
            b_col = bcum[:, h:h + 1]
            dmat = jnp.where(causal, b_col + r_t[h:h + 1, :], NEG_BIG)
            a_t = b_col + m_in
            m_t = jnp.maximum(a_t, jnp.max(dmat, axis=1, keepdims=True))
            inter_w = jnp.exp(a_t - m_t)
            s = _dot_nt(qh, kh.astype(bf16)) * jnp.exp(dmat - m_t)
            tot = _dot(s.astype(bf16), vaug) + inter_w * _dot(qh, c_st[st].astype(bf16))
            den = jnp.maximum(jnp.abs(tot[:, d:]), jnp.exp(-m_t))
            hh = tot[:, :d] / den

            bl = b_last[:, h:h + 1]
            m_new = jnp.maximum(bl + m_in, m_loc[:, h:h + 1])
            w_col = jnp.exp(g[:, h:h + 1] - m_new)
            kw_t = (kh * w_col).T.astype(bf16)
            c_st[st] = jnp.exp(bl + m_in - m_new) * c_st[st] + _dot(kw_t, vaug)
            m_st[st] = jnp.broadcast_to(m_new, (SUBLANES, LANES))

            hn = hh * lax.rsqrt(jnp.mean(hh * hh, axis=-1, keepdims=True) + RMS_EPS) * mg_ref[:, hs]
            o_ref[b, :, hs] = (_sigmoid(mo_ref[b, :, hs].astype(f32)) * hn).astype(bf16)


def _mlstm(mqk, mv, mo, gates, cw, cb, bi, bfg, mg, batch, seq):
    L = M_CHUNK
    tok = lambda width: pl.BlockSpec((batch, L, width), lambda c: (0, c, 0))
    per_batch = lambda a: a.reshape(batch, seq, a.shape[-1])
    out = pl.pallas_call(
        _mlstm_kernel,
        grid=(seq // L,),
        in_specs=[tok(2 * M_WIDTH), tok(M_WIDTH), tok(M_WIDTH), tok(2 * LANES),
                  _const_spec(cw.shape), _const_spec(cb.shape), _const_spec(bi.shape),
                  _const_spec(bfg.shape), _const_spec(mg.shape)],
        out_specs=tok(M_WIDTH),
        out_shape=jax.ShapeDtypeStruct((batch, seq, M_WIDTH), bf16),
        scratch_shapes=[pltpu.VMEM((batch * M_HEADS, M_HEAD_DIM, 2 * M_HEAD_DIM), f32),
                        pltpu.VMEM((batch * M_HEADS, SUBLANES, LANES), f32),
                        pltpu.VMEM((batch, SUBLANES, 2 * M_WIDTH), f32)],
        compiler_params=pltpu.CompilerParams(dimension_semantics=("arbitrary",),
                                             vmem_limit_bytes=VMEM_LIMIT),
        name="mlstm",
    )(per_batch(mqk), per_batch(mv), per_batch(mo), per_batch(gates), cw, cb, bi, bfg, mg)
    return out.reshape(batch * seq, M_WIDTH)


def _attn_kernel(tab_ref, q_ref, k_ref, v_ref, bias_ref, lam_ref, ng_ref, qg_ref, o_ref,
                 q2_ref, acc_ref, l_ref, m_ref, mfix_ref, kmax_ref, safe_ref, *, lam_init):
    T = A_TILE
    dv = A_V_DIM
    hg = pl.program_id(1)
    qi = pl.program_id(2)
    heads = range(A_HPS)
    sel = jnp.where((lax.broadcasted_iota(jnp.int32, (LANES, 2 * LANES), 0) < A_QK_DIM)
                    == (lax.broadcasted_iota(jnp.int32, (LANES, 2 * LANES), 1) < LANES), 1.0, 0.0).astype(bf16)

    def sq_norms(x):
        xf = x.astype(f32)
        return _dot((xf * xf).astype(bf16), sel)

    def lanes_of(j):
        return slice(j * LANES, (j + 1) * LANES)

    bmax, bmin = [], []
    for j in heads:
        hi = jnp.float32(0.0)
        lo = jnp.float32(0.0)
        for b in range(N_BUCKETS - 1):
            rel = (tab_ref[hg * A_HPS + j, b] - tab_ref[hg * A_HPS + j, N_BUCKETS - 1]) * LOG2E
            hi = jnp.maximum(hi, rel)
            lo = jnp.minimum(lo, rel)
        bmax.append(hi)
        bmin.append(lo)

    @pl.when(qi == 0)
    def _():
        ok = None
        for j in heads:
            kmax = jnp.zeros((1, 2 * LANES), f32)
            for c in range(k_ref.shape[0] // A_NORM_ROWS):
                kn2 = sq_norms(k_ref[c * A_NORM_ROWS:(c + 1) * A_NORM_ROWS, lanes_of(j)])
                kmax = jnp.maximum(kmax, jnp.max(kn2, axis=0, keepdims=True))
            kmax_ref[j] = kmax
            qmax = math.sqrt(A_QK_DIM) * jnp.max(jnp.abs(qg_ref[:, lanes_of(j)]))
            span = 2.0 * qmax * jnp.sqrt(jnp.max(kmax)) + (bmax[j] - bmin[j])
            ok_j = span < A_SAFE_RANGE
            ok = ok_j if ok is None else jnp.logical_and(ok, ok_j)
        safe_ref[0] = ok.astype(jnp.int32)

    for j in heads:
        q = q_ref[:, lanes_of(j)]
        lane = lax.broadcasted_iota(jnp.int32, q.shape, 1)
        zero = jnp.zeros_like(q)
        q2_ref[j, 0:T, :] = jnp.where(lane < A_QK_DIM, q, zero)
        q2_ref[j, T:2 * T, :] = jnp.where(lane >= A_QK_DIM, q, zero)
        bound = jnp.sqrt(sq_norms(q) * kmax_ref[j])
        mfix_ref[j, 0:T, :] = bound[:, 0:LANES] + bmax[j]
        mfix_ref[j, T:2 * T, :] = bound[:, LANES:2 * LANES] + bmax[j]
    acc_ref[...] = jnp.zeros_like(acc_ref)
    l_ref[...] = jnp.zeros_like(l_ref)
    safe = safe_ref[0] == 1

    def lane_tile_sum(e):
        parts = [e[:, c:c + LANES] for c in range(0, e.shape[1], LANES)]
        while len(parts) > 1:
            parts = [a + b for a, b in zip(parts[0::2], parts[1::2])] + ([parts[-1]] if len(parts) % 2 else [])
        return parts[0]

    def step(start, nk, near, fixed_shift):
        start = pl.multiple_of(start, T)
        for rb in range(2 * T // A_ROWS):
            rows = slice(rb * A_ROWS, (rb + 1) * A_ROWS)
            brow = (rb * A_ROWS) % T
            nkb = nk - (T - brow - A_ROWS) if near else nk
            for j in heads:
                kt = k_ref[pl.ds(start, nkb), lanes_of(j)]
                vt = v_ref[pl.ds(start, nkb), lanes_of(j)]
                s = _dot_nt(q2_ref[j, rows, :], kt)
                if near:
                    s = s + bias_ref[j, brow:brow + A_ROWS, 2 * T - nk:2 * T - nk + nkb]
                if fixed_shift:
                    mrow = mfix_ref[j, rows, :]
                    e = jnp.exp2(s - jnp.concatenate([mrow] * (nkb // LANES), axis=1))
                    l_ref[j, rows, :] += lane_tile_sum(e)
                    acc_ref[j, rows, :] += _dot(e.astype(bf16), vt)
                else:
                    m_old = m_ref[j, rows, :]
                    m_new = jnp.maximum(m_old, jnp.max(s, axis=1, keepdims=True))
                    alpha = jnp.exp2(m_old - m_new)
                    e = jnp.exp2(s - m_new)
                    l_ref[j, rows, :] = alpha * l_ref[j, rows, :] + lane_tile_sum(e)
                    acc_ref[j, rows, :] = alpha * acc_ref[j, rows, :] + _dot(e.astype(bf16), vt)
                    m_ref[j, rows, :] = m_new

    def sweep(fixed_shift):
        nfar = jnp.maximum(qi - 1, 0)
        npair = nfar // 2

        def far_body(i, carry):
            step(i * (2 * T), 2 * T, False, fixed_shift)
            return carry

        lax.fori_loop(0, npair, far_body, 0)

        @pl.when(nfar % 2 == 1)
        def _():
            step(npair * (2 * T), T, False, fixed_shift)

        @pl.when(qi >= 1)
        def _():
            step((qi - 1) * T, 2 * T, True, fixed_shift)

        @pl.when(qi == 0)
        def _():
            step(0, T, True, fixed_shift)

    @pl.when(safe)
    def _():
        sweep(True)

    @pl.when(jnp.logical_not(safe))
    def _():
        m_ref[...] = jnp.full_like(m_ref, NEG_BIG)
        sweep(False)

    al = lam_ref[...]
    lam = (jnp.exp(jnp.sum(al[0:1] * al[1:2], keepdims=True))
           - jnp.exp(jnp.sum(al[2:3] * al[3:4], keepdims=True)) + lam_init)
    for j in heads:
        o = acc_ref[j] / jnp.sum(l_ref[j], axis=1, keepdims=True)
        ha = o[0:T] - lam * o[T:2 * T]
        hn = ha * lax.rsqrt(jnp.mean(ha * ha, axis=-1, keepdims=True) + RMS_EPS) * ng_ref[:, lanes_of(j)]
        o_ref[:, lanes_of(j)] = (hn * (1.0 - lam_init)).astype(bf16)


def _diffattn(tab, qn, kn, av, bias, lam_par, ng, qg, batch, seq, lam_init):
    T = A_TILE
    nq = seq // T
    wide = A_HPS * LANES
    return pl.pallas_call(
        functools.partial(_attn_kernel, lam_init=lam_init),
        grid=(batch, A_HEADS // A_HPS, nq),
        in_specs=[pl.BlockSpec(memory_space=pltpu.SMEM),
                  pl.BlockSpec((T, wide), lambda b, h, i: (b * nq + i, h)),
                  pl.BlockSpec((seq, wide), lambda b, h, i: (b, h), pipeline_mode=pl.Buffered(1)),
                  pl.BlockSpec((seq, wide), lambda b, h, i: (b, h), pipeline_mode=pl.Buffered(1)),
                  pl.BlockSpec((A_HPS, T, 2 * T), lambda b, h, i: (h, 0, 0), pipeline_mode=pl.Buffered(1)),
                  _const_spec(lam_par.shape),
                  pl.BlockSpec((1, wide), lambda b, h, i: (0, h)),
                  pl.BlockSpec((1, wide), lambda b, h, i: (0, h))],
        out_specs=pl.BlockSpec((T, wide), lambda b, h, i: (b * nq + i, h)),
        out_shape=jax.ShapeDtypeStruct((batch * seq, A_WIDTH), bf16),
        scratch_shapes=[pltpu.VMEM((A_HPS, 2 * T, LANES), bf16),
                        pltpu.VMEM((A_HPS, 2 * T, A_V_DIM), f32),
                        pltpu.VMEM((A_HPS, 2 * T, LANES), f32),
                        pltpu.VMEM((A_HPS, 2 * T, 1), f32),
                        pltpu.VMEM((A_HPS, 2 * T, LANES), f32),
                        pltpu.VMEM((A_HPS, 1, 2 * LANES), f32),
                        pltpu.SMEM((1,), jnp.int32)],
        compiler_params=pltpu.CompilerParams(
            dimension_semantics=("arbitrary", "arbitrary", "arbitrary"),
            vmem_limit_bytes=VMEM_LIMIT),
        name="diffattn",
    )(tab, qn, kn, av, bias, lam_par, ng, qg)


def _merge_kernel(x_ref, mod_ref, hm_ref, ha_ref, gm_ref, ga_ref, wm_ref, wa_ref, wo_ref, o_ref):
    ym = _dot(hm_ref[...], wm_ref[...])
    ya = _dot(ha_ref[...], wa_ref[...])
    y = _sigmoid(gm_ref[...].astype(f32)) * ym + _sigmoid(ga_ref[...].astype(f32)) * ya
    o_ref[...] = x_ref[...] + mod_ref[0][2:3] * _dot(y.astype(bf16), wo_ref[...])


def _merge(x2, mod3, hm, ha, gm, ga, wm, wa, wo, tiles_per_batch):
    n = x2.shape[0]
    tm = TM_PROJ
    tok = lambda width: pl.BlockSpec((tm, width), lambda i: (i, 0))
    return pl.pallas_call(
        _merge_kernel,
        grid=(n // tm,),
        in_specs=[tok(D_MODEL),
                  pl.BlockSpec((1, 6, D_MODEL), lambda i: (i // tiles_per_batch, 0, 0)),
                  tok(M_WIDTH), tok(A_WIDTH), tok(D_MODEL), tok(D_MODEL),
                  _const_spec(wm.shape), _const_spec(wa.shape), _const_spec(wo.shape)],
        out_specs=tok(D_MODEL),
        out_shape=jax.ShapeDtypeStruct((n, D_MODEL), f32),
        compiler_params=pltpu.CompilerParams(dimension_semantics=("arbitrary",),
                                             vmem_limit_bytes=VMEM_LIMIT),
        name="merge",
    )(x2, mod3, hm, ha, gm, ga, wm, wa, wo)


def _ffn_kernel(x_ref, mod_ref, g_ref, wup_ref, cw_ref, cb_ref, wd_ref,
                o_ref, hv_ref, hg_ref, ubuf_ref, act_ref, *, tiles_per_batch):
    tm = x_ref.shape[0]

    @pl.when(pl.program_id(0) % tiles_per_batch == 0)
    def _():
        hv_ref[...] = jnp.zeros_like(hv_ref)
        hg_ref[...] = jnp.zeros_like(hg_ref)

    x = x_ref[...]
    mod = mod_ref[0]
    ms = jnp.mean(x * x, axis=-1, keepdims=True)
    y = x * lax.rsqrt(ms + RMS_EPS) * g_ref[...]
    hb = (y * (1.0 + mod[4:5]) + mod[3:4]).astype(bf16)

    def conv(u, halo, buf, w, b):
        buf[0:SUBLANES, :] = halo
        buf[SUBLANES:SUBLANES + tm, :] = u
        return (w[2:3] * u + w[1:2] * buf[SUBLANES - 1:SUBLANES - 1 + tm, :]
                + w[0:1] * buf[SUBLANES - 2:SUBLANES - 2 + tm, :] + b)

    for j in range(N_FCHUNK):
        cols = slice(j * F_CHUNK, (j + 1) * F_CHUNK)
        gcols = slice(D_FF + j * F_CHUNK, D_FF + (j + 1) * F_CHUNK)
        uv = _dot(hb, wup_ref[:, cols])
        ug = _dot(hb, wup_ref[:, gcols])
        cv = conv(uv, hv_ref[j], ubuf_ref.at[(2 * j) % N_UBUF], cw_ref[:, cols], cb_ref[:, cols])
        cg = conv(ug, hg_ref[j], ubuf_ref.at[(2 * j + 1) % N_UBUF], cw_ref[:, gcols], cb_ref[:, gcols])
        hv_ref[j] = uv[tm - SUBLANES:, :]
        hg_ref[j] = ug[tm - SUBLANES:, :]
        act_ref[:, cols] = ((cg * _sigmoid(cg)) * cv).astype(bf16)
    o_ref[...] = x + mod[5:6] * _dot(act_ref[...], wd_ref[...])


def _ffn(x2, mod3, g2, wup, cw, cb, wd, tiles_per_batch):
    n = x2.shape[0]
    tm = TM_PROJ
    tok = lambda width: pl.BlockSpec((tm, width), lambda i: (i, 0))
    return pl.pallas_call(
        functools.partial(_ffn_kernel, tiles_per_batch=tiles_per_batch),
        grid=(n // tm,),
        in_specs=[tok(D_MODEL),
                  pl.BlockSpec((1, 6, D_MODEL), lambda i: (i // tiles_per_batch, 0, 0)),
                  _const_spec(g2.shape), _const_spec(wup.shape), _const_spec(cw.shape),
                  _const_spec(cb.shape), _const_spec(wd.shape)],
        out_specs=tok(D_MODEL),
        out_shape=jax.ShapeDtypeStruct((n, D_MODEL), f32),
        scratch_shapes=[pltpu.VMEM((N_FCHUNK, SUBLANES, F_CHUNK), f32),
                        pltpu.VMEM((N_FCHUNK, SUBLANES, F_CHUNK), f32),
                        pltpu.VMEM((N_UBUF, tm + SUBLANES, F_CHUNK), f32),
                        pltpu.VMEM((tm, D_FF), bf16)],
        compiler_params=pltpu.CompilerParams(dimension_semantics=("arbitrary",),
                                             vmem_limit_bytes=VMEM_LIMIT),
        name="ffn",
    )(x2, mod3, g2, wup, cw, cb, wd)


def _pad_cols(a, width):
    return jnp.pad(a, ((0, 0), (0, width - a.shape[1])))


def _layer(x2, c8, batch, seq, layer, w_ada, b_ada, norm1_g, w_in, m_conv_w, m_conv_b, m_igate_b,
           m_fgate_b, m_norm_g, a_qnorm_g, a_knorm_g, a_lambda, a_norm_g, tab, bias, w_branch_m,
           w_branch_a, w_out, norm2_g, w_up, ffn_conv_w, ffn_conv_b, w_down):
    tiles_per_batch = seq // TM_PROJ
    mod3 = _adaln(c8, w_ada, b_ada.reshape(1, -1))[:batch].reshape(batch, 6, D_MODEL)

    o = 0
    parts = {}
    for name, size in (("mqk", 2 * M_WIDTH), ("mv", M_WIDTH), ("mo", M_WIDTH), ("mi", M_HEADS),
                       ("mf", M_HEADS), ("aq", A_WIDTH), ("ak", A_WIDTH), ("av", A_WIDTH),
                       ("gm", D_MODEL), ("ga", D_MODEL)):
        parts[name] = w_in[:, o:o + size]
        o += size

    def per_head(w):
        return w.reshape(D_MODEL, 2, A_HEADS, A_QK_DIM).transpose(0, 2, 1, 3).reshape(D_MODEL, A_WIDTH)

    w_cat = jnp.concatenate(
        [parts["mqk"], parts["mv"], parts["mo"], per_head(parts["aq"]), per_head(parts["ak"]),
         parts["av"], parts["gm"], parts["ga"], _pad_cols(parts["mi"], LANES),
         _pad_cols(parts["mf"], LANES)], axis=1).astype(bf16)
    gid = jnp.arange(2 * LANES) // A_QK_DIM
    grp = jnp.where(gid[:, None] == gid[None, :], 1.0 / A_QK_DIM, 0.0).astype(bf16)
    qg = (jnp.tile(a_qnorm_g, A_WIDTH // A_QK_DIM) * (A_QK_DIM ** -0.5 * LOG2E)).reshape(1, A_WIDTH)
    kg = jnp.tile(a_knorm_g, A_WIDTH // A_QK_DIM).reshape(1, A_WIDTH)

    mqk, mv, mo, qn, kn, av, gm, ga, gates = _inproj(
        x2, mod3, norm1_g.reshape(1, -1), w_cat, grp, qg, kg, tiles_per_batch)

    hm = _mlstm(mqk, mv, mo, gates, m_conv_w, m_conv_b.reshape(1, -1),
                _pad_cols(m_igate_b.reshape(1, -1), LANES), _pad_cols(m_fgate_b.reshape(1, -1), LANES),
                m_norm_g.reshape(1, -1), batch, seq)

    lam_init = 0.8 - 0.6 * math.exp(-0.3 * layer)
    ha = _diffattn(tab, qn, kn, av, bias, a_lambda, a_norm_g.reshape(1, -1), qg, batch, seq, lam_init)

    x1 = _merge(x2, mod3, hm, ha, gm, ga, w_branch_m.astype(bf16), w_branch_a.astype(bf16),
                w_out.astype(bf16), tiles_per_batch)

    return _ffn(x1, mod3, norm2_g.reshape(1, -1), w_up.astype(bf16), ffn_conv_w, ffn_conv_b.reshape(1, -1),
                w_down.astype(bf16), tiles_per_batch)


def kernel(x, c, w_ada, b_ada, norm1_g, w_in, m_conv_w, m_conv_b, m_igate_b, m_fgate_b, m_norm_g,
           a_qnorm_g, a_knorm_g, a_lambda, a_norm_g, rel_bias, w_branch_m, w_branch_a, w_out, norm2_g,
           w_up, ffn_conv_w, ffn_conv_b, w_down):
    batch, seq, _ = x.shape
    depth = w_ada.shape[0]
    x2 = x.reshape(batch * seq, D_MODEL)
    c8 = jnp.pad(c, ((0, SUBLANES - batch), (0, 0)))
    tab = rel_bias.astype(f32).T
    bias = _bias_tiles(tab)
    for l in range(depth):
        x2 = _layer(x2, c8, batch, seq, l, w_ada[l], b_ada[l], norm1_g[l], w_in[l], m_conv_w[l],
                    m_conv_b[l], m_igate_b[l], m_fgate_b[l], m_norm_g[l], a_qnorm_g[l], a_knorm_g[l],
                    a_lambda[l], a_norm_g[l], tab, bias, w_branch_m[l], w_branch_a[l], w_out[l], norm2_g[l],
                    w_up[l], ffn_conv_w[l], ffn_conv_b[l], w_down[l])
    return x2.reshape(batch, seq, D_MODEL)
```

```python
import functools
import math

import jax
import jax.numpy as jnp
from jax import lax
from jax.experimental import pallas as pl
from jax.experimental.pallas import tpu as pltpu

D_MODEL = 1024
M_HEADS = 4
M_HEAD_DIM = 128
M_WIDTH = M_HEADS * M_HEAD_DIM
M_CONV = 4
A_HEADS = 4
A_QK_DIM = 64
A_V_DIM = 2 * A_QK_DIM
A_WIDTH = A_HEADS * A_V_DIM
N_BUCKETS = 32
MAX_DISTANCE = 128
D_FF = 2816
FFN_CONV = 3
RMS_EPS = 1e-6
LOG2E = math.log2(math.e)
NEG_BIG = -1e30

LANES = 128
SUBLANES = 8
VMEM_LIMIT = 56 * 1024 * 1024

TM_PROJ = 512
M_CHUNK = 256
A_TILE = 512
A_ROWS = 256
A_HPS = 2
A_NORM_ROWS = 1024
A_SAFE_RANGE = 90.0
F_CHUNK = 256
N_FCHUNK = D_FF // F_CHUNK

bf16 = jnp.bfloat16
f32 = jnp.float32


def _dot(a, b):
    return jnp.dot(a, b, preferred_element_type=f32)


def _dot_nt(a, b):
    return lax.dot_general(a, b, (((1,), (1,)), ((), ())), preferred_element_type=f32)


def _sigmoid(x):
    return 1.0 / (1.0 + jnp.exp(-x))


def _const_spec(shape):
    nd = len(shape)
    return pl.BlockSpec(shape, lambda *_: (0,) * nd)


def _shift_rows(u, prev8, k):
    r = pltpu.roll(u, k, 0)
    rp = pltpu.roll(prev8, k, 0)
    row = lax.broadcasted_iota(jnp.int32, (SUBLANES, u.shape[1]), 0)
    first = jnp.where(row < k, rp, r[:SUBLANES])
    return jnp.concatenate([first, r[SUBLANES:]], axis=0)


def _adaln_kernel(c_ref, w_ref, b_ref, o_ref):
    c = c_ref[...]
    a = (c * _sigmoid(c)).astype(bf16)
    o_ref[...] = _dot(a, w_ref[...].astype(bf16)) + b_ref[...]


def _adaln(c8, w, b):
    n = w.shape[1]
    tn = 1536
    return pl.pallas_call(
        _adaln_kernel,
        grid=(n // tn,),
        in_specs=[_const_spec(c8.shape),
                  pl.BlockSpec((D_MODEL, tn), lambda j: (0, j)),
                  pl.BlockSpec((1, tn), lambda j: (0, j))],
        out_specs=pl.BlockSpec((c8.shape[0], tn), lambda j: (0, j)),
        out_shape=jax.ShapeDtypeStruct((c8.shape[0], n), f32),
        compiler_params=pltpu.CompilerParams(dimension_semantics=("arbitrary",),
                                             vmem_limit_bytes=VMEM_LIMIT),
        name="adaln",
    )(c8, w, b)


def _bias_kernel(tab_ref, o_ref):
    h = pl.program_id(0)
    t = pl.program_id(1)
    row = lax.broadcasted_iota(jnp.int32, (A_TILE, A_TILE), 0)
    col = lax.broadcasted_iota(jnp.int32, (A_TILE, A_TILE), 1)
    dist = row - col + (1 - t) * A_TILE
    n = jnp.maximum(dist, 0)
    max_exact = N_BUCKETS // 2
    nf = jnp.maximum(n, 1).astype(f32)
    large = max_exact + (jnp.log(nf / max_exact) / math.log(MAX_DISTANCE / max_exact)
                         * (N_BUCKETS - max_exact)).astype(jnp.int32)
    large = jnp.minimum(large, N_BUCKETS - 1)
    bucket = jnp.where(n < max_exact, n, large)
    far = tab_ref[h, N_BUCKETS - 1]
    val = jnp.zeros((A_TILE, A_TILE), f32)
    for b in range(N_BUCKETS - 1):
        val = jnp.where(bucket == b, tab_ref[h, b] - far, val)
    o_ref[0] = jnp.where(dist >= 0, val * LOG2E, NEG_BIG)


def _bias_tiles(tab):
    return pl.pallas_call(
        _bias_kernel,
        grid=(A_HEADS, 2),
        in_specs=[pl.BlockSpec(memory_space=pltpu.SMEM)],
        out_specs=pl.BlockSpec((1, A_TILE, A_TILE), lambda h, t: (h, 0, t)),
        out_shape=jax.ShapeDtypeStruct((A_HEADS, A_TILE, 2 * A_TILE), f32),
        compiler_params=pltpu.CompilerParams(dimension_semantics=("arbitrary", "arbitrary"),
                                             vmem_limit_bytes=VMEM_LIMIT),
        name="bias_tiles",
    )(tab)


C_MQK, C_MV, C_MO, C_AQ, C_AK, C_AV, C_GM, C_GA, C_GATE, C_END = (
    0, 1024, 1536, 2048, 2560, 3072, 3584, 4608, 5632, 5888)


def _inproj_kernel(x_ref, mod_ref, g_ref, w_ref, grp_ref, qg_ref, kg_ref,
                   mqk_ref, mv_ref, mo_ref, qn_ref, kn_ref, av_ref, gm_ref, ga_ref, gate_ref):
    x = x_ref[...]
    mod = mod_ref[0]
    ms = jnp.mean(x * x, axis=-1, keepdims=True)
    y = x * lax.rsqrt(ms + RMS_EPS) * g_ref[...]
    hb = (y * (1.0 + mod[1:2]) + mod[0:1]).astype(bf16)

    def proj(c0, c1):
        return _dot(hb, w_ref[:, c0:c1])

    def qknorm(a, gain_ref):
        sq = (a * a).astype(bf16)
        gw = grp_ref.shape[0]
        msq = jnp.concatenate([_dot(sq[:, c:c + gw], grp_ref[...]) for c in range(0, a.shape[1], gw)], axis=1)
        return (a * lax.rsqrt(msq + RMS_EPS) * gain_ref[...]).astype(bf16)

    mqk_ref[:, 0:512] = proj(C_MQK, C_MQK + 512).astype(bf16)
    mqk_ref[:, 512:1024] = proj(C_MQK + 512, C_MV).astype(bf16)
    mv_ref[...] = proj(C_MV, C_MO).astype(bf16)
    mo_ref[...] = proj(C_MO, C_AQ).astype(bf16)
    qn_ref[...] = qknorm(proj(C_AQ, C_AK), qg_ref)
    kn_ref[...] = qknorm(proj(C_AK, C_AV), kg_ref)
    av_ref[...] = proj(C_AV, C_GM).astype(bf16)
    gm_ref[:, 0:512] = proj(C_GM, C_GM + 512).astype(bf16)
    gm_ref[:, 512:1024] = proj(C_GM + 512, C_GA).astype(bf16)
    ga_ref[:, 0:512] = proj(C_GA, C_GA + 512).astype(bf16)
    ga_ref[:, 512:1024] = proj(C_GA + 512, C_GATE).astype(bf16)
    gate_ref[...] = proj(C_GATE, C_END)


def _inproj(x2, mod3, g1, w, grp, qg, kg, tiles_per_batch):
    n = x2.shape[0]
    tm = TM_PROJ
    tok = lambda width: pl.BlockSpec((tm, width), lambda i: (i, 0))
    outs = [(1024, bf16), (512, bf16), (512, bf16), (512, bf16), (512, bf16), (512, bf16),
            (1024, bf16), (1024, bf16), (2 * LANES, f32)]
    return pl.pallas_call(
        _inproj_kernel,
        grid=(n // tm,),
        in_specs=[tok(D_MODEL),
                  pl.BlockSpec((1, 6, D_MODEL), lambda i: (i // tiles_per_batch, 0, 0)),
                  _const_spec(g1.shape), _const_spec(w.shape), _const_spec(grp.shape),
                  _const_spec(qg.shape), _const_spec(kg.shape)],
        out_specs=[tok(wd) for wd, _ in outs],
        out_shape=[jax.ShapeDtypeStruct((n, wd), dt) for wd, dt in outs],
        compiler_params=pltpu.CompilerParams(dimension_semantics=("arbitrary",),
                                             vmem_limit_bytes=VMEM_LIMIT),
        name="inproj",
    )(x2, mod3, g1, w, grp, qg, kg)


def _mlstm_kernel(mqk_ref, mv_ref, mo_ref, gate_ref, cw_ref, cb_ref, bi_ref, bfg_ref, mg_ref,
                  o_ref, c_st, m_st, tail_ref):
    nb = mqk_ref.shape[0]
    L = M_CHUNK
    d = M_HEAD_DIM

    @pl.when(pl.program_id(0) == 0)
    def _():
        c_st[...] = jnp.zeros_like(c_st)
        m_st[...] = jnp.zeros_like(m_st)
        tail_ref[...] = jnp.zeros_like(tail_ref)

    row = lax.broadcasted_iota(jnp.int32, (L, L), 0)
    col = lax.broadcasted_iota(jnp.int32, (L, L), 1)
    causal = row >= col
    tri = jnp.where(causal, 1.0, 0.0).astype(bf16)
    ones = jnp.ones((L, d), bf16)
    cw = cw_ref[...]

    for b in range(nb):
        x = mqk_ref[b].astype(f32)
        prev8 = tail_ref[b]
        y = (cw[3:4] * x + cw[2:3] * _shift_rows(x, prev8, 1) + cw[1:2] * _shift_rows(x, prev8, 2)
             + cw[0:1] * _shift_rows(x, prev8, 3) + cb_ref[...])
        tail_ref[b] = x[L - SUBLANES:, :]
        qk = y * _sigmoid(y)

        gi = gate_ref[b, :, 0:LANES] + bi_ref[...]
        gf = gate_ref[b, :, LANES:2 * LANES] + bfg_ref[...]
        logf = jnp.minimum(gf, 0.0) - jnp.log(1.0 + jnp.exp(-jnp.abs(gf)))
        logf_hi = logf.astype(bf16)
        logf_lo = (logf - logf_hi.astype(f32)).astype(bf16)
        bcum = _dot(tri, logf_hi) + _dot(tri, logf_lo)
        r = gi - bcum
        r_t = r.T
        b_last = bcum[L - 1:L, :]
        g = b_last + r
        m_loc = jnp.max(g, axis=0, keepdims=True)

        for h in range(M_HEADS):
            st = b * M_HEADS + h
            hs = slice(h * d, (h + 1) * d)
            qh = qk[:, hs].astype(bf16)
            kh = qk[:, M_WIDTH + h * d:M_WIDTH + (h + 1) * d] * (d ** -0.5)
            vaug = jnp.concatenate([mv_ref[b, :, hs], ones], axis=1)
            m_in = m_st[st][0:1, 0:1]
            b_col = bcum[:, h:h + 1]
            dmat = jnp.where(causal, b_col + r_t[h:h + 1, :], NEG_BIG)
            a_t = b_col + m_in
            m_t = jnp.maximum(a_t, jnp.max(dmat, axis=1, keepdims=True))
            inter_w = jnp.exp(a_t - m_t)
            s = _dot_nt(qh, kh.astype(bf16)) * jnp.exp(dmat - m_t)
            tot = _dot(s.astype(bf16), vaug) + inter_w * _dot(qh, c_st[st].astype(bf16))
            den = jnp.maximum(jnp.abs(tot[:, d:]), jnp.exp(-m_t))
            hh = tot[:, :d] / den

            bl = b_last[:, h:h + 1]
            m_new = jnp.maximum(bl + m_in, m_loc[:, h:h + 1])
            w_col = jnp.exp(g[:, h:h + 1] - m_new)
            kw_t = (kh * w_col).T.astype(bf16)
            c_st[st] = jnp.exp(bl + m_in - m_new) * c_st[st] + _dot(kw_t, vaug)
            m_st[st] = jnp.broadcast_to(m_new, (SUBLANES, LANES))

            hn = hh * lax.rsqrt(jnp.mean(hh * hh, axis=-1, keepdims=True) + RMS_EPS) * mg_ref[:, hs]
            o_ref[b, :, hs] = (_sigmoid(mo_ref[b, :, hs].astype(f32)) * hn).astype(bf16)


def _mlstm(mqk, mv, mo, gates, cw, cb, bi, bfg, mg, batch, seq):
    L = M_CHUNK
    tok = lambda width: pl.BlockSpec((batch, L, width), lambda c: (0, c, 0))
    per_batch = lambda a: a.reshape(batch, seq, a.shape[-1])
    out = pl.pallas_call(
        _mlstm_kernel,
        grid=(seq // L,),
        in_specs=[tok(2 * M_WIDTH), tok(M_WIDTH), tok(M_WIDTH), tok(2 * LANES),
                  _const_spec(cw.shape), _const_spec(cb.shape), _const_spec(bi.shape),
                  _const_spec(bfg.shape), _const_spec(mg.shape)],
        out_specs=tok(M_WIDTH),
        out_shape=jax.ShapeDtypeStruct((batch, seq, M_WIDTH), bf16),
        scratch_shapes=[pltpu.VMEM((batch * M_HEADS, M_HEAD_DIM, 2 * M_HEAD_DIM), f32),
                        pltpu.VMEM((batch * M_HEADS, SUBLANES, LANES), f32),
                        pltpu.VMEM((batch, SUBLANES, 2 * M_WIDTH), f32)],
        compiler_params=pltpu.CompilerParams(dimension_semantics=("arbitrary",),
                                             vmem_limit_bytes=VMEM_LIMIT),
        name="mlstm",
    )(per_batch(mqk), per_batch(mv), per_batch(mo), per_batch(gates), cw, cb, bi, bfg, mg)
    return out.reshape(batch * seq, M_WIDTH)


def _attn_kernel(tab_ref, q_ref, k_ref, v_ref, bias_ref, lam_ref, ng_ref, qg_ref, o_ref,
                 vaug_ref, q2_ref, acc_ref, m_ref, mfix_ref, kmax_ref, safe_ref, *, lam_init):
    T = A_TILE
    dv = A_V_DIM
    hg = pl.program_id(1)
    qi = pl.program_id(2)
    heads = range(A_HPS)
    sel = jnp.where((lax.broadcasted_iota(jnp.int32, (LANES, 2 * LANES), 0) < A_QK_DIM)
                    == (lax.broadcasted_iota(jnp.int32, (LANES, 2 * LANES), 1) < LANES), 1.0, 0.0).astype(bf16)

    def sq_norms(x):
        xf = x.astype(f32)
        return _dot((xf * xf).astype(bf16), sel)

    def lanes_of(j):
        return slice(j * LANES, (j + 1) * LANES)

    bmax, bmin = [], []
    for j in heads:
        hi = jnp.float32(0.0)
        lo = jnp.float32(0.0)
        for b in range(N_BUCKETS - 1):
            rel = (tab_ref[hg * A_HPS + j, b] - tab_ref[hg * A_HPS + j, N_BUCKETS - 1]) * LOG2E
            hi = jnp.maximum(hi, rel)
            lo = jnp.minimum(lo, rel)
        bmax.append(hi)
        bmin.append(lo)

    @pl.when(qi == 0)
    def _():
        ok = None
        for j in heads:
            vaug_ref[j, :, 0:dv] = v_ref[:, lanes_of(j)]
            vaug_ref[j, :, dv:2 * dv] = jnp.ones((v_ref.shape[0], dv), bf16)
            kmax = jnp.zeros((1, 2 * LANES), f32)
            for c in range(k_ref.shape[0] // A_NORM_ROWS):
                kn2 = sq_norms(k_ref[c * A_NORM_ROWS:(c + 1) * A_NORM_ROWS, lanes_of(j)])
                kmax = jnp.maximum(kmax, jnp.max(kn2, axis=0, keepdims=True))
            kmax_ref[j] = kmax
            qmax = math.sqrt(A_QK_DIM) * jnp.max(jnp.abs(qg_ref[:, lanes_of(j)]))
            span = 2.0 * qmax * jnp.sqrt(jnp.max(kmax)) + (bmax[j] - bmin[j])
            ok_j = span < A_SAFE_RANGE
            ok = ok_j if ok is None else jnp.logical_and(ok, ok_j)
        safe_ref[0] = ok.astype(jnp.int32)

    for j in heads:
        q = q_ref[:, lanes_of(j)]
        lane = lax.broadcasted_iota(jnp.int32, q.shape, 1)
        zero = jnp.zeros_like(q)
        q2_ref[j, 0:T, :] = jnp.where(lane < A_QK_DIM, q, zero)
        q2_ref[j, T:2 * T, :] = jnp.where(lane >= A_QK_DIM, q, zero)
        bound = jnp.sqrt(sq_norms(q) * kmax_ref[j])
        mfix_ref[j, 0:T, :] = bound[:, 0:LANES] + bmax[j]
        mfix_ref[j, T:2 * T, :] = bound[:, LANES:2 * LANES] + bmax[j]
    acc_ref[...] = jnp.zeros_like(acc_ref)
    safe = safe_ref[0] == 1

    def step(start, nk, near, fixed_shift):
        start = pl.multiple_of(start, T)
        for rb in range(2 * T // A_ROWS):
            rows = slice(rb * A_ROWS, (rb + 1) * A_ROWS)
            brow = (rb * A_ROWS) % T
            nkb = nk - (T - brow - A_ROWS) if near else nk
            for j in heads:
                kt = k_ref[pl.ds(start, nkb), lanes_of(j)]
                va = vaug_ref[j, pl.ds(start, nkb), :]
                s = _dot_nt(q2_ref[j, rows, :], kt)
                if near:
                    s = s + bias_ref[j, brow:brow + A_ROWS, 2 * T - nk:2 * T - nk + nkb]
                if fixed_shift:
                    mrow = mfix_ref[j, rows, :]
                    p = jnp.exp2(s - jnp.concatenate([mrow] * (nkb // LANES), axis=1)).astype(bf16)
                    acc_ref[j, rows, :] += _dot(p, va)
                else:
                    m_old = m_ref[j, rows, :]
                    m_new = jnp.maximum(m_old, jnp.max(s, axis=1, keepdims=True))
                    p = jnp.exp2(s - m_new).astype(bf16)
                    acc_ref[j, rows, :] = jnp.exp2(m_old - m_new) * acc_ref[j, rows, :] + _dot(p, va)
                    m_ref[j, rows, :] = m_new

    def sweep(fixed_shift):
        nfar = jnp.maximum(qi - 1, 0)
        npair = nfar // 2

        def far_body(i, carry):
            step(i * (2 * T), 2 * T, False, fixed_shift)
            return carry

        lax.fori_loop(0, npair, far_body, 0)

        @pl.when(nfar % 2 == 1)
        def _():
            step(npair * (2 * T), T, False, fixed_shift)

        @pl.when(qi >= 1)
        def _():
            step((qi - 1) * T, 2 * T, True, fixed_shift)

        @pl.when(qi == 0)
        def _():
            step(0, T, True, fixed_shift)

    @pl.when(safe)
    def _():
        sweep(True)

    @pl.when(jnp.logical_not(safe))
    def _():
        m_ref[...] = jnp.full_like(m_ref, NEG_BIG)
        sweep(False)

    al = lam_ref[...]
    lam = (jnp.exp(jnp.sum(al[0:1] * al[1:2], keepdims=True))
           - jnp.exp(jnp.sum(al[2:3] * al[3:4], keepdims=True)) + lam_init)
    for j in heads:
        acc = acc_ref[j]
        o = acc[:, 0:dv] / acc[:, dv:2 * dv]
        ha = o[0:T] - lam * o[T:2 * T]
        hn = ha * lax.rsqrt(jnp.mean(ha * ha, axis=-1, keepdims=True) + RMS_EPS) * ng_ref[:, lanes_of(j)]
        o_ref[:, lanes_of(j)] = (hn * (1.0 - lam_init)).astype(bf16)


def _diffattn(tab, qn, kn, av, bias, lam_par, ng, qg, batch, seq, lam_init):
    T = A_TILE
    nq = seq // T
    wide = A_HPS * LANES
    return pl.pallas_call(
        functools.partial(_attn_kernel, lam_init=lam_init),
        grid=(batch, A_HEADS // A_HPS, nq),
        in_specs=[pl.BlockSpec(memory_space=pltpu.SMEM),
                  pl.BlockSpec((T, wide), lambda b, h, i: (b * nq + i, h)),
                  pl.BlockSpec((seq, wide), lambda b, h, i: (b, h)),
                  pl.BlockSpec((seq, wide), lambda b, h, i: (b, h)),
                  pl.BlockSpec((A_HPS, T, 2 * T), lambda b, h, i: (h, 0, 0)),
                  _const_spec(lam_par.shape),
                  pl.BlockSpec((1, wide), lambda b, h, i: (0, h)),
                  pl.BlockSpec((1, wide), lambda b, h, i: (0, h))],
        out_specs=pl.BlockSpec((T, wide), lambda b, h, i: (b * nq + i, h)),
        out_shape=jax.ShapeDtypeStruct((batch * seq, A_WIDTH), bf16),
        scratch_shapes=[pltpu.VMEM((A_HPS, seq, 2 * A_V_DIM), bf16),
                        pltpu.VMEM((A_HPS, 2 * T, LANES), bf16),
                        pltpu.VMEM((A_HPS, 2 * T, 2 * A_V_DIM), f32),
                        pltpu.VMEM((A_HPS, 2 * T, 1), f32),
                        pltpu.VMEM((A_HPS, 2 * T, LANES), f32),
                        pltpu.VMEM((A_HPS, 1, 2 * LANES), f32),
                        pltpu.SMEM((1,), jnp.int32)],
        compiler_params=pltpu.CompilerParams(
            dimension_semantics=("arbitrary", "arbitrary", "arbitrary"),
            vmem_limit_bytes=VMEM_LIMIT),
        name="diffattn",
    )(tab, qn, kn, av, bias, lam_par, ng, qg)


def _merge_kernel(x_ref, mod_ref, hm_ref, ha_ref, gm_ref, ga_ref, wm_ref, wa_ref, wo_ref, o_ref):
    ym = _dot(hm_ref[...], wm_ref[...])
    ya = _dot(ha_ref[...], wa_ref[...])
    y = _sigmoid(gm_ref[...].astype(f32)) * ym + _sigmoid(ga_ref[...].astype(f32)) * ya
    o_ref[...] = x_ref[...] + mod_ref[0][2:3] * _dot(y.astype(bf16), wo_ref[...])


def _merge(x2, mod3, hm, ha, gm, ga, wm, wa, wo, tiles_per_batch):
    n = x2.shape[0]
    tm = TM_PROJ
    tok = lambda width: pl.BlockSpec((tm, width), lambda i: (i, 0))
    return pl.pallas_call(
        _merge_kernel,
        grid=(n // tm,),
        in_specs=[tok(D_MODEL),
                  pl.BlockSpec((1, 6, D_MODEL), lambda i: (i // tiles_per_batch, 0, 0)),
                  tok(M_WIDTH), tok(A_WIDTH), tok(D_MODEL), tok(D_MODEL),
                  _const_spec(wm.shape), _const_spec(wa.shape), _const_spec(wo.shape)],
        out_specs=tok(D_MODEL),
        out_shape=jax.ShapeDtypeStruct((n, D_MODEL), f32),
        compiler_params=pltpu.CompilerParams(dimension_semantics=("arbitrary",),
                                             vmem_limit_bytes=VMEM_LIMIT),
        name="merge",
    )(x2, mod3, hm, ha, gm, ga, wm, wa, wo)


def _ffn_kernel(x_ref, mod_ref, g_ref, wup_ref, cw_ref, cb_ref, wd_ref,
                o_ref, hv_ref, hg_ref, act_ref, stage_ref, *, tiles_per_batch):
    tm = x_ref.shape[0]
    ng = tm // SUBLANES
    halo = FFN_CONV - 1

    @pl.when(pl.program_id(0) % tiles_per_batch == 0)
    def _():
        hv_ref[...] = jnp.zeros_like(hv_ref)
        hg_ref[...] = jnp.zeros_like(hg_ref)

    nl = x_ref.shape[1] // LANES
    pitch = ng + SUBLANES
    for c in range(nl):
        for sgm in range(SUBLANES):
            stage_ref[c, sgm * pitch:sgm * pitch + ng, :] = x_ref[sgm * ng:(sgm + 1) * ng, c * LANES:(c + 1) * LANES]
    x = jnp.concatenate(
        [jnp.concatenate([stage_ref[c, pl.ds(v, SUBLANES, stride=pitch), :] for c in range(nl)], axis=1)
         for v in range(ng)], axis=0)
    mod = mod_ref[0]
    ms = jnp.mean(x * x, axis=-1, keepdims=True)
    y = x * lax.rsqrt(ms + RMS_EPS) * g_ref[...]
    hb = (y * (1.0 + mod[4:5]) + mod[3:4]).astype(bf16)
    first = lax.broadcasted_iota(jnp.int32, (SUBLANES, F_CHUNK), 0) == 0

    def shifted(u, prev, k):
        def wrapped(g):
            cur = u[g * SUBLANES:(g + 1) * SUBLANES, :]
            old = prev[(g - (ng - halo)) * SUBLANES:(g - (ng - halo) + 1) * SUBLANES, :]
            return jnp.where(first, pltpu.roll(old, 1, 0), pltpu.roll(cur, 1, 0))
        return jnp.concatenate([wrapped(ng - k + v) for v in range(k)] + [u[:tm - k * SUBLANES, :]], axis=0)

    def conv(u, prev, w, b):
        return w[2:3] * u + w[1:2] * shifted(u, prev, 1) + w[0:1] * shifted(u, prev, 2) + b

    for j in range(N_FCHUNK):
        cols = slice(j * F_CHUNK, (j + 1) * F_CHUNK)
        gcols = slice(D_FF + j * F_CHUNK, D_FF + (j + 1) * F_CHUNK)
        uv = _dot(hb, wup_ref[:, cols])
        ug = _dot(hb, wup_ref[:, gcols])
        cv = conv(uv, hv_ref[j], cw_ref[:, cols], cb_ref[:, cols])
        cg = conv(ug, hg_ref[j], cw_ref[:, gcols], cb_ref[:, gcols])
        hv_ref[j] = uv[tm - halo * SUBLANES:, :]
        hg_ref[j] = ug[tm - halo * SUBLANES:, :]
        act_ref[:, cols] = ((cg * _sigmoid(cg)) * cv).astype(bf16)
    out = x + mod[5:6] * _dot(act_ref[...], wd_ref[...])
    for c in range(nl):
        for v in range(ng):
            stage_ref[c, pl.ds(v, SUBLANES, stride=pitch), :] = out[v * SUBLANES:(v + 1) * SUBLANES,
                                                                    c * LANES:(c + 1) * LANES]
        for sgm in range(SUBLANES):
            o_ref[sgm * ng:(sgm + 1) * ng, c * LANES:(c + 1) * LANES] = stage_ref[c, sgm * pitch:sgm * pitch + ng, :]


def _ffn(x2, mod3, g2, wup, cw, cb, wd, tiles_per_batch):
    n = x2.shape[0]
    tm = TM_PROJ
    tok = lambda width: pl.BlockSpec((tm, width), lambda i: (i, 0))
    return pl.pallas_call(
        functools.partial(_ffn_kernel, tiles_per_batch=tiles_per_batch),
        grid=(n // tm,),
        in_specs=[tok(D_MODEL),
                  pl.BlockSpec((1, 6, D_MODEL), lambda i: (i // tiles_per_batch, 0, 0)),
                  _const_spec(g2.shape), _const_spec(wup.shape), _const_spec(cw.shape),
                  _const_spec(cb.shape), _const_spec(wd.shape)],
        out_specs=tok(D_MODEL),
        out_shape=jax.ShapeDtypeStruct((n, D_MODEL), f32),
        scratch_shapes=[pltpu.VMEM((N_FCHUNK, (FFN_CONV - 1) * SUBLANES, F_CHUNK), f32),
                        pltpu.VMEM((N_FCHUNK, (FFN_CONV - 1) * SUBLANES, F_CHUNK), f32),
                        pltpu.VMEM((tm, D_FF), bf16),
                        pltpu.VMEM((D_MODEL // LANES, tm + SUBLANES * SUBLANES, LANES), f32)],
        compiler_params=pltpu.CompilerParams(dimension_semantics=("arbitrary",),
                                             vmem_limit_bytes=VMEM_LIMIT),
        name="ffn",
    )(x2, mod3, g2, wup, cw, cb, wd)


def _pad_cols(a, width):
    return jnp.pad(a, ((0, 0), (0, width - a.shape[1])))


def _layer(x2, c8, batch, seq, layer, w_ada, b_ada, norm1_g, w_in, m_conv_w, m_conv_b, m_igate_b,
           m_fgate_b, m_norm_g, a_qnorm_g, a_knorm_g, a_lambda, a_norm_g, tab, bias, w_branch_m,
           w_branch_a, w_out, norm2_g, w_up, ffn_conv_w, ffn_conv_b, w_down):
    tiles_per_batch = seq // TM_PROJ
    mod3 = _adaln(c8, w_ada, b_ada.reshape(1, -1))[:batch].reshape(batch, 6, D_MODEL)

    o = 0
    parts = {}
    for name, size in (("mqk", 2 * M_WIDTH), ("mv", M_WIDTH), ("mo", M_WIDTH), ("mi", M_HEADS),
                       ("mf", M_HEADS), ("aq", A_WIDTH), ("ak", A_WIDTH), ("av", A_WIDTH),
                       ("gm", D_MODEL), ("ga", D_MODEL)):
        parts[name] = w_in[:, o:o + size]
        o += size

    def per_head(w):
        return w.reshape(D_MODEL, 2, A_HEADS, A_QK_DIM).transpose(0, 2, 1, 3).reshape(D_MODEL, A_WIDTH)

    w_cat = jnp.concatenate(
        [parts["mqk"], parts["mv"], parts["mo"], per_head(parts["aq"]), per_head(parts["ak"]),
         parts["av"], parts["gm"], parts["ga"], _pad_cols(parts["mi"], LANES),
         _pad_cols(parts["mf"], LANES)], axis=1).astype(bf16)
    gid = jnp.arange(2 * LANES) // A_QK_DIM
    grp = jnp.where(gid[:, None] == gid[None, :], 1.0 / A_QK_DIM, 0.0).astype(bf16)
    qg = (jnp.tile(a_qnorm_g, A_WIDTH // A_QK_DIM) * (A_QK_DIM ** -0.5 * LOG2E)).reshape(1, A_WIDTH)
    kg = jnp.tile(a_knorm_g, A_WIDTH // A_QK_DIM).reshape(1, A_WIDTH)

    mqk, mv, mo, qn, kn, av, gm, ga, gates = _inproj(
        x2, mod3, norm1_g.reshape(1, -1), w_cat, grp, qg, kg, tiles_per_batch)

    hm = _mlstm(mqk, mv, mo, gates, m_conv_w, m_conv_b.reshape(1, -1),
                _pad_cols(m_igate_b.reshape(1, -1), LANES), _pad_cols(m_fgate_b.reshape(1, -1), LANES),
                m_norm_g.reshape(1, -1), batch, seq)

    lam_init = 0.8 - 0.6 * math.exp(-0.3 * layer)
    ha = _diffattn(tab, qn, kn, av, bias, a_lambda, a_norm_g.reshape(1, -1), qg, batch, seq, lam_init)

    x1 = _merge(x2, mod3, hm, ha, gm, ga, w_branch_m.astype(bf16), w_branch_a.astype(bf16),
                w_out.astype(bf16), tiles_per_batch)

    return _ffn(x1, mod3, norm2_g.reshape(1, -1), w_up.astype(bf16), ffn_conv_w, ffn_conv_b.reshape(1, -1),
                w_down.astype(bf16), tiles_per_batch)


def kernel(x, c, w_ada, b_ada, norm1_g, w_in, m_conv_w, m_conv_b, m_igate_b, m_fgate_b, m_norm_g,
           a_qnorm_g, a_knorm_g, a_lambda, a_norm_g, rel_bias, w_branch_m, w_branch_a, w_out, norm2_g,
           w_up, ffn_conv_w, ffn_conv_b, w_down):
    batch, seq, _ = x.shape
    depth = w_ada.shape[0]
    x2 = x.reshape(batch * seq, D_MODEL)
    c8 = jnp.pad(c, ((0, SUBLANES - batch), (0, 0)))
    tab = rel_bias.astype(f32).T
    bias = _bias_tiles(tab)
    for l in range(depth):
        x2 = _layer(x2, c8, batch, seq, l, w_ada[l], b_ada[l], norm1_g[l], w_in[l], m_conv_w[l],
                    m_conv_b[l], m_igate_b[l], m_fgate_b[l], m_norm_g[l], a_qnorm_g[l], a_knorm_g[l],
                    a_lambda[l], a_norm_g[l], tab, bias, w_branch_m[l], w_branch_a[l], w_out[l], norm2_g[l],
                    w_up[l], ffn_conv_w[l], ffn_conv_b[l], w_down[l])
    return x2.reshape(batch, seq, D_MODEL)
```

```python
import functools
import math

import jax
import jax.numpy as jnp
from jax import lax
from jax.experimental import pallas as pl
from jax.experimental.pallas import tpu as pltpu

D_MODEL = 1024
M_HEADS = 4
M_HEAD_DIM = 128
M_WIDTH = M_HEADS * M_HEAD_DIM
M_CONV = 4
A_HEADS = 4
A_QK_DIM = 64
A_V_DIM = 2 * A_QK_DIM
A_WIDTH = A_HEADS * A_V_DIM
N_BUCKETS = 32
MAX_DISTANCE = 128
D_FF = 2816
FFN_CONV = 3
RMS_EPS = 1e-6
LOG2E = math.log2(math.e)
NEG_BIG = -1e30

LANES = 128
SUBLANES = 8
VMEM_LIMIT = 56 * 1024 * 1024

TM_PROJ = 512
M_CHUNK = 256
A_TILE = 512
A_ROWS = 256
A_HPS = 2
A_NORM_ROWS = 1024
A_SAFE_RANGE = 90.0
F_CHUNK = 256
N_FCHUNK = D_FF // F_CHUNK

bf16 = jnp.bfloat16
f32 = jnp.float32


def _dot(a, b):
    return jnp.dot(a, b, preferred_element_type=f32)


def _dot_nt(a, b):
    return lax.dot_general(a, b, (((1,), (1,)), ((), ())), preferred_element_type=f32)


def _sigmoid(x):
    return 1.0 / (1.0 + jnp.exp(-x))


def _const_spec(shape):
    nd = len(shape)
    return pl.BlockSpec(shape, lambda *_: (0,) * nd)


def _to_token_strided(x, stage_ref):
    ng = x.shape[0] // SUBLANES
    pitch = ng + SUBLANES
    nl = x.shape[1] // LANES
    for c in range(nl):
        for sgm in range(SUBLANES):
            stage_ref[c, sgm * pitch:sgm * pitch + ng, :] = x[sgm * ng:(sgm + 1) * ng, c * LANES:(c + 1) * LANES]
    return jnp.concatenate(
        [jnp.concatenate([stage_ref[c, pl.ds(v, SUBLANES, stride=pitch), :] for c in range(nl)], axis=1)
         for v in range(ng)], axis=0)


def _from_token_strided(xs, stage_ref):
    ng = xs.shape[0] // SUBLANES
    pitch = ng + SUBLANES
    nl = xs.shape[1] // LANES
    for c in range(nl):
        for v in range(ng):
            stage_ref[c, pl.ds(v, SUBLANES, stride=pitch), :] = xs[v * SUBLANES:(v + 1) * SUBLANES,
                                                                   c * LANES:(c + 1) * LANES]
    return jnp.concatenate(
        [jnp.concatenate([stage_ref[c, sgm * pitch:sgm * pitch + ng, :] for c in range(nl)], axis=1)
         for sgm in range(SUBLANES)], axis=0)


def _shift_tokens(u, prev, k):
    rows = u.shape[0]
    ng = rows // SUBLANES
    halo = prev.shape[0] // SUBLANES
    first = lax.broadcasted_iota(jnp.int32, (SUBLANES, u.shape[1]), 0) == 0

    def wrapped(g):
        cur = u[g * SUBLANES:(g + 1) * SUBLANES, :]
        old = prev[(g - (ng - halo)) * SUBLANES:(g - (ng - halo) + 1) * SUBLANES, :]
        return jnp.where(first, pltpu.roll(old, 1, 0), pltpu.roll(cur, 1, 0))

    return jnp.concatenate([wrapped(ng - k + v) for v in range(k)] + [u[:rows - k * SUBLANES, :]], axis=0)


def _adaln_kernel(c_ref, w_ref, b_ref, o_ref):
    c = c_ref[...]
    a = (c * _sigmoid(c)).astype(bf16)
    o_ref[...] = _dot(a, w_ref[...].astype(bf16)) + b_ref[...]


def _adaln(c8, w, b):
    n = w.shape[1]
    tn = 1536
    return pl.pallas_call(
        _adaln_kernel,
        grid=(n // tn,),
        in_specs=[_const_spec(c8.shape),
                  pl.BlockSpec((D_MODEL, tn), lambda j: (0, j)),
                  pl.BlockSpec((1, tn), lambda j: (0, j))],
        out_specs=pl.BlockSpec((c8.shape[0], tn), lambda j: (0, j)),
        out_shape=jax.ShapeDtypeStruct((c8.shape[0], n), f32),
        compiler_params=pltpu.CompilerParams(dimension_semantics=("arbitrary",),
                                             vmem_limit_bytes=VMEM_LIMIT),
        name="adaln",
    )(c8, w, b)


def _bias_kernel(tab_ref, o_ref):
    h = pl.program_id(0)
    t = pl.program_id(1)
    row = lax.broadcasted_iota(jnp.int32, (A_TILE, A_TILE), 0)
    col = lax.broadcasted_iota(jnp.int32, (A_TILE, A_TILE), 1)
    dist = row - col + (1 - t) * A_TILE
    n = jnp.maximum(dist, 0)
    max_exact = N_BUCKETS // 2
    nf = jnp.maximum(n, 1).astype(f32)
    large = max_exact + (jnp.log(nf / max_exact) / math.log(MAX_DISTANCE / max_exact)
                         * (N_BUCKETS - max_exact)).astype(jnp.int32)
    large = jnp.minimum(large, N_BUCKETS - 1)
    bucket = jnp.where(n < max_exact, n, large)
    far = tab_ref[h, N_BUCKETS - 1]
    val = jnp.zeros((A_TILE, A_TILE), f32)
    for b in range(N_BUCKETS - 1):
        val = jnp.where(bucket == b, tab_ref[h, b] - far, val)
    o_ref[0] = jnp.where(dist >= 0, val * LOG2E, NEG_BIG)


def _bias_tiles(tab):
    return pl.pallas_call(
        _bias_kernel,
        grid=(A_HEADS, 2),
        in_specs=[pl.BlockSpec(memory_space=pltpu.SMEM)],
        out_specs=pl.BlockSpec((1, A_TILE, A_TILE), lambda h, t: (h, 0, t)),
        out_shape=jax.ShapeDtypeStruct((A_HEADS, A_TILE, 2 * A_TILE), f32),
        compiler_params=pltpu.CompilerParams(dimension_semantics=("arbitrary", "arbitrary"),
                                             vmem_limit_bytes=VMEM_LIMIT),
        name="bias_tiles",
    )(tab)


C_MQK, C_MV, C_MO, C_AQ, C_AK, C_AV, C_GM, C_GA, C_GATE, C_END = (
    0, 1024, 1536, 2048, 2560, 3072, 3584, 4608, 5632, 5888)


def _inproj_kernel(x_ref, mod_ref, g_ref, w_ref, grp_ref, qg_ref, kg_ref,
                   mqk_ref, mv_ref, mo_ref, qn_ref, kn_ref, av_ref, gm_ref, ga_ref, gate_ref):
    x = x_ref[...]
    mod = mod_ref[0]
    ms = jnp.mean(x * x, axis=-1, keepdims=True)
    y = x * lax.rsqrt(ms + RMS_EPS) * g_ref[...]
    hb = (y * (1.0 + mod[1:2]) + mod[0:1]).astype(bf16)

    def proj(c0, c1):
        return _dot(hb, w_ref[:, c0:c1])

    def qknorm(a, gain_ref):
        sq = (a * a).astype(bf16)
        gw = grp_ref.shape[0]
        msq = jnp.concatenate([_dot(sq[:, c:c + gw], grp_ref[...]) for c in range(0, a.shape[1], gw)], axis=1)
        return (a * lax.rsqrt(msq + RMS_EPS) * gain_ref[...]).astype(bf16)

    mqk_ref[:, 0:512] = proj(C_MQK, C_MQK + 512).astype(bf16)
    mqk_ref[:, 512:1024] = proj(C_MQK + 512, C_MV).astype(bf16)
    mv_ref[...] = proj(C_MV, C_MO).astype(bf16)
    mo_ref[...] = proj(C_MO, C_AQ).astype(bf16)
    qn_ref[...] = qknorm(proj(C_AQ, C_AK), qg_ref)
    kn_ref[...] = qknorm(proj(C_AK, C_AV), kg_ref)
    av_ref[...] = proj(C_AV, C_GM).astype(bf16)
    gm_ref[:, 0:512] = proj(C_GM, C_GM + 512).astype(bf16)
    gm_ref[:, 512:1024] = proj(C_GM + 512, C_GA).astype(bf16)
    ga_ref[:, 0:512] = proj(C_GA, C_GA + 512).astype(bf16)
    ga_ref[:, 512:1024] = proj(C_GA + 512, C_GATE).astype(bf16)
    gate_ref[...] = proj(C_GATE, C_END)


def _inproj(x2, mod3, g1, w, grp, qg, kg, tiles_per_batch):
    n = x2.shape[0]
    tm = TM_PROJ
    tok = lambda width: pl.BlockSpec((tm, width), lambda i: (i, 0))
    outs = [(1024, bf16), (512, bf16), (512, bf16), (512, bf16), (512, bf16), (512, bf16),
            (1024, bf16), (1024, bf16), (2 * LANES, f32)]
    return pl.pallas_call(
        _inproj_kernel,
        grid=(n // tm,),
        in_specs=[tok(D_MODEL),
                  pl.BlockSpec((1, 6, D_MODEL), lambda i: (i // tiles_per_batch, 0, 0)),
                  _const_spec(g1.shape), _const_spec(w.shape), _const_spec(grp.shape),
                  _const_spec(qg.shape), _const_spec(kg.shape)],
        out_specs=[tok(wd) for wd, _ in outs],
        out_shape=[jax.ShapeDtypeStruct((n, wd), dt) for wd, dt in outs],
        compiler_params=pltpu.CompilerParams(dimension_semantics=("arbitrary",),
                                             vmem_limit_bytes=VMEM_LIMIT),
        name="inproj",
    )(x2, mod3, g1, w, grp, qg, kg)


def _mlstm_kernel(mqk_ref, mv_ref, mo_ref, gate_ref, cw_ref, cb_ref, bi_ref, bfg_ref, mg_ref,
                  o_ref, c_st, m_st, tail_ref, stage_ref):
    nb = mqk_ref.shape[0]
    L = M_CHUNK
    d = M_HEAD_DIM

    @pl.when(pl.program_id(0) == 0)
    def _():
        c_st[...] = jnp.zeros_like(c_st)
        m_st[...] = jnp.zeros_like(m_st)
        tail_ref[...] = jnp.zeros_like(tail_ref)

    row = lax.broadcasted_iota(jnp.int32, (L, L), 0)
    col = lax.broadcasted_iota(jnp.int32, (L, L), 1)
    causal = row >= col
    tri = jnp.where(causal, 1.0, 0.0).astype(bf16)
    ones = jnp.ones((L, d), bf16)
    cw = cw_ref[...]

    for b in range(nb):
        x = _to_token_strided(mqk_ref[b].astype(f32), stage_ref.at[b])
        prev = tail_ref[b]
        y = (cw[3:4] * x + cw[2:3] * _shift_tokens(x, prev, 1) + cw[1:2] * _shift_tokens(x, prev, 2)
             + cw[0:1] * _shift_tokens(x, prev, 3) + cb_ref[...])
        tail_ref[b] = x[L - (M_CONV - 1) * SUBLANES:, :]
        qk = _from_token_strided(y * _sigmoid(y), stage_ref.at[b])

        gi = gate_ref[b, :, 0:LANES] + bi_ref[...]
        gf = gate_ref[b, :, LANES:2 * LANES] + bfg_ref[...]
        logf = jnp.minimum(gf, 0.0) - jnp.log(1.0 + jnp.exp(-jnp.abs(gf)))
        logf_hi = logf.astype(bf16)
        logf_lo = (logf - logf_hi.astype(f32)).astype(bf16)
        bcum = _dot(tri, logf_hi) + _dot(tri, logf_lo)
        r = gi - bcum
        r_t = r.T
        b_last = bcum[L - 1:L, :]
        g = b_last + r
        m_loc = jnp.max(g, axis=0, keepdims=True)

        for h in range(M_HEADS):
            st = b * M_HEADS + h
            hs = slice(h * d, (h + 1) * d)
            qh = qk[:, hs].astype(bf16)
            kh = qk[:, M_WIDTH + h * d:M_WIDTH + (h + 1) * d] * (d ** -0.5)
            vaug = jnp.concatenate([mv_ref[b, :, hs], ones], axis=1)
            m_in = m_st[st][0:1, 0:1]
            b_col = bcum[:, h:h + 1]
            dmat = jnp.where(causal, b_col + r_t[h:h + 1, :], NEG_BIG)
            a_t = b_col + m_in
            m_t = jnp.maximum(a_t, jnp.max(dmat, axis=1, keepdims=True))
            inter_w = jnp.exp(a_t - m_t)
            s = _dot_nt(qh, kh.astype(bf16)) * jnp.exp(dmat - m_t)
            tot = _dot(s.astype(bf16), vaug) + inter_w * _dot(qh, c_st[st].astype(bf16))
            den = jnp.maximum(jnp.abs(tot[:, d:]), jnp.exp(-m_t))
            hh = tot[:, :d] / den

            bl = b_last[:, h:h + 1]
            m_new = jnp.maximum(bl + m_in, m_loc[:, h:h + 1])
            w_col = jnp.exp(g[:, h:h + 1] - m_new)
            kw_t = (kh * w_col).T.astype(bf16)
            c_st[st] = jnp.exp(bl + m_in - m_new) * c_st[st] + _dot(kw_t, vaug)
            m_st[st] = jnp.broadcast_to(m_new, (SUBLANES, LANES))

            hn = hh * lax.rsqrt(jnp.mean(hh * hh, axis=-1, keepdims=True) + RMS_EPS) * mg_ref[:, hs]
            o_ref[b, :, hs] = (_sigmoid(mo_ref[b, :, hs].astype(f32)) * hn).astype(bf16)


def _mlstm(mqk, mv, mo, gates, cw, cb, bi, bfg, mg, batch, seq):
    L = M_CHUNK
    tok = lambda width: pl.BlockSpec((batch, L, width), lambda c: (0, c, 0))
    per_batch = lambda a: a.reshape(batch, seq, a.shape[-1])
    out = pl.pallas_call(
        _mlstm_kernel,
        grid=(seq // L,),
        in_specs=[tok(2 * M_WIDTH), tok(M_WIDTH), tok(M_WIDTH), tok(2 * LANES),
                  _const_spec(cw.shape), _const_spec(cb.shape), _const_spec(bi.shape),
                  _const_spec(bfg.shape), _const_spec(mg.shape)],
        out_specs=tok(M_WIDTH),
        out_shape=jax.ShapeDtypeStruct((batch, seq, M_WIDTH), bf16),
        scratch_shapes=[pltpu.VMEM((batch * M_HEADS, M_HEAD_DIM, 2 * M_HEAD_DIM), f32),
                        pltpu.VMEM((batch * M_HEADS, SUBLANES, LANES), f32),
                        pltpu.VMEM((batch, (M_CONV - 1) * SUBLANES, 2 * M_WIDTH), f32),
                        pltpu.VMEM((batch, 2 * M_WIDTH // LANES, L + SUBLANES * SUBLANES, LANES), f32)],
        compiler_params=pltpu.CompilerParams(dimension_semantics=("arbitrary",),
                                             vmem_limit_bytes=VMEM_LIMIT),
        name="mlstm",
    )(per_batch(mqk), per_batch(mv), per_batch(mo), per_batch(gates), cw, cb, bi, bfg, mg)
    return out.reshape(batch * seq, M_WIDTH)


def _attn_kernel(tab_ref, q_ref, k_ref, v_ref, bias_ref, lam_ref, ng_ref, qg_ref, o_ref,
                 vaug_ref, q2_ref, acc_ref, m_ref, mfix_ref, kmax_ref, safe_ref, *, lam_init):
    T = A_TILE
    dv = A_V_DIM
    hg = pl.program_id(1)
    qi = pl.program_id(2)
    heads = range(A_HPS)
    sel = jnp.where((lax.broadcasted_iota(jnp.int32, (LANES, 2 * LANES), 0) < A_QK_DIM)
                    == (lax.broadcasted_iota(jnp.int32, (LANES, 2 * LANES), 1) < LANES), 1.0, 0.0).astype(bf16)

    def sq_norms(x):
        xf = x.astype(f32)
        return _dot((xf * xf).astype(bf16), sel)

    def lanes_of(j):
        return slice(j * LANES, (j + 1) * LANES)

    bmax, bmin = [], []
    for j in heads:
        hi = jnp.float32(0.0)
        lo = jnp.float32(0.0)
        for b in range(N_BUCKETS - 1):
            rel = (tab_ref[hg * A_HPS + j, b] - tab_ref[hg * A_HPS + j, N_BUCKETS - 1]) * LOG2E
            hi = jnp.maximum(hi, rel)
            lo = jnp.minimum(lo, rel)
        bmax.append(hi)
        bmin.append(lo)

    @pl.when(qi == 0)
    def _():
        ok = None
        for j in heads:
            vaug_ref[j, :, 0:dv] = v_ref[:, lanes_of(j)]
            vaug_ref[j, :, dv:2 * dv] = jnp.ones((v_ref.shape[0], dv), bf16)
            kmax = jnp.zeros((1, 2 * LANES), f32)
            for c in range(k_ref.shape[0] // A_NORM_ROWS):
                kn2 = sq_norms(k_ref[c * A_NORM_ROWS:(c + 1) * A_NORM_ROWS, lanes_of(j)])
                kmax = jnp.maximum(kmax, jnp.max(kn2, axis=0, keepdims=True))
            kmax_ref[j] = kmax
            qmax = math.sqrt(A_QK_DIM) * jnp.max(jnp.abs(qg_ref[:, lanes_of(j)]))
            span = 2.0 * qmax * jnp.sqrt(jnp.max(kmax)) + (bmax[j] - bmin[j])
            ok_j = span < A_SAFE_RANGE
            ok = ok_j if ok is None else jnp.logical_and(ok, ok_j)
        safe_ref[0] = ok.astype(jnp.int32)

    for j in heads:
        q = q_ref[:, lanes_of(j)]
        lane = lax.broadcasted_iota(jnp.int32, q.shape, 1)
        zero = jnp.zeros_like(q)
        q2_ref[j, 0:T, :] = jnp.where(lane < A_QK_DIM, q, zero)
        q2_ref[j, T:2 * T, :] = jnp.where(lane >= A_QK_DIM, q, zero)
        bound = jnp.sqrt(sq_norms(q) * kmax_ref[j])
        mfix_ref[j, 0:T, :] = bound[:, 0:LANES] + bmax[j]
        mfix_ref[j, T:2 * T, :] = bound[:, LANES:2 * LANES] + bmax[j]
    acc_ref[...] = jnp.zeros_like(acc_ref)
    safe = safe_ref[0] == 1

    def step(start, nk, near, fixed_shift):
        start = pl.multiple_of(start, T)
        for rb in range(2 * T // A_ROWS):
            rows = slice(rb * A_ROWS, (rb + 1) * A_ROWS)
            brow = (rb * A_ROWS) % T
            nkb = nk - (T - brow - A_ROWS) if near else nk
            for j in heads:
                kt = k_ref[pl.ds(start, nkb), lanes_of(j)]
                va = vaug_ref[j, pl.ds(start, nkb), :]
                s = _dot_nt(q2_ref[j, rows, :], kt)
                if near:
                    s = s + bias_ref[j, brow:brow + A_ROWS, 2 * T - nk:2 * T - nk + nkb]
                if fixed_shift:
                    mrow = mfix_ref[j, rows, :]
                    p = jnp.exp2(s - jnp.concatenate([mrow] * (nkb // LANES), axis=1)).astype(bf16)
                    acc_ref[j, rows, :] += _dot(p, va)
                else:
                    m_old = m_ref[j, rows, :]
                    m_new = jnp.maximum(m_old, jnp.max(s, axis=1, keepdims=True))
                    p = jnp.exp2(s - m_new).astype(bf16)
                    acc_ref[j, rows, :] = jnp.exp2(m_old - m_new) * acc_ref[j, rows, :] + _dot(p, va)
                    m_ref[j, rows, :] = m_new

    def sweep(fixed_shift):
        nfar = jnp.maximum(qi - 1, 0)
        npair = nfar // 2

        def far_body(i, carry):
            step(i * (2 * T), 2 * T, False, fixed_shift)
            return carry

        lax.fori_loop(0, npair, far_body, 0)

        @pl.when(nfar % 2 == 1)
        def _():
            step(npair * (2 * T), T, False, fixed_shift)

        @pl.when(qi >= 1)
        def _():
            step((qi - 1) * T, 2 * T, True, fixed_shift)

        @pl.when(qi == 0)
        def _():
            step(0, T, True, fixed_shift)

    @pl.when(safe)
    def _():
        sweep(True)

    @pl.when(jnp.logical_not(safe))
    def _():
        m_ref[...] = jnp.full_like(m_ref, NEG_BIG)
        sweep(False)

    al = lam_ref[...]
    lam = (jnp.exp(jnp.sum(al[0:1] * al[1:2], keepdims=True))
           - jnp.exp(jnp.sum(al[2:3] * al[3:4], keepdims=True)) + lam_init)
    for j in heads:
        acc = acc_ref[j]
        o = acc[:, 0:dv] / acc[:, dv:2 * dv]
        ha = o[0:T] - lam * o[T:2 * T]
        hn = ha * lax.rsqrt(jnp.mean(ha * ha, axis=-1, keepdims=True) + RMS_EPS) * ng_ref[:, lanes_of(j)]
        o_ref[:, lanes_of(j)] = (hn * (1.0 - lam_init)).astype(bf16)


def _diffattn(tab, qn, kn, av, bias, lam_par, ng, qg, batch, seq, lam_init):
    T = A_TILE
    nq = seq // T
    wide = A_HPS * LANES
    return pl.pallas_call(
        functools.partial(_attn_kernel, lam_init=lam_init),
        grid=(batch, A_HEADS // A_HPS, nq),
        in_specs=[pl.BlockSpec(memory_space=pltpu.SMEM),
                  pl.BlockSpec((T, wide), lambda b, h, i: (b * nq + i, h)),
                  pl.BlockSpec((seq, wide), lambda b, h, i: (b, h)),
                  pl.BlockSpec((seq, wide), lambda b, h, i: (b, h)),
                  pl.BlockSpec((A_HPS, T, 2 * T), lambda b, h, i: (h, 0, 0)),
                  _const_spec(lam_par.shape),
                  pl.BlockSpec((1, wide), lambda b, h, i: (0, h)),
                  pl.BlockSpec((1, wide), lambda b, h, i: (0, h))],
        out_specs=pl.BlockSpec((T, wide), lambda b, h, i: (b * nq + i, h)),
        out_shape=jax.ShapeDtypeStruct((batch * seq, A_WIDTH), bf16),
        scratch_shapes=[pltpu.VMEM((A_HPS, seq, 2 * A_V_DIM), bf16),
                        pltpu.VMEM((A_HPS, 2 * T, LANES), bf16),
                        pltpu.VMEM((A_HPS, 2 * T, 2 * A_V_DIM), f32),
                        pltpu.VMEM((A_HPS, 2 * T, 1), f32),
                        pltpu.VMEM((A_HPS, 2 * T, LANES), f32),
                        pltpu.VMEM((A_HPS, 1, 2 * LANES), f32),
                        pltpu.SMEM((1,), jnp.int32)],
        compiler_params=pltpu.CompilerParams(
            dimension_semantics=("arbitrary", "arbitrary", "arbitrary"),
            vmem_limit_bytes=VMEM_LIMIT),
        name="diffattn",
    )(tab, qn, kn, av, bias, lam_par, ng, qg)


def _merge_kernel(x_ref, mod_ref, hm_ref, ha_ref, gm_ref, ga_ref, wm_ref, wa_ref, wo_ref, o_ref):
    ym = _dot(hm_ref[...], wm_ref[...])
    ya = _dot(ha_ref[...], wa_ref[...])
    y = _sigmoid(gm_ref[...].astype(f32)) * ym + _sigmoid(ga_ref[...].astype(f32)) * ya
    o_ref[...] = x_ref[...] + mod_ref[0][2:3] * _dot(y.astype(bf16), wo_ref[...])


def _merge(x2, mod3, hm, ha, gm, ga, wm, wa, wo, tiles_per_batch):
    n = x2.shape[0]
    tm = TM_PROJ
    tok = lambda width: pl.BlockSpec((tm, width), lambda i: (i, 0))
    return pl.pallas_call(
        _merge_kernel,
        grid=(n // tm,),
        in_specs=[tok(D_MODEL),
                  pl.BlockSpec((1, 6, D_MODEL), lambda i: (i // tiles_per_batch, 0, 0)),
                  tok(M_WIDTH), tok(A_WIDTH), tok(D_MODEL), tok(D_MODEL),
                  _const_spec(wm.shape), _const_spec(wa.shape), _const_spec(wo.shape)],
        out_specs=tok(D_MODEL),
        out_shape=jax.ShapeDtypeStruct((n, D_MODEL), f32),
        compiler_params=pltpu.CompilerParams(dimension_semantics=("arbitrary",),
                                             vmem_limit_bytes=VMEM_LIMIT),
        name="merge",
    )(x2, mod3, hm, ha, gm, ga, wm, wa, wo)


def _ffn_kernel(x_ref, mod_ref, g_ref, wup_ref, cw_ref, cb_ref, wd_ref,
                o_ref, hv_ref, hg_ref, act_ref, stage_ref, *, tiles_per_batch):
    tm = x_ref.shape[0]
    halo = FFN_CONV - 1

    @pl.when(pl.program_id(0) % tiles_per_batch == 0)
    def _():
        hv_ref[...] = jnp.zeros_like(hv_ref)
        hg_ref[...] = jnp.zeros_like(hg_ref)

    x = _to_token_strided(x_ref[...], stage_ref)
    mod = mod_ref[0]
    ms = jnp.mean(x * x, axis=-1, keepdims=True)
    y = x * lax.rsqrt(ms + RMS_EPS) * g_ref[...]
    hb = (y * (1.0 + mod[4:5]) + mod[3:4]).astype(bf16)

    def conv(u, prev, w, b):
        return w[2:3] * u + w[1:2] * _shift_tokens(u, prev, 1) + w[0:1] * _shift_tokens(u, prev, 2) + b

    for j in range(N_FCHUNK):
        cols = slice(j * F_CHUNK, (j + 1) * F_CHUNK)
        gcols = slice(D_FF + j * F_CHUNK, D_FF + (j + 1) * F_CHUNK)
        uv = _dot(hb, wup_ref[:, cols])
        ug = _dot(hb, wup_ref[:, gcols])
        cv = conv(uv, hv_ref[j], cw_ref[:, cols], cb_ref[:, cols])
        cg = conv(ug, hg_ref[j], cw_ref[:, gcols], cb_ref[:, gcols])
        hv_ref[j] = uv[tm - halo * SUBLANES:, :]
        hg_ref[j] = ug[tm - halo * SUBLANES:, :]
        act_ref[:, cols] = ((cg * _sigmoid(cg)) * cv).astype(bf16)
    o_ref[...] = _from_token_strided(x + mod[5:6] * _dot(act_ref[...], wd_ref[...]), stage_ref)


def _ffn(x2, mod3, g2, wup, cw, cb, wd, tiles_per_batch):
    n = x2.shape[0]
    tm = TM_PROJ
    tok = lambda width: pl.BlockSpec((tm, width), lambda i: (i, 0))
    return pl.pallas_call(
        functools.partial(_ffn_kernel, tiles_per_batch=tiles_per_batch),
        grid=(n // tm,),
        in_specs=[tok(D_MODEL),
                  pl.BlockSpec((1, 6, D_MODEL), lambda i: (i // tiles_per_batch, 0, 0)),
                  _const_spec(g2.shape), _const_spec(wup.shape), _const_spec(cw.shape),
                  _const_spec(cb.shape), _const_spec(wd.shape)],
        out_specs=tok(D_MODEL),
        out_shape=jax.ShapeDtypeStruct((n, D_MODEL), f32),
        scratch_shapes=[pltpu.VMEM((N_FCHUNK, (FFN_CONV - 1) * SUBLANES, F_CHUNK), f32),
                        pltpu.VMEM((N_FCHUNK, (FFN_CONV - 1) * SUBLANES, F_CHUNK), f32),
                        pltpu.VMEM((tm, D_FF), bf16),
                        pltpu.VMEM((D_MODEL // LANES, tm + SUBLANES * SUBLANES, LANES), f32)],
        compiler_params=pltpu.CompilerParams(dimension_semantics=("arbitrary",),
                                             vmem_limit_bytes=VMEM_LIMIT),
        name="ffn",
    )(x2, mod3, g2, wup, cw, cb, wd)


def _pad_cols(a, width):
    return jnp.pad(a, ((0, 0), (0, width - a.shape[1])))


def _layer(x2, c8, batch, seq, layer, w_ada, b_ada, norm1_g, w_in, m_conv_w, m_conv_b, m_igate_b,
           m_fgate_b, m_norm_g, a_qnorm_g, a_knorm_g, a_lambda, a_norm_g, tab, bias, w_branch_m,
           w_branch_a, w_out, norm2_g, w_up, ffn_conv_w, ffn_conv_b, w_down):
    tiles_per_batch = seq // TM_PROJ
    mod3 = _adaln(c8, w_ada, b_ada.reshape(1, -1))[:batch].reshape(batch, 6, D_MODEL)

    o = 0
    parts = {}
    for name, size in (("mqk", 2 * M_WIDTH), ("mv", M_WIDTH), ("mo", M_WIDTH), ("mi", M_HEADS),
                       ("mf", M_HEADS), ("aq", A_WIDTH), ("ak", A_WIDTH), ("av", A_WIDTH),
                       ("gm", D_MODEL), ("ga", D_MODEL)):
        parts[name] = w_in[:, o:o + size]
        o += size

    def per_head(w):
        return w.reshape(D_MODEL, 2, A_HEADS, A_QK_DIM).transpose(0, 2, 1, 3).reshape(D_MODEL, A_WIDTH)

    w_cat = jnp.concatenate(
        [parts["mqk"], parts["mv"], parts["mo"], per_head(parts["aq"]), per_head(parts["ak"]),
         parts["av"], parts["gm"], parts["ga"], _pad_cols(parts["mi"], LANES),
         _pad_cols(parts["mf"], LANES)], axis=1).astype(bf16)
    gid = jnp.arange(2 * LANES) // A_QK_DIM
    grp = jnp.where(gid[:, None] == gid[None, :], 1.0 / A_QK_DIM, 0.0).astype(bf16)
    qg = (jnp.tile(a_qnorm_g, A_WIDTH // A_QK_DIM) * (A_QK_DIM ** -0.5 * LOG2E)).reshape(1, A_WIDTH)
    kg = jnp.tile(a_knorm_g, A_WIDTH // A_QK_DIM).reshape(1, A_WIDTH)

    mqk, mv, mo, qn, kn, av, gm, ga, gates = _inproj(
        x2, mod3, norm1_g.reshape(1, -1), w_cat, grp, qg, kg, tiles_per_batch)

    hm = _mlstm(mqk, mv, mo, gates, m_conv_w, m_conv_b.reshape(1, -1),
                _pad_cols(m_igate_b.reshape(1, -1), LANES), _pad_cols(m_fgate_b.reshape(1, -1), LANES),
                m_norm_g.reshape(1, -1), batch, seq)

    lam_init = 0.8 - 0.6 * math.exp(-0.3 * layer)
    ha = _diffattn(tab, qn, kn, av, bias, a_lambda, a_norm_g.reshape(1, -1), qg, batch, seq, lam_init)

    x1 = _merge(x2, mod3, hm, ha, gm, ga, w_branch_m.astype(bf16), w_branch_a.astype(bf16),
                w_out.astype(bf16), tiles_per_batch)

    return _ffn(x1, mod3, norm2_g.reshape(1, -1), w_up.astype(bf16), ffn_conv_w, ffn_conv_b.reshape(1, -1),
                w_down.astype(bf16), tiles_per_batch)


def kernel(x, c, w_ada, b_ada, norm1_g, w_in, m_conv_w, m_conv_b, m_igate_b, m_fgate_b, m_norm_g,
           a_qnorm_g, a_knorm_g, a_lambda, a_norm_g, rel_bias, w_branch_m, w_branch_a, w_out, norm2_g,
           w_up, ffn_conv_w, ffn_conv_b, w_down):
    batch, seq, _ = x.shape
    depth = w_ada.shape[0]
    x2 = x.reshape(batch * seq, D_MODEL)
    c8 = jnp.pad(c, ((0, SUBLANES - batch), (0, 0)))
    tab = rel_bias.astype(f32).T
    bias = _bias_tiles(tab)
    for l in range(depth):
        x2 = _layer(x2, c8, batch, seq, l, w_ada[l], b_ada[l], norm1_g[l], w_in[l], m_conv_w[l],
                    m_conv_b[l], m_igate_b[l], m_fgate_b[l], m_norm_g[l], a_qnorm_g[l], a_knorm_g[l],
                    a_lambda[l], a_norm_g[l], tab, bias, w_branch_m[l], w_branch_a[l], w_out[l], norm2_g[l],
                    w_up[l], ffn_conv_w[l], ffn_conv_b[l], w_down[l])
    return x2.reshape(batch, seq, D_MODEL)
```

```python
import functools
import math

import jax
import jax.numpy as jnp
from jax import lax
from jax.experimental import pallas as pl
from jax.experimental.pallas import tpu as pltpu

D_MODEL = 1024
M_HEADS = 4
M_HEAD_DIM = 128
M_WIDTH = M_HEADS * M_HEAD_DIM
M_CONV = 4
A_HEADS = 4
A_QK_DIM = 64
A_V_DIM = 2 * A_QK_DIM
A_WIDTH = A_HEADS * A_V_DIM
N_BUCKETS = 32
MAX_DISTANCE = 128
D_FF = 2816
FFN_CONV = 3
RMS_EPS = 1e-6
LOG2E = math.log2(math.e)
NEG_BIG = -1e30

LANES = 128
SUBLANES = 8
VMEM_LIMIT = 56 * 1024 * 1024

TM_PROJ = 512
M_CHUNK = 256
A_TILE = 512
A_ROWS = 256
A_HPS = 2
A_NORM_ROWS = 1024
A_SAFE_RANGE = 90.0
F_CHUNK = 256
N_FCHUNK = D_FF // F_CHUNK

bf16 = jnp.bfloat16
f32 = jnp.float32


def _dot(a, b):
    return jnp.dot(a, b, preferred_element_type=f32)


def _dot_nt(a, b):
    return lax.dot_general(a, b, (((1,), (1,)), ((), ())), preferred_element_type=f32)


def _sigmoid(x):
    return 1.0 / (1.0 + jnp.exp(-x))


def _const_spec(shape):
    nd = len(shape)
    return pl.BlockSpec(shape, lambda *_: (0,) * nd)


def _shift_rows(u, prev8, k):
    r = pltpu.roll(u, k, 0)
    rp = pltpu.roll(prev8, k, 0)
    row = lax.broadcasted_iota(jnp.int32, (SUBLANES, u.shape[1]), 0)
    first = jnp.where(row < k, rp, r[:SUBLANES])
    return jnp.concatenate([first, r[SUBLANES:]], axis=0)


def _to_token_strided(x, stage_ref):
    ng = x.shape[0] // SUBLANES
    pitch = ng + SUBLANES
    nl = x.shape[1] // LANES
    for c in range(nl):
        for sgm in range(SUBLANES):
            stage_ref[c, sgm * pitch:sgm * pitch + ng, :] = x[sgm * ng:(sgm + 1) * ng, c * LANES:(c + 1) * LANES]
    return jnp.concatenate(
        [jnp.concatenate([stage_ref[c, pl.ds(v, SUBLANES, stride=pitch), :] for c in range(nl)], axis=1)
         for v in range(ng)], axis=0)


def _from_token_strided(xs, stage_ref):
    ng = xs.shape[0] // SUBLANES
    pitch = ng + SUBLANES
    nl = xs.shape[1] // LANES
    for c in range(nl):
        for v in range(ng):
            stage_ref[c, pl.ds(v, SUBLANES, stride=pitch), :] = xs[v * SUBLANES:(v + 1) * SUBLANES,
                                                                   c * LANES:(c + 1) * LANES]
    return jnp.concatenate(
        [jnp.concatenate([stage_ref[c, sgm * pitch:sgm * pitch + ng, :] for c in range(nl)], axis=1)
         for sgm in range(SUBLANES)], axis=0)


def _shift_tokens(u, prev, k):
    rows = u.shape[0]
    ng = rows // SUBLANES
    halo = prev.shape[0] // SUBLANES
    first = lax.broadcasted_iota(jnp.int32, (SUBLANES, u.shape[1]), 0) == 0

    def wrapped(g):
        cur = u[g * SUBLANES:(g + 1) * SUBLANES, :]
        old = prev[(g - (ng - halo)) * SUBLANES:(g - (ng - halo) + 1) * SUBLANES, :]
        return jnp.where(first, pltpu.roll(old, 1, 0), pltpu.roll(cur, 1, 0))

    return jnp.concatenate([wrapped(ng - k + v) for v in range(k)] + [u[:rows - k * SUBLANES, :]], axis=0)


def _adaln_kernel(c_ref, w_ref, b_ref, o_ref):
    c = c_ref[...]
    a = (c * _sigmoid(c)).astype(bf16)
    o_ref[...] = _dot(a, w_ref[...].astype(bf16)) + b_ref[...]


def _adaln(c8, w, b):
    n = w.shape[1]
    tn = 1536
    return pl.pallas_call(
        _adaln_kernel,
        grid=(n // tn,),
        in_specs=[_const_spec(c8.shape),
                  pl.BlockSpec((D_MODEL, tn), lambda j: (0, j)),
                  pl.BlockSpec((1, tn), lambda j: (0, j))],
        out_specs=pl.BlockSpec((c8.shape[0], tn), lambda j: (0, j)),
        out_shape=jax.ShapeDtypeStruct((c8.shape[0], n), f32),
        compiler_params=pltpu.CompilerParams(dimension_semantics=("arbitrary",),
                                             vmem_limit_bytes=VMEM_LIMIT),
        name="adaln",
    )(c8, w, b)


def _bias_kernel(tab_ref, o_ref):
    h = pl.program_id(0)
    t = pl.program_id(1)
    blk = MAX_DISTANCE
    nblk = A_TILE // blk
    far = tab_ref[h, N_BUCKETS - 1]

    def block(r0, c0, shift):
        row = lax.broadcasted_iota(jnp.int32, (blk, blk), 0) + r0
        col = lax.broadcasted_iota(jnp.int32, (blk, blk), 1) + c0
        dist = row - col + shift
        n = jnp.maximum(dist, 0)
        max_exact = N_BUCKETS // 2
        nf = jnp.maximum(n, 1).astype(f32)
        large = max_exact + (jnp.log(nf / max_exact) / math.log(MAX_DISTANCE / max_exact)
                             * (N_BUCKETS - max_exact)).astype(jnp.int32)
        large = jnp.minimum(large, N_BUCKETS - 1)
        bucket = jnp.where(n < max_exact, n, large)
        val = jnp.zeros((blk, blk), f32)
        for b in range(N_BUCKETS - 1):
            val = jnp.where(bucket == b, tab_ref[h, b] - far, val)
        return jnp.where(dist >= 0, val * LOG2E, NEG_BIG)

    @pl.when(t == 0)
    def _():
        o_ref[0] = jnp.zeros((A_TILE, A_TILE), f32)
        o_ref[0, 0:blk, A_TILE - blk:A_TILE] = block(0, A_TILE - blk, A_TILE)

    @pl.when(t == 1)
    def _():
        for i in range(nblk):
            rows = slice(i * blk, (i + 1) * blk)
            if i >= 2:
                o_ref[0, rows, 0:(i - 1) * blk] = jnp.zeros((blk, (i - 1) * blk), f32)
            if i >= 1:
                o_ref[0, rows, (i - 1) * blk:i * blk] = block(i * blk, (i - 1) * blk, 0)
            o_ref[0, rows, i * blk:(i + 1) * blk] = block(i * blk, i * blk, 0)
            if i < nblk - 1:
                o_ref[0, rows, (i + 1) * blk:A_TILE] = jnp.full((blk, A_TILE - (i + 1) * blk), NEG_BIG, f32)


def _bias_tiles(tab):
    return pl.pallas_call(
        _bias_kernel,
        grid=(A_HEADS, 2),
        in_specs=[pl.BlockSpec(memory_space=pltpu.SMEM)],
        out_specs=pl.BlockSpec((1, A_TILE, A_TILE), lambda h, t: (h, 0, t)),
        out_shape=jax.ShapeDtypeStruct((A_HEADS, A_TILE, 2 * A_TILE), f32),
        compiler_params=pltpu.CompilerParams(dimension_semantics=("arbitrary", "arbitrary"),
                                             vmem_limit_bytes=VMEM_LIMIT),
        name="bias_tiles",
    )(tab)


C_MQK, C_MV, C_MO, C_AQ, C_AK, C_AV, C_GM, C_GA, C_GATE, C_END = (
    0, 1024, 1536, 2048, 2560, 3072, 3584, 4608, 5632, 5888)


def _inproj_kernel(x_ref, mod_ref, g_ref, w_ref, grp_ref, qg_ref, kg_ref,
                   mqk_ref, mv_ref, mo_ref, qn_ref, kn_ref, av_ref, gm_ref, ga_ref, gate_ref):
    x = x_ref[...]
    mod = mod_ref[0]
    ms = jnp.mean(x * x, axis=-1, keepdims=True)
    y = x * lax.rsqrt(ms + RMS_EPS) * g_ref[...]
    hb = (y * (1.0 + mod[1:2]) + mod[0:1]).astype(bf16)

    def proj(c0, c1):
        return _dot(hb, w_ref[:, c0:c1])

    def qknorm(a, gain_ref):
        sq = (a * a).astype(bf16)
        gw = grp_ref.shape[0]
        msq = jnp.concatenate([_dot(sq[:, c:c + gw], grp_ref[...]) for c in range(0, a.shape[1], gw)], axis=1)
        return (a * lax.rsqrt(msq + RMS_EPS) * gain_ref[...]).astype(bf16)

    mqk_ref[:, 0:512] = proj(C_MQK, C_MQK + 512).astype(bf16)
    mqk_ref[:, 512:1024] = proj(C_MQK + 512, C_MV).astype(bf16)
    mv_ref[...] = proj(C_MV, C_MO).astype(bf16)
    mo_ref[...] = proj(C_MO, C_AQ).astype(bf16)
    qn_ref[...] = qknorm(proj(C_AQ, C_AK), qg_ref)
    kn_ref[...] = qknorm(proj(C_AK, C_AV), kg_ref)
    av_ref[...] = proj(C_AV, C_GM).astype(bf16)
    gm_ref[:, 0:512] = proj(C_GM, C_GM + 512).astype(bf16)
    gm_ref[:, 512:1024] = proj(C_GM + 512, C_GA).astype(bf16)
    ga_ref[:, 0:512] = proj(C_GA, C_GA + 512).astype(bf16)
    ga_ref[:, 512:1024] = proj(C_GA + 512, C_GATE).astype(bf16)
    gate_ref[...] = proj(C_GATE, C_END)


def _inproj(x2, mod3, g1, w, grp, qg, kg, tiles_per_batch):
    n = x2.shape[0]
    tm = TM_PROJ
    tok = lambda width: pl.BlockSpec((tm, width), lambda i: (i, 0))
    outs = [(1024, bf16), (512, bf16), (512, bf16), (512, bf16), (512, bf16), (512, bf16),
            (1024, bf16), (1024, bf16), (2 * LANES, f32)]
    return pl.pallas_call(
        _inproj_kernel,
        grid=(n // tm,),
        in_specs=[tok(D_MODEL),
                  pl.BlockSpec((1, 6, D_MODEL), lambda i: (i // tiles_per_batch, 0, 0)),
                  _const_spec(g1.shape), _const_spec(w.shape), _const_spec(grp.shape),
                  _const_spec(qg.shape), _const_spec(kg.shape)],
        out_specs=[tok(wd) for wd, _ in outs],
        out_shape=[jax.ShapeDtypeStruct((n, wd), dt) for wd, dt in outs],
        compiler_params=pltpu.CompilerParams(dimension_semantics=("arbitrary",),
                                             vmem_limit_bytes=VMEM_LIMIT),
        name="inproj",
    )(x2, mod3, g1, w, grp, qg, kg)


def _mlstm_kernel(mqk_ref, mv_ref, mo_ref, gate_ref, cw_ref, cb_ref, bi_ref, bfg_ref, mg_ref,
                  o_ref, c_st, m_st, tail_ref):
    nb = mqk_ref.shape[0]
    L = M_CHUNK
    d = M_HEAD_DIM

    @pl.when(pl.program_id(0) == 0)
    def _():
        c_st[...] = jnp.zeros_like(c_st)
        m_st[...] = jnp.zeros_like(m_st)
        tail_ref[...] = jnp.zeros_like(tail_ref)

    row = lax.broadcasted_iota(jnp.int32, (L, L), 0)
    col = lax.broadcasted_iota(jnp.int32, (L, L), 1)
    causal = row >= col
    tri = jnp.where(causal, 1.0, 0.0).astype(bf16)
    ones = jnp.ones((L, d), bf16)
    cw = cw_ref[...]

    for b in range(nb):
        x = mqk_ref[b].astype(f32)
        prev8 = tail_ref[b]
        y = (cw[3:4] * x + cw[2:3] * _shift_rows(x, prev8, 1) + cw[1:2] * _shift_rows(x, prev8, 2)
             + cw[0:1] * _shift_rows(x, prev8, 3) + cb_ref[...])
        tail_ref[b] = x[L - SUBLANES:, :]
        qk = y * _sigmoid(y)

        gi = gate_ref[b, :, 0:LANES] + bi_ref[...]
        gf = gate_ref[b, :, LANES:2 * LANES] + bfg_ref[...]
        logf = jnp.minimum(gf, 0.0) - jnp.log(1.0 + jnp.exp(-jnp.abs(gf)))
        logf_hi = logf.astype(bf16)
        logf_lo = (logf - logf_hi.astype(f32)).astype(bf16)
        bcum = _dot(tri, logf_hi) + _dot(tri, logf_lo)
        r = gi - bcum
        r_t = r.T
        b_last = bcum[L - 1:L, :]
        g = b_last + r
        m_loc = jnp.max(g, axis=0, keepdims=True)

        for h in range(M_HEADS):
            st = b * M_HEADS + h
            hs = slice(h * d, (h + 1) * d)
            qh = qk[:, hs].astype(bf16)
            kh = qk[:, M_WIDTH + h * d:M_WIDTH + (h + 1) * d] * (d ** -0.5)
            vaug = jnp.concatenate([mv_ref[b, :, hs], ones], axis=1)
            m_in = m_st[st][0:1, 0:1]
            b_col = bcum[:, h:h + 1]
            dmat = jnp.where(causal, b_col + r_t[h:h + 1, :], NEG_BIG)
            a_t = b_col + m_in
            m_t = jnp.maximum(a_t, jnp.max(dmat, axis=1, keepdims=True))
            inter_w = jnp.exp(a_t - m_t)
            s = _dot_nt(qh, kh.astype(bf16)) * jnp.exp(dmat - m_t)
            tot = _dot(s.astype(bf16), vaug) + inter_w * _dot(qh, c_st[st].astype(bf16))
            den = jnp.maximum(jnp.abs(tot[:, d:]), jnp.exp(-m_t))
            hh = tot[:, :d] / den

            bl = b_last[:, h:h + 1]
            m_new = jnp.maximum(bl + m_in, m_loc[:, h:h + 1])
            w_col = jnp.exp(g[:, h:h + 1] - m_new)
            kw_t = (kh * w_col).T.astype(bf16)
            c_st[st] = jnp.exp(bl + m_in - m_new) * c_st[st] + _dot(kw_t, vaug)
            m_st[st] = jnp.broadcast_to(m_new, (SUBLANES, LANES))

            hn = hh * lax.rsqrt(jnp.mean(hh * hh, axis=-1, keepdims=True) + RMS_EPS) * mg_ref[:, hs]
            o_ref[b, :, hs] = (_sigmoid(mo_ref[b, :, hs].astype(f32)) * hn).astype(bf16)


def _mlstm(mqk, mv, mo, gates, cw, cb, bi, bfg, mg, batch, seq):
    L = M_CHUNK
    tok = lambda width: pl.BlockSpec((batch, L, width), lambda c: (0, c, 0))
    per_batch = lambda a: a.reshape(batch, seq, a.shape[-1])
    out = pl.pallas_call(
        _mlstm_kernel,
        grid=(seq // L,),
        in_specs=[tok(2 * M_WIDTH), tok(M_WIDTH), tok(M_WIDTH), tok(2 * LANES),
                  _const_spec(cw.shape), _const_spec(cb.shape), _const_spec(bi.shape),
                  _const_spec(bfg.shape), _const_spec(mg.shape)],
        out_specs=tok(M_WIDTH),
        out_shape=jax.ShapeDtypeStruct((batch, seq, M_WIDTH), bf16),
        scratch_shapes=[pltpu.VMEM((batch * M_HEADS, M_HEAD_DIM, 2 * M_HEAD_DIM), f32),
                        pltpu.VMEM((batch * M_HEADS, SUBLANES, LANES), f32),
                        pltpu.VMEM((batch, SUBLANES, 2 * M_WIDTH), f32)],
        compiler_params=pltpu.CompilerParams(dimension_semantics=("arbitrary",),
                                             vmem_limit_bytes=VMEM_LIMIT),
        name="mlstm",
    )(per_batch(mqk), per_batch(mv), per_batch(mo), per_batch(gates), cw, cb, bi, bfg, mg)
    return out.reshape(batch * seq, M_WIDTH)


def _attn_kernel(tab_ref, q_ref, k_ref, v_ref, bias_ref, lam_ref, ng_ref, qg_ref, o_ref,
                 vaug_ref, q2_ref, acc_ref, m_ref, mfix_ref, kmax_ref, safe_ref, *, lam_init):
    T = A_TILE
    dv = A_V_DIM
    hg = pl.program_id(1)
    qi = pl.program_id(2)
    heads = range(A_HPS)
    sel = jnp.where((lax.broadcasted_iota(jnp.int32, (LANES, 2 * LANES), 0) < A_QK_DIM)
                    == (lax.broadcasted_iota(jnp.int32, (LANES, 2 * LANES), 1) < LANES), 1.0, 0.0).astype(bf16)

    def sq_norms(x):
        xf = x.astype(f32)
        return _dot((xf * xf).astype(bf16), sel)

    def lanes_of(j):
        return slice(j * LANES, (j + 1) * LANES)

    bmax, bmin = [], []
    for j in heads:
        hi = jnp.float32(0.0)
        lo = jnp.float32(0.0)
        for b in range(N_BUCKETS - 1):
            rel = (tab_ref[hg * A_HPS + j, b] - tab_ref[hg * A_HPS + j, N_BUCKETS - 1]) * LOG2E
            hi = jnp.maximum(hi, rel)
            lo = jnp.minimum(lo, rel)
        bmax.append(hi)
        bmin.append(lo)

    @pl.when(qi == 0)
    def _():
        ok = None
        for j in heads:
            vaug_ref[j, :, 0:dv] = v_ref[:, lanes_of(j)]
            vaug_ref[j, :, dv:2 * dv] = jnp.ones((v_ref.shape[0], dv), bf16)
            kmax = jnp.zeros((1, 2 * LANES), f32)
            for c in range(k_ref.shape[0] // A_NORM_ROWS):
                kn2 = sq_norms(k_ref[c * A_NORM_ROWS:(c + 1) * A_NORM_ROWS, lanes_of(j)])
                kmax = jnp.maximum(kmax, jnp.max(kn2, axis=0, keepdims=True))
            kmax_ref[j] = kmax
            qmax = math.sqrt(A_QK_DIM) * jnp.max(jnp.abs(qg_ref[:, lanes_of(j)]))
            span = 2.0 * qmax * jnp.sqrt(jnp.max(kmax)) + (bmax[j] - bmin[j])
            ok_j = span < A_SAFE_RANGE
            ok = ok_j if ok is None else jnp.logical_and(ok, ok_j)
        safe_ref[0] = ok.astype(jnp.int32)

    for j in heads:
        q = q_ref[:, lanes_of(j)]
        lane = lax.broadcasted_iota(jnp.int32, q.shape, 1)
        zero = jnp.zeros_like(q)
        q2_ref[j, 0:T, :] = jnp.where(lane < A_QK_DIM, q, zero)
        q2_ref[j, T:2 * T, :] = jnp.where(lane >= A_QK_DIM, q, zero)
        bound = jnp.sqrt(sq_norms(q) * kmax_ref[j])
        mfix_ref[j, 0:T, :] = bound[:, 0:LANES] + bmax[j]
        mfix_ref[j, T:2 * T, :] = bound[:, LANES:2 * LANES] + bmax[j]
    acc_ref[...] = jnp.zeros_like(acc_ref)
    safe = safe_ref[0] == 1

    def step(start, nk, near, fixed_shift):
        start = pl.multiple_of(start, T)
        for rb in range(2 * T // A_ROWS):
            rows = slice(rb * A_ROWS, (rb + 1) * A_ROWS)
            brow = (rb * A_ROWS) % T
            nkb = nk - (T - brow - A_ROWS) if near else nk
            for j in heads:
                kt = k_ref[pl.ds(start, nkb), lanes_of(j)]
                va = vaug_ref[j, pl.ds(start, nkb), :]
                s = _dot_nt(q2_ref[j, rows, :], kt)
                if near:
                    s = s + bias_ref[j, brow:brow + A_ROWS, 2 * T - nk:2 * T - nk + nkb]
                if fixed_shift:
                    mrow = mfix_ref[j, rows, :]
                    p = jnp.exp2(s - jnp.concatenate([mrow] * (nkb // LANES), axis=1)).astype(bf16)
                    acc_ref[j, rows, :] += _dot(p, va)
                else:
                    m_old = m_ref[j, rows, :]
                    m_new = jnp.maximum(m_old, jnp.max(s, axis=1, keepdims=True))
                    p = jnp.exp2(s - m_new).astype(bf16)
                    acc_ref[j, rows, :] = jnp.exp2(m_old - m_new) * acc_ref[j, rows, :] + _dot(p, va)
                    m_ref[j, rows, :] = m_new

    def sweep(fixed_shift):
        nfar = jnp.maximum(qi - 1, 0)
        npair = nfar // 2

        def far_body(i, carry):
            step(i * (2 * T), 2 * T, False, fixed_shift)
            return carry

        lax.fori_loop(0, npair, far_body, 0)

        @pl.when(nfar % 2 == 1)
        def _():
            step(npair * (2 * T), T, False, fixed_shift)

        @pl.when(qi >= 1)
        def _():
            step((qi - 1) * T, 2 * T, True, fixed_shift)

        @pl.when(qi == 0)
        def _():
            step(0, T, True, fixed_shift)

    @pl.when(safe)
    def _():
        sweep(True)

    @pl.when(jnp.logical_not(safe))
    def _():
        m_ref[...] = jnp.full_like(m_ref, NEG_BIG)
        sweep(False)

    al = lam_ref[...]
    lam = (jnp.exp(jnp.sum(al[0:1] * al[1:2], keepdims=True))
           - jnp.exp(jnp.sum(al[2:3] * al[3:4], keepdims=True)) + lam_init)
    for j in heads:
        acc = acc_ref[j]
        o = acc[:, 0:dv] / acc[:, dv:2 * dv]
        ha = o[0:T] - lam * o[T:2 * T]
        hn = ha * lax.rsqrt(jnp.mean(ha * ha, axis=-1, keepdims=True) + RMS_EPS) * ng_ref[:, lanes_of(j)]
        o_ref[:, lanes_of(j)] = (hn * (1.0 - lam_init)).astype(bf16)


def _diffattn(tab, qn, kn, av, bias, lam_par, ng, qg, batch, seq, lam_init):
    T = A_TILE
    nq = seq // T
    wide = A_HPS * LANES
    return pl.pallas_call(
        functools.partial(_attn_kernel, lam_init=lam_init),
        grid=(batch, A_HEADS // A_HPS, nq),
        in_specs=[pl.BlockSpec(memory_space=pltpu.SMEM),
                  pl.BlockSpec((T, wide), lambda b, h, i: (b * nq + i, h)),
                  pl.BlockSpec((seq, wide), lambda b, h, i: (b, h)),
                  pl.BlockSpec((seq, wide), lambda b, h, i: (b, h)),
                  pl.BlockSpec((A_HPS, T, 2 * T), lambda b, h, i: (h, 0, 0)),
                  _const_spec(lam_par.shape),
                  pl.BlockSpec((1, wide), lambda b, h, i: (0, h)),
                  pl.BlockSpec((1, wide), lambda b, h, i: (0, h))],
        out_specs=pl.BlockSpec((T, wide), lambda b, h, i: (b * nq + i, h)),
        out_shape=jax.ShapeDtypeStruct((batch * seq, A_WIDTH), bf16),
        scratch_shapes=[pltpu.VMEM((A_HPS, seq, 2 * A_V_DIM), bf16),
                        pltpu.VMEM((A_HPS, 2 * T, LANES), bf16),
                        pltpu.VMEM((A_HPS, 2 * T, 2 * A_V_DIM), f32),
                        pltpu.VMEM((A_HPS, 2 * T, 1), f32),
                        pltpu.VMEM((A_HPS, 2 * T, LANES), f32),
                        pltpu.VMEM((A_HPS, 1, 2 * LANES), f32),
                        pltpu.SMEM((1,), jnp.int32)],
        compiler_params=pltpu.CompilerParams(
            dimension_semantics=("arbitrary", "arbitrary", "arbitrary"),
            vmem_limit_bytes=VMEM_LIMIT),
        name="diffattn",
    )(tab, qn, kn, av, bias, lam_par, ng, qg)


def _ffn_kernel(x_ref, mod_ref, hm_ref, ha_ref, gm_ref, ga_ref, wm_ref, wa_ref, wo_ref,
                g_ref, wup_ref, cw_ref, cb_ref, wd_ref,
                o_ref, hv_ref, hg_ref, act_ref, stage_ref, *, tiles_per_batch):
    tm = x_ref.shape[0]
    halo = FFN_CONV - 1

    @pl.when(pl.program_id(0) % tiles_per_batch == 0)
    def _():
        hv_ref[...] = jnp.zeros_like(hv_ref)
        hg_ref[...] = jnp.zeros_like(hg_ref)

    mod = mod_ref[0]
    y = (_sigmoid(gm_ref[...].astype(f32)) * _dot(hm_ref[...], wm_ref[...])
         + _sigmoid(ga_ref[...].astype(f32)) * _dot(ha_ref[...], wa_ref[...]))
    x1 = x_ref[...] + mod[2:3] * _dot(y.astype(bf16), wo_ref[...])
    x = _to_token_strided(x1, stage_ref)
    ms = jnp.mean(x * x, axis=-1, keepdims=True)
    y = x * lax.rsqrt(ms + RMS_EPS) * g_ref[...]
    hb = (y * (1.0 + mod[4:5]) + mod[3:4]).astype(bf16)

    def conv(u, prev, w, b):
        return w[2:3] * u + w[1:2] * _shift_tokens(u, prev, 1) + w[0:1] * _shift_tokens(u, prev, 2) + b

    for j in range(N_FCHUNK):
        cols = slice(j * F_CHUNK, (j + 1) * F_CHUNK)
        gcols = slice(D_FF + j * F_CHUNK, D_FF + (j + 1) * F_CHUNK)
        uv = _dot(hb, wup_ref[:, cols])
        ug = _dot(hb, wup_ref[:, gcols])
        cv = conv(uv, hv_ref[j], cw_ref[:, cols], cb_ref[:, cols])
        cg = conv(ug, hg_ref[j], cw_ref[:, gcols], cb_ref[:, gcols])
        hv_ref[j] = uv[tm - halo * SUBLANES:, :]
        hg_ref[j] = ug[tm - halo * SUBLANES:, :]
        act_ref[:, cols] = ((cg * _sigmoid(cg)) * cv).astype(bf16)
    o_ref[...] = _from_token_strided(x + mod[5:6] * _dot(act_ref[...], wd_ref[...]), stage_ref)


def _ffn(x2, mod3, hm, ha, gm, ga, wm, wa, wo, g2, wup, cw, cb, wd, tiles_per_batch):
    n = x2.shape[0]
    tm = TM_PROJ
    tok = lambda width: pl.BlockSpec((tm, width), lambda i: (i, 0))
    return pl.pallas_call(
        functools.partial(_ffn_kernel, tiles_per_batch=tiles_per_batch),
        grid=(n // tm,),
        in_specs=[tok(D_MODEL),
                  pl.BlockSpec((1, 6, D_MODEL), lambda i: (i // tiles_per_batch, 0, 0)),
                  tok(M_WIDTH), tok(A_WIDTH), tok(D_MODEL), tok(D_MODEL),
                  _const_spec(wm.shape), _const_spec(wa.shape), _const_spec(wo.shape),
                  _const_spec(g2.shape), _const_spec(wup.shape), _const_spec(cw.shape),
                  _const_spec(cb.shape), _const_spec(wd.shape)],
        out_specs=tok(D_MODEL),
        out_shape=jax.ShapeDtypeStruct((n, D_MODEL), f32),
        scratch_shapes=[pltpu.VMEM((N_FCHUNK, (FFN_CONV - 1) * SUBLANES, F_CHUNK), f32),
                        pltpu.VMEM((N_FCHUNK, (FFN_CONV - 1) * SUBLANES, F_CHUNK), f32),
                        pltpu.VMEM((tm, D_FF), bf16),
                        pltpu.VMEM((D_MODEL // LANES, tm + SUBLANES * SUBLANES, LANES), f32)],
        compiler_params=pltpu.CompilerParams(dimension_semantics=("arbitrary",),
                                             vmem_limit_bytes=VMEM_LIMIT),
        name="ffn",
    )(x2, mod3, hm, ha, gm, ga, wm, wa, wo, g2, wup, cw, cb, wd)


def _pad_cols(a, width):
    return jnp.pad(a, ((0, 0), (0, width - a.shape[1])))


def _layer(x2, c8, batch, seq, layer, w_ada, b_ada, norm1_g, w_in, m_conv_w, m_conv_b, m_igate_b,
           m_fgate_b, m_norm_g, a_qnorm_g, a_knorm_g, a_lambda, a_norm_g, tab, bias, w_branch_m,
           w_branch_a, w_out, norm2_g, w_up, ffn_conv_w, ffn_conv_b, w_down):
    tiles_per_batch = seq // TM_PROJ
    mod3 = _adaln(c8, w_ada, b_ada.reshape(1, -1))[:batch].reshape(batch, 6, D_MODEL)

    o = 0
    parts = {}
    w_in = w_in.astype(bf16)
    for name, size in (("mqk", 2 * M_WIDTH), ("mv", M_WIDTH), ("mo", M_WIDTH), ("mi", M_HEADS),
                       ("mf", M_HEADS), ("aq", A_WIDTH), ("ak", A_WIDTH), ("av", A_WIDTH),
                       ("gm", D_MODEL), ("ga", D_MODEL)):
        parts[name] = w_in[:, o:o + size]
        o += size

    def per_head(w):
        return w.reshape(D_MODEL, 2, A_HEADS, A_QK_DIM).transpose(0, 2, 1, 3).reshape(D_MODEL, A_WIDTH)

    w_cat = jnp.concatenate(
        [parts["mqk"], parts["mv"], parts["mo"], per_head(parts["aq"]), per_head(parts["ak"]),
         parts["av"], parts["gm"], parts["ga"], _pad_cols(parts["mi"], LANES),
         _pad_cols(parts["mf"], LANES)], axis=1)
    gid = jnp.arange(2 * LANES) // A_QK_DIM
    grp = jnp.where(gid[:, None] == gid[None, :], 1.0 / A_QK_DIM, 0.0).astype(bf16)
    qg = (jnp.tile(a_qnorm_g, A_WIDTH // A_QK_DIM) * (A_QK_DIM ** -0.5 * LOG2E)).reshape(1, A_WIDTH)
    kg = jnp.tile(a_knorm_g, A_WIDTH // A_QK_DIM).reshape(1, A_WIDTH)

    mqk, mv, mo, qn, kn, av, gm, ga, gates = _inproj(
        x2, mod3, norm1_g.reshape(1, -1), w_cat, grp, qg, kg, tiles_per_batch)

    hm = _mlstm(mqk, mv, mo, gates, m_conv_w, m_conv_b.reshape(1, -1),
                _pad_cols(m_igate_b.reshape(1, -1), LANES), _pad_cols(m_fgate_b.reshape(1, -1), LANES),
                m_norm_g.reshape(1, -1), batch, seq)

    lam_init = 0.8 - 0.6 * math.exp(-0.3 * layer)
    ha = _diffattn(tab, qn, kn, av, bias, a_lambda, a_norm_g.reshape(1, -1), qg, batch, seq, lam_init)

    return _ffn(x2, mod3, hm, ha, gm, ga, w_branch_m.astype(bf16), w_branch_a.astype(bf16), w_out.astype(bf16),
                norm2_g.reshape(1, -1), w_up.astype(bf16), ffn_conv_w, ffn_conv_b.reshape(1, -1),
                w_down.astype(bf16), tiles_per_batch)


def kernel(x, c, w_ada, b_ada, norm1_g, w_in, m_conv_w, m_conv_b, m_igate_b, m_fgate_b, m_norm_g,
           a_qnorm_g, a_knorm_g, a_lambda, a_norm_g, rel_bias, w_branch_m, w_branch_a, w_out, norm2_g,
           w_up, ffn_conv_w, ffn_conv_b, w_down):
    batch, seq, _ = x.shape
    depth = w_ada.shape[0]
    x2 = x.reshape(batch * seq, D_MODEL)
    c8 = jnp.pad(c, ((0, SUBLANES - batch), (0, 0)))
    tab = rel_bias.astype(f32).T
    bias = _bias_tiles(tab)
    for l in range(depth):
        x2 = _layer(x2, c8, batch, seq, l, w_ada[l], b_ada[l], norm1_g[l], w_in[l], m_conv_w[l],
                    m_conv_b[l], m_igate_b[l], m_fgate_b[l], m_norm_g[l], a_qnorm_g[l], a_knorm_g[l],
                    a_lambda[l], a_norm_g[l], tab, bias, w_branch_m[l], w_branch_a[l], w_out[l], norm2_g[l],
                    w_up[l], ffn_conv_w[l], ffn_conv_b[l], w_down[l])
    return x2.reshape(batch, seq, D_MODEL)
```

```python
import functools
import math

import jax
import jax.numpy as jnp
from jax import lax
from jax.experimental import pallas as pl
from jax.experimental.pallas import tpu as pltpu

D_MODEL = 1024
M_HEADS = 4
M_HEAD_DIM = 128
M_WIDTH = M_HEADS * M_HEAD_DIM
M_CONV = 4
A_HEADS = 4
A_QK_DIM = 64
A_V_DIM = 2 * A_QK_DIM
A_WIDTH = A_HEADS * A_V_DIM
N_BUCKETS = 32
MAX_DISTANCE = 128
D_FF = 2816
FFN_CONV = 3
RMS_EPS = 1e-6
LOG2E = math.log2(math.e)
NEG_BIG = -1e30

LANES = 128
SUBLANES = 8
VMEM_LIMIT = 56 * 1024 * 1024

TM_PROJ = 512
M_CHUNK = 256
A_TILE = 512
A_ROWS = 256
A_HPS = 2
A_NORM_ROWS = 1024
A_SAFE_RANGE = 90.0
F_CHUNK = 256
N_FCHUNK = D_FF // F_CHUNK

bf16 = jnp.bfloat16
f32 = jnp.float32


def _dot(a, b):
    return jnp.dot(a, b, preferred_element_type=f32)


def _dot_nt(a, b):
    return lax.dot_general(a, b, (((1,), (1,)), ((), ())), preferred_element_type=f32)


def _sigmoid(x):
    return 1.0 / (1.0 + jnp.exp(-x))


def _const_spec(shape):
    nd = len(shape)
    return pl.BlockSpec(shape, lambda *_: (0,) * nd)


def _shift_rows(u, prev8, k):
    r = pltpu.roll(u, k, 0)
    rp = pltpu.roll(prev8, k, 0)
    row = lax.broadcasted_iota(jnp.int32, (SUBLANES, u.shape[1]), 0)
    first = jnp.where(row < k, rp, r[:SUBLANES])
    return jnp.concatenate([first, r[SUBLANES:]], axis=0)


def _to_token_strided(x, stage_ref):
    ng = x.shape[0] // SUBLANES
    pitch = ng + SUBLANES
    nl = x.shape[1] // LANES
    for c in range(nl):
        for sgm in range(SUBLANES):
            stage_ref[c, sgm * pitch:sgm * pitch + ng, :] = x[sgm * ng:(sgm + 1) * ng, c * LANES:(c + 1) * LANES]
    return jnp.concatenate(
        [jnp.concatenate([stage_ref[c, pl.ds(v, SUBLANES, stride=pitch), :] for c in range(nl)], axis=1)
         for v in range(ng)], axis=0)


def _from_token_strided(xs, stage_ref):
    ng = xs.shape[0] // SUBLANES
    pitch = ng + SUBLANES
    nl = xs.shape[1] // LANES
    for c in range(nl):
        for v in range(ng):
            stage_ref[c, pl.ds(v, SUBLANES, stride=pitch), :] = xs[v * SUBLANES:(v + 1) * SUBLANES,
                                                                   c * LANES:(c + 1) * LANES]
    return jnp.concatenate(
        [jnp.concatenate([stage_ref[c, sgm * pitch:sgm * pitch + ng, :] for c in range(nl)], axis=1)
         for sgm in range(SUBLANES)], axis=0)


def _shift_tokens(u, prev, k):
    rows = u.shape[0]
    ng = rows // SUBLANES
    halo = prev.shape[0] // SUBLANES
    first = lax.broadcasted_iota(jnp.int32, (SUBLANES, u.shape[1]), 0) == 0

    def wrapped(g):
        cur = u[g * SUBLANES:(g + 1) * SUBLANES, :]
        old = prev[(g - (ng - halo)) * SUBLANES:(g - (ng - halo) + 1) * SUBLANES, :]
        return jnp.where(first, pltpu.roll(old, 1, 0), pltpu.roll(cur, 1, 0))

    return jnp.concatenate([wrapped(ng - k + v) for v in range(k)] + [u[:rows - k * SUBLANES, :]], axis=0)


def _adaln_kernel(c_ref, w_ref, b_ref, o_ref):
    c = c_ref[...]
    a = (c * _sigmoid(c)).astype(bf16)
    o_ref[...] = _dot(a, w_ref[...].astype(bf16)) + b_ref[...]


def _adaln(c8, w, b):
    n = w.shape[1]
    tn = 1536
    return pl.pallas_call(
        _adaln_kernel,
        grid=(n // tn,),
        in_specs=[_const_spec(c8.shape),
                  pl.BlockSpec((D_MODEL, tn), lambda j: (0, j)),
                  pl.BlockSpec((1, tn), lambda j: (0, j))],
        out_specs=pl.BlockSpec((c8.shape[0], tn), lambda j: (0, j)),
        out_shape=jax.ShapeDtypeStruct((c8.shape[0], n), f32),
        compiler_params=pltpu.CompilerParams(dimension_semantics=("arbitrary",),
                                             vmem_limit_bytes=VMEM_LIMIT),
        name="adaln",
    )(c8, w, b)


def _bias_kernel(tab_ref, o_ref):
    h = pl.program_id(0)
    t = pl.program_id(1)
    blk = MAX_DISTANCE
    nblk = A_TILE // blk
    far = tab_ref[h, N_BUCKETS - 1]

    def block(r0, c0, shift):
        row = lax.broadcasted_iota(jnp.int32, (blk, blk), 0) + r0
        col = lax.broadcasted_iota(jnp.int32, (blk, blk), 1) + c0
        dist = row - col + shift
        n = jnp.maximum(dist, 0)
        max_exact = N_BUCKETS // 2
        nf = jnp.maximum(n, 1).astype(f32)
        large = max_exact + (jnp.log(nf / max_exact) / math.log(MAX_DISTANCE / max_exact)
                             * (N_BUCKETS - max_exact)).astype(jnp.int32)
        large = jnp.minimum(large, N_BUCKETS - 1)
        bucket = jnp.where(n < max_exact, n, large)
        val = jnp.zeros((blk, blk), f32)
        for b in range(N_BUCKETS - 1):
            val = jnp.where(bucket == b, tab_ref[h, b] - far, val)
        return jnp.where(dist >= 0, val * LOG2E, NEG_BIG)

    @pl.when(t == 0)
    def _():
        o_ref[0] = jnp.zeros((A_TILE, A_TILE), f32)
        o_ref[0, 0:blk, A_TILE - blk:A_TILE] = block(0, A_TILE - blk, A_TILE)

    @pl.when(t == 1)
    def _():
        for i in range(nblk):
            rows = slice(i * blk, (i + 1) * blk)
            if i >= 2:
                o_ref[0, rows, 0:(i - 1) * blk] = jnp.zeros((blk, (i - 1) * blk), f32)
            if i >= 1:
                o_ref[0, rows, (i - 1) * blk:i * blk] = block(i * blk, (i - 1) * blk, 0)
            o_ref[0, rows, i * blk:(i + 1) * blk] = block(i * blk, i * blk, 0)
            if i < nblk - 1:
                o_ref[0, rows, (i + 1) * blk:A_TILE] = jnp.full((blk, A_TILE - (i + 1) * blk), NEG_BIG, f32)


def _bias_tiles(tab):
    return pl.pallas_call(
        _bias_kernel,
        grid=(A_HEADS, 2),
        in_specs=[pl.BlockSpec(memory_space=pltpu.SMEM)],
        out_specs=pl.BlockSpec((1, A_TILE, A_TILE), lambda h, t: (h, 0, t)),
        out_shape=jax.ShapeDtypeStruct((A_HEADS, A_TILE, 2 * A_TILE), f32),
        compiler_params=pltpu.CompilerParams(dimension_semantics=("arbitrary", "arbitrary"),
                                             vmem_limit_bytes=VMEM_LIMIT),
        name="bias_tiles",
    )(tab)


C_MQK, C_MV, C_MO, C_AQ, C_AK, C_AV, C_GM, C_GA, C_GATE, C_END = (
    0, 1024, 1536, 2048, 2560, 3072, 3584, 4608, 5632, 5888)


def _inproj_kernel(x_ref, mod_ref, g_ref, w_ref, grp_ref, qg_ref, kg_ref,
                   mqk_ref, mv_ref, mo_ref, qn_ref, kn_ref, av_ref, gm_ref, ga_ref, gate_ref):
    x = x_ref[...]
    mod = mod_ref[0]
    ms = jnp.mean(x * x, axis=-1, keepdims=True)
    y = x * lax.rsqrt(ms + RMS_EPS) * g_ref[...]
    hb = (y * (1.0 + mod[1:2]) + mod[0:1]).astype(bf16)

    def proj(c0, c1):
        return _dot(hb, w_ref[:, c0:c1])

    def qknorm(a, gain_ref):
        sq = (a * a).astype(bf16)
        gw = grp_ref.shape[0]
        msq = jnp.concatenate([_dot(sq[:, c:c + gw], grp_ref[...]) for c in range(0, a.shape[1], gw)], axis=1)
        return (a * lax.rsqrt(msq + RMS_EPS) * gain_ref[...]).astype(bf16)

    mqk_ref[:, 0:512] = proj(C_MQK, C_MQK + 512).astype(bf16)
    mqk_ref[:, 512:1024] = proj(C_MQK + 512, C_MV).astype(bf16)
    mv_ref[...] = proj(C_MV, C_MO).astype(bf16)
    mo_ref[...] = proj(C_MO, C_AQ).astype(bf16)
    qn_ref[...] = qknorm(proj(C_AQ, C_AK), qg_ref)
    kn_ref[...] = qknorm(proj(C_AK, C_AV), kg_ref)
    av_ref[...] = proj(C_AV, C_GM).astype(bf16)
    gm_ref[:, 0:512] = proj(C_GM, C_GM + 512).astype(bf16)
    gm_ref[:, 512:1024] = proj(C_GM + 512, C_GA).astype(bf16)
    ga_ref[:, 0:512] = proj(C_GA, C_GA + 512).astype(bf16)
    ga_ref[:, 512:1024] = proj(C_GA + 512, C_GATE).astype(bf16)
    gate_ref[...] = proj(C_GATE, C_END)


def _inproj(x2, mod3, g1, w, grp, qg, kg, tiles_per_batch):
    n = x2.shape[0]
    tm = TM_PROJ
    tok = lambda width: pl.BlockSpec((tm, width), lambda i: (i, 0))
    outs = [(1024, bf16), (512, bf16), (512, bf16), (512, bf16), (512, bf16), (512, bf16),
            (1024, bf16), (1024, bf16), (2 * LANES, f32)]
    return pl.pallas_call(
        _inproj_kernel,
        grid=(n // tm,),
        in_specs=[tok(D_MODEL),
                  pl.BlockSpec((1, 6, D_MODEL), lambda i: (i // tiles_per_batch, 0, 0)),
                  _const_spec(g1.shape), _const_spec(w.shape), _const_spec(grp.shape),
                  _const_spec(qg.shape), _const_spec(kg.shape)],
        out_specs=[tok(wd) for wd, _ in outs],
        out_shape=[jax.ShapeDtypeStruct((n, wd), dt) for wd, dt in outs],
        compiler_params=pltpu.CompilerParams(dimension_semantics=("arbitrary",),
                                             vmem_limit_bytes=VMEM_LIMIT),
        name="inproj",
    )(x2, mod3, g1, w, grp, qg, kg)


def _mlstm_kernel(mqk_ref, mv_ref, mo_ref, gate_ref, cw_ref, cb_ref, bi_ref, bfg_ref, mg_ref,
                  o_ref, c_st, m_st, tail_ref):
    nb = mqk_ref.shape[0]
    L = M_CHUNK
    d = M_HEAD_DIM

    @pl.when(pl.program_id(0) == 0)
    def _():
        c_st[...] = jnp.zeros_like(c_st)
        m_st[...] = jnp.zeros_like(m_st)
        tail_ref[...] = jnp.zeros_like(tail_ref)

    row = lax.broadcasted_iota(jnp.int32, (L, L), 0)
    col = lax.broadcasted_iota(jnp.int32, (L, L), 1)
    causal = row >= col
    tri = jnp.where(causal, 1.0, 0.0).astype(bf16)
    ones = jnp.ones((L, d), bf16)
    cw = cw_ref[...]

    for b in range(nb):
        x = mqk_ref[b].astype(f32)
        prev8 = tail_ref[b]
        y = (cw[3:4] * x + cw[2:3] * _shift_rows(x, prev8, 1) + cw[1:2] * _shift_rows(x, prev8, 2)
             + cw[0:1] * _shift_rows(x, prev8, 3) + cb_ref[...])
        tail_ref[b] = x[L - SUBLANES:, :]
        qk = y * _sigmoid(y)

        gi = gate_ref[b, :, 0:LANES] + bi_ref[...]
        gf = gate_ref[b, :, LANES:2 * LANES] + bfg_ref[...]
        logf = jnp.minimum(gf, 0.0) - jnp.log(1.0 + jnp.exp(-jnp.abs(gf)))
        logf_hi = logf.astype(bf16)
        logf_lo = (logf - logf_hi.astype(f32)).astype(bf16)
        bcum = _dot(tri, logf_hi) + _dot(tri, logf_lo)
        r = gi - bcum
        r_t = r.T
        b_last = bcum[L - 1:L, :]
        g = b_last + r
        m_loc = jnp.max(g, axis=0, keepdims=True)

        for h in range(M_HEADS):
            st = b * M_HEADS + h
            hs = slice(h * d, (h + 1) * d)
            qh = qk[:, hs].astype(bf16)
            kh = qk[:, M_WIDTH + h * d:M_WIDTH + (h + 1) * d] * (d ** -0.5)
            vaug = jnp.concatenate([mv_ref[b, :, hs], ones], axis=1)
            m_in = m_st[st][0:1, 0:1]
            b_col = bcum[:, h:h + 1]
            dmat = jnp.where(causal, b_col + r_t[h:h + 1, :], NEG_BIG)
            a_t = b_col + m_in
            m_t = jnp.maximum(a_t, jnp.max(dmat, axis=1, keepdims=True))
            inter_w = jnp.exp(a_t - m_t)
            s = _dot_nt(qh, kh.astype(bf16)) * jnp.exp(dmat - m_t)
            tot = _dot(s.astype(bf16), vaug) + inter_w * _dot(qh, c_st[st].astype(bf16))
            den = jnp.maximum(jnp.abs(tot[:, d:]), jnp.exp(-m_t))
            hh = tot[:, :d] / den

            bl = b_last[:, h:h + 1]
            m_new = jnp.maximum(bl + m_in, m_loc[:, h:h + 1])
            w_col = jnp.exp(g[:, h:h + 1] - m_new)
            kw_t = (kh * w_col).T.astype(bf16)
            c_st[st] = jnp.exp(bl + m_in - m_new) * c_st[st] + _dot(kw_t, vaug)
            m_st[st] = jnp.broadcast_to(m_new, (SUBLANES, LANES))

            hn = hh * lax.rsqrt(jnp.mean(hh * hh, axis=-1, keepdims=True) + RMS_EPS) * mg_ref[:, hs]
            o_ref[b, :, hs] = (_sigmoid(mo_ref[b, :, hs].astype(f32)) * hn).astype(bf16)


def _mlstm(mqk, mv, mo, gates, cw, cb, bi, bfg, mg, batch, seq):
    L = M_CHUNK
    tok = lambda width: pl.BlockSpec((batch, L, width), lambda c: (0, c, 0))
    per_batch = lambda a: a.reshape(batch, seq, a.shape[-1])
    out = pl.pallas_call(
        _mlstm_kernel,
        grid=(seq // L,),
        in_specs=[tok(2 * M_WIDTH), tok(M_WIDTH), tok(M_WIDTH), tok(2 * LANES),
                  _const_spec(cw.shape), _const_spec(cb.shape), _const_spec(bi.shape),
                  _const_spec(bfg.shape), _const_spec(mg.shape)],
        out_specs=tok(M_WIDTH),
        out_shape=jax.ShapeDtypeStruct((batch, seq, M_WIDTH), bf16),
        scratch_shapes=[pltpu.VMEM((batch * M_HEADS, M_HEAD_DIM, 2 * M_HEAD_DIM), f32),
                        pltpu.VMEM((batch * M_HEADS, SUBLANES, LANES), f32),
                        pltpu.VMEM((batch, SUBLANES, 2 * M_WIDTH), f32)],
        compiler_params=pltpu.CompilerParams(dimension_semantics=("arbitrary",),
                                             vmem_limit_bytes=VMEM_LIMIT),
        name="mlstm",
    )(per_batch(mqk), per_batch(mv), per_batch(mo), per_batch(gates), cw, cb, bi, bfg, mg)
    return out.reshape(batch * seq, M_WIDTH)


def _attn_kernel(tab_ref, q_ref, k_ref, v_ref, bias_ref, lam_ref, ng_ref, qg_ref, o_ref,
                 vaug_ref, q2_ref, acc_ref, m_ref, mfix_ref, kmax_ref, safe_ref, *, lam_init):
    T = A_TILE
    dv = A_V_DIM
    hg = pl.program_id(1)
    qi = pl.program_id(2)
    heads = range(A_HPS)
    sel = jnp.where((lax.broadcasted_iota(jnp.int32, (LANES, 2 * LANES), 0) < A_QK_DIM)
                    == (lax.broadcasted_iota(jnp.int32, (LANES, 2 * LANES), 1) < LANES), 1.0, 0.0).astype(bf16)

    def sq_norms(x):
        xf = x.astype(f32)
        return _dot((xf * xf).astype(bf16), sel)

    def lanes_of(j):
        return slice(j * LANES, (j + 1) * LANES)

    bmax, bmin = [], []
    for j in heads:
        hi = jnp.float32(0.0)
        lo = jnp.float32(0.0)
        for b in range(N_BUCKETS - 1):
            rel = (tab_ref[hg * A_HPS + j, b] - tab_ref[hg * A_HPS + j, N_BUCKETS - 1]) * LOG2E
            hi = jnp.maximum(hi, rel)
            lo = jnp.minimum(lo, rel)
        bmax.append(hi)
        bmin.append(lo)

    @pl.when(qi == 0)
    def _():
        ok = None
        for j in heads:
            vaug_ref[j, :, 0:dv] = v_ref[:, lanes_of(j)]
            vaug_ref[j, :, dv:2 * dv] = jnp.ones((v_ref.shape[0], dv), bf16)
            kmax = jnp.zeros((1, 2 * LANES), f32)
            for c in range(k_ref.shape[0] // A_NORM_ROWS):
                kn2 = sq_norms(k_ref[c * A_NORM_ROWS:(c + 1) * A_NORM_ROWS, lanes_of(j)])
                kmax = jnp.maximum(kmax, jnp.max(kn2, axis=0, keepdims=True))
            kmax_ref[j] = kmax
            qmax = math.sqrt(A_QK_DIM) * jnp.max(jnp.abs(qg_ref[:, lanes_of(j)]))
            span = 2.0 * qmax * jnp.sqrt(jnp.max(kmax)) + (bmax[j] - bmin[j])
            ok_j = span < A_SAFE_RANGE
            ok = ok_j if ok is None else jnp.logical_and(ok, ok_j)
        safe_ref[0] = ok.astype(jnp.int32)

    for j in heads:
        q = q_ref[:, lanes_of(j)]
        lane = lax.broadcasted_iota(jnp.int32, q.shape, 1)
        zero = jnp.zeros_like(q)
        q2_ref[j, 0:T, :] = jnp.where(lane < A_QK_DIM, q, zero)
        q2_ref[j, T:2 * T, :] = jnp.where(lane >= A_QK_DIM, q, zero)
        bound = jnp.sqrt(sq_norms(q) * kmax_ref[j])
        mfix_ref[j, 0:T, :] = bound[:, 0:LANES] + bmax[j]
        mfix_ref[j, T:2 * T, :] = bound[:, LANES:2 * LANES] + bmax[j]
    acc_ref[...] = jnp.zeros_like(acc_ref)
    safe = safe_ref[0] == 1

    def step(start, nk, near, fixed_shift):
        start = pl.multiple_of(start, T)
        for rb in range(2 * T // A_ROWS):
            rows = slice(rb * A_ROWS, (rb + 1) * A_ROWS)
            brow = (rb * A_ROWS) % T
            nkb = nk - (T - brow - A_ROWS) if near else nk
            for j in heads:
                kt = k_ref[pl.ds(start, nkb), lanes_of(j)]
                va = vaug_ref[j, pl.ds(start, nkb), :]
                s = _dot_nt(q2_ref[j, rows, :], kt)
                if near:
                    s = s + bias_ref[j, brow:brow + A_ROWS, 2 * T - nk:2 * T - nk + nkb]
                if fixed_shift:
                    mrow = mfix_ref[j, rows, :]
                    p = jnp.exp2(s - jnp.concatenate([mrow] * (nkb // LANES), axis=1)).astype(bf16)
                    acc_ref[j, rows, :] += _dot(p, va)
                else:
                    m_old = m_ref[j, rows, :]
                    m_new = jnp.maximum(m_old, jnp.max(s, axis=1, keepdims=True))
                    p = jnp.exp2(s - m_new).astype(bf16)
                    acc_ref[j, rows, :] = jnp.exp2(m_old - m_new) * acc_ref[j, rows, :] + _dot(p, va)
                    m_ref[j, rows, :] = m_new

    def sweep(fixed_shift):
        nfar = jnp.maximum(qi - 1, 0)
        nquad = nfar // 4

        def far_body(i, carry):
            step(i * (4 * T), 2 * T, False, fixed_shift)
            step(i * (4 * T) + 2 * T, 2 * T, False, fixed_shift)
            return carry

        lax.fori_loop(0, nquad, far_body, 0)

        @pl.when(nfar % 4 >= 2)
        def _():
            step(nquad * (4 * T), 2 * T, False, fixed_shift)

        @pl.when(nfar % 2 == 1)
        def _():
            step((nfar - 1) * T, T, False, fixed_shift)

        @pl.when(qi >= 1)
        def _():
            step((qi - 1) * T, 2 * T, True, fixed_shift)

        @pl.when(qi == 0)
        def _():
            step(0, T, True, fixed_shift)

    @pl.when(safe)
    def _():
        sweep(True)

    @pl.when(jnp.logical_not(safe))
    def _():
        m_ref[...] = jnp.full_like(m_ref, NEG_BIG)
        sweep(False)

    al = lam_ref[...]
    lam = (jnp.exp(jnp.sum(al[0:1] * al[1:2], keepdims=True))
           - jnp.exp(jnp.sum(al[2:3] * al[3:4], keepdims=True)) + lam_init)
    for j in heads:
        acc = acc_ref[j]
        o = acc[:, 0:dv] / acc[:, dv:2 * dv]
        ha = o[0:T] - lam * o[T:2 * T]
        hn = ha * lax.rsqrt(jnp.mean(ha * ha, axis=-1, keepdims=True) + RMS_EPS) * ng_ref[:, lanes_of(j)]
        o_ref[:, lanes_of(j)] = (hn * (1.0 - lam_init)).astype(bf16)


def _diffattn(tab, qn, kn, av, bias, lam_par, ng, qg, batch, seq, lam_init):
    T = A_TILE
    nq = seq // T
    wide = A_HPS * LANES
    return pl.pallas_call(
        functools.partial(_attn_kernel, lam_init=lam_init),
        grid=(batch, A_HEADS // A_HPS, nq),
        in_specs=[pl.BlockSpec(memory_space=pltpu.SMEM),
                  pl.BlockSpec((T, wide), lambda b, h, i: (b * nq + i, h)),
                  pl.BlockSpec((seq, wide), lambda b, h, i: (b, h)),
                  pl.BlockSpec((seq, wide), lambda b, h, i: (b, h)),
                  pl.BlockSpec((A_HPS, T, 2 * T), lambda b, h, i: (h, 0, 0)),
                  _const_spec(lam_par.shape),
                  pl.BlockSpec((1, wide), lambda b, h, i: (0, h)),
                  pl.BlockSpec((1, wide), lambda b, h, i: (0, h))],
        out_specs=pl.BlockSpec((T, wide), lambda b, h, i: (b * nq + i, h)),
        out_shape=jax.ShapeDtypeStruct((batch * seq, A_WIDTH), bf16),
        scratch_shapes=[pltpu.VMEM((A_HPS, seq, 2 * A_V_DIM), bf16),
                        pltpu.VMEM((A_HPS, 2 * T, LANES), bf16),
                        pltpu.VMEM((A_HPS, 2 * T, 2 * A_V_DIM), f32),
                        pltpu.VMEM((A_HPS, 2 * T, 1), f32),
                        pltpu.VMEM((A_HPS, 2 * T, LANES), f32),
                        pltpu.VMEM((A_HPS, 1, 2 * LANES), f32),
                        pltpu.SMEM((1,), jnp.int32)],
        compiler_params=pltpu.CompilerParams(
            dimension_semantics=("arbitrary", "arbitrary", "arbitrary"),
            vmem_limit_bytes=VMEM_LIMIT),
        name="diffattn",
    )(tab, qn, kn, av, bias, lam_par, ng, qg)


def _ffn_kernel(x_ref, mod_ref, hm_ref, ha_ref, gm_ref, ga_ref, wm_ref, wa_ref, wo_ref,
                g_ref, wup_ref, cw_ref, cb_ref, wd_ref,
                o_ref, hv_ref, hg_ref, act_ref, stage_ref, *, tiles_per_batch):
    tm = x_ref.shape[0]
    halo = FFN_CONV - 1

    @pl.when(pl.program_id(0) % tiles_per_batch == 0)
    def _():
        hv_ref[...] = jnp.zeros_like(hv_ref)
        hg_ref[...] = jnp.zeros_like(hg_ref)

    mod = mod_ref[0]
    y = (_sigmoid(gm_ref[...].astype(f32)) * _dot(hm_ref[...], wm_ref[...])
         + _sigmoid(ga_ref[...].astype(f32)) * _dot(ha_ref[...], wa_ref[...]))
    x1 = x_ref[...] + mod[2:3] * _dot(y.astype(bf16), wo_ref[...])
    x = _to_token_strided(x1, stage_ref)
    ms = jnp.mean(x * x, axis=-1, keepdims=True)
    y = x * lax.rsqrt(ms + RMS_EPS) * g_ref[...]
    hb = (y * (1.0 + mod[4:5]) + mod[3:4]).astype(bf16)

    def conv(u, prev, w, b):
        return w[2:3] * u + w[1:2] * _shift_tokens(u, prev, 1) + w[0:1] * _shift_tokens(u, prev, 2) + b

    for j in range(N_FCHUNK):
        cols = slice(j * F_CHUNK, (j + 1) * F_CHUNK)
        gcols = slice(D_FF + j * F_CHUNK, D_FF + (j + 1) * F_CHUNK)
        uv = _dot(hb, wup_ref[:, cols])
        ug = _dot(hb, wup_ref[:, gcols])
        cv = conv(uv, hv_ref[j], cw_ref[:, cols], cb_ref[:, cols])
        cg = conv(ug, hg_ref[j], cw_ref[:, gcols], cb_ref[:, gcols])
        hv_ref[j] = uv[tm - halo * SUBLANES:, :]
        hg_ref[j] = ug[tm - halo * SUBLANES:, :]
        act_ref[:, cols] = ((cg * _sigmoid(cg)) * cv).astype(bf16)
    o_ref[...] = _from_token_strided(x + mod[5:6] * _dot(act_ref[...], wd_ref[...]), stage_ref)


def _ffn(x2, mod3, hm, ha, gm, ga, wm, wa, wo, g2, wup, cw, cb, wd, tiles_per_batch):
    n = x2.shape[0]
    tm = TM_PROJ
    tok = lambda width: pl.BlockSpec((tm, width), lambda i: (i, 0))
    return pl.pallas_call(
        functools.partial(_ffn_kernel, tiles_per_batch=tiles_per_batch),
        grid=(n // tm,),
        in_specs=[tok(D_MODEL),
                  pl.BlockSpec((1, 6, D_MODEL), lambda i: (i // tiles_per_batch, 0, 0)),
                  tok(M_WIDTH), tok(A_WIDTH), tok(D_MODEL), tok(D_MODEL),
                  _const_spec(wm.shape), _const_spec(wa.shape), _const_spec(wo.shape),
                  _const_spec(g2.shape), _const_spec(wup.shape), _const_spec(cw.shape),
                  _const_spec(cb.shape), _const_spec(wd.shape)],
        out_specs=tok(D_MODEL),
        out_shape=jax.ShapeDtypeStruct((n, D_MODEL), f32),
        scratch_shapes=[pltpu.VMEM((N_FCHUNK, (FFN_CONV - 1) * SUBLANES, F_CHUNK), f32),
                        pltpu.VMEM((N_FCHUNK, (FFN_CONV - 1) * SUBLANES, F_CHUNK), f32),
                        pltpu.VMEM((tm, D_FF), bf16),
                        pltpu.VMEM((D_MODEL // LANES, tm + SUBLANES * SUBLANES, LANES), f32)],
        compiler_params=pltpu.CompilerParams(dimension_semantics=("arbitrary",),
                                             vmem_limit_bytes=VMEM_LIMIT),
        name="ffn",
    )(x2, mod3, hm, ha, gm, ga, wm, wa, wo, g2, wup, cw, cb, wd)


def _pad_cols(a, width):
    return jnp.pad(a, ((0, 0), (0, width - a.shape[1])))


def _layer(x2, c8, batch, seq, layer, w_ada, b_ada, norm1_g, w_in, m_conv_w, m_conv_b, m_igate_b,
           m_fgate_b, m_norm_g, a_qnorm_g, a_knorm_g, a_lambda, a_norm_g, tab, bias, w_branch_m,
           w_branch_a, w_out, norm2_g, w_up, ffn_conv_w, ffn_conv_b, w_down):
    tiles_per_batch = seq // TM_PROJ
    mod3 = _adaln(c8, w_ada, b_ada.reshape(1, -1))[:batch].reshape(batch, 6, D_MODEL)

    o = 0
    parts = {}
    w_in = w_in.astype(bf16)
    for name, size in (("mqk", 2 * M_WIDTH), ("mv", M_WIDTH), ("mo", M_WIDTH), ("mi", M_HEADS),
                       ("mf", M_HEADS), ("aq", A_WIDTH), ("ak", A_WIDTH), ("av", A_WIDTH),
                       ("gm", D_MODEL), ("ga", D_MODEL)):
        parts[name] = w_in[:, o:o + size]
        o += size

    def per_head(w):
        return w.reshape(D_MODEL, 2, A_HEADS, A_QK_DIM).transpose(0, 2, 1, 3).reshape(D_MODEL, A_WIDTH)

    w_cat = jnp.concatenate(
        [parts["mqk"], parts["mv"], parts["mo"], per_head(parts["aq"]), per_head(parts["ak"]),
         parts["av"], parts["gm"], parts["ga"], _pad_cols(parts["mi"], LANES),
         _pad_cols(parts["mf"], LANES)], axis=1)
    gid = jnp.arange(2 * LANES) // A_QK_DIM
    grp = jnp.where(gid[:, None] == gid[None, :], 1.0 / A_QK_DIM, 0.0).astype(bf16)
    qg = (jnp.tile(a_qnorm_g, A_WIDTH // A_QK_DIM) * (A_QK_DIM ** -0.5 * LOG2E)).reshape(1, A_WIDTH)
    kg = jnp.tile(a_knorm_g, A_WIDTH // A_QK_DIM).reshape(1, A_WIDTH)

    mqk, mv, mo, qn, kn, av, gm, ga, gates = _inproj(
        x2, mod3, norm1_g.reshape(1, -1), w_cat, grp, qg, kg, tiles_per_batch)

    hm = _mlstm(mqk, mv, mo, gates, m_conv_w, m_conv_b.reshape(1, -1),
                _pad_cols(m_igate_b.reshape(1, -1), LANES), _pad_cols(m_fgate_b.reshape(1, -1), LANES),
                m_norm_g.reshape(1, -1), batch, seq)

    lam_init = 0.8 - 0.6 * math.exp(-0.3 * layer)
    ha = _diffattn(tab, qn, kn, av, bias, a_lambda, a_norm_g.reshape(1, -1), qg, batch, seq, lam_init)

    return _ffn(x2, mod3, hm, ha, gm, ga, w_branch_m.astype(bf16), w_branch_a.astype(bf16), w_out.astype(bf16),
                norm2_g.reshape(1, -1), w_up.astype(bf16), ffn_conv_w, ffn_conv_b.reshape(1, -1),
                w_down.astype(bf16), tiles_per_batch)


def kernel(x, c, w_ada, b_ada, norm1_g, w_in, m_conv_w, m_conv_b, m_igate_b, m_fgate_b, m_norm_g,
           a_qnorm_g, a_knorm_g, a_lambda, a_norm_g, rel_bias, w_branch_m, w_branch_a, w_out, norm2_g,
           w_up, ffn_conv_w, ffn_conv_b, w_down):
    batch, seq, _ = x.shape
    depth = w_ada.shape[0]
    x2 = x.reshape(batch * seq, D_MODEL)
    c8 = jnp.pad(c, ((0, SUBLANES - batch), (0, 0)))
    tab = rel_bias.astype(f32).T
    bias = _bias_tiles(tab)
    for l in range(depth):
        x2 = _layer(x2, c8, batch, seq, l, w_ada[l], b_ada[l], norm1_g[l], w_in[l], m_conv_w[l],
                    m_conv_b[l], m_igate_b[l], m_fgate_b[l], m_norm_g[l], a_qnorm_g[l], a_knorm_g[l],
                    a_lambda[l], a_norm_g[l], tab, bias, w_branch_m[l], w_branch_a[l], w_out[l], norm2_g[l],
                    w_up[l], ffn_conv_w[l], ffn_conv_b[l], w_down[l])
    return x2.reshape(batch, seq, D_MODEL)
```

```python
import functools
import math

import jax
import jax.numpy as jnp
from jax import lax
from jax.experimental import pallas as pl
from jax.experimental.pallas import tpu as pltpu

D_MODEL = 1024
M_HEADS = 4
M_HEAD_DIM = 128
M_WIDTH = M_HEADS * M_HEAD_DIM
M_CONV = 4
A_HEADS = 4
A_QK_DIM = 64
A_V_DIM = 2 * A_QK_DIM
A_WIDTH = A_HEADS * A_V_DIM
N_BUCKETS = 32
MAX_DISTANCE = 128
D_FF = 2816
FFN_CONV = 3
RMS_EPS = 1e-6
LOG2E = math.log2(math.e)
NEG_BIG = -1e30

LANES = 128
SUBLANES = 8
VMEM_LIMIT = 56 * 1024 * 1024

TM_PROJ = 512
M_CHUNK = 256
A_TILE = 512
A_ROWS = 256
A_HPS = 2
A_NORM_ROWS = 1024
A_SAFE_RANGE = 90.0
F_CHUNK = 256
N_FCHUNK = D_FF // F_CHUNK

bf16 = jnp.bfloat16
f32 = jnp.float32


def _dot(a, b):
    return jnp.dot(a, b, preferred_element_type=f32)


def _dot_nt(a, b):
    return lax.dot_general(a, b, (((1,), (1,)), ((), ())), preferred_element_type=f32)


def _sigmoid(x):
    return 1.0 / (1.0 + jnp.exp(-x))


def _const_spec(shape):
    nd = len(shape)
    return pl.BlockSpec(shape, lambda *_: (0,) * nd)


def _shift_rows(u, prev8, k):
    r = pltpu.roll(u, k, 0)
    rp = pltpu.roll(prev8, k, 0)
    row = lax.broadcasted_iota(jnp.int32, (SUBLANES, u.shape[1]), 0)
    first = jnp.where(row < k, rp, r[:SUBLANES])
    return jnp.concatenate([first, r[SUBLANES:]], axis=0)


def _to_token_strided(x, stage_ref):
    ng = x.shape[0] // SUBLANES
    pitch = ng + SUBLANES
    nl = x.shape[1] // LANES
    for c in range(nl):
        for sgm in range(SUBLANES):
            stage_ref[c, sgm * pitch:sgm * pitch + ng, :] = x[sgm * ng:(sgm + 1) * ng, c * LANES:(c + 1) * LANES]
    return jnp.concatenate(
        [jnp.concatenate([stage_ref[c, pl.ds(v, SUBLANES, stride=pitch), :] for c in range(nl)], axis=1)
         for v in range(ng)], axis=0)


def _from_token_strided(xs, stage_ref):
    ng = xs.shape[0] // SUBLANES
    pitch = ng + SUBLANES
    nl = xs.shape[1] // LANES
    for c in range(nl):
        for v in range(ng):
            stage_ref[c, pl.ds(v, SUBLANES, stride=pitch), :] = xs[v * SUBLANES:(v + 1) * SUBLANES,
                                                                   c * LANES:(c + 1) * LANES]
    return jnp.concatenate(
        [jnp.concatenate([stage_ref[c, sgm * pitch:sgm * pitch + ng, :] for c in range(nl)], axis=1)
         for sgm in range(SUBLANES)], axis=0)


def _shift_tokens(u, prev, k):
    rows = u.shape[0]
    ng = rows // SUBLANES
    halo = prev.shape[0] // SUBLANES
    first = lax.broadcasted_iota(jnp.int32, (SUBLANES, u.shape[1]), 0) == 0

    def wrapped(g):
        cur = u[g * SUBLANES:(g + 1) * SUBLANES, :]
        old = prev[(g - (ng - halo)) * SUBLANES:(g - (ng - halo) + 1) * SUBLANES, :]
        return jnp.where(first, pltpu.roll(old, 1, 0), pltpu.roll(cur, 1, 0))

    return jnp.concatenate([wrapped(ng - k + v) for v in range(k)] + [u[:rows - k * SUBLANES, :]], axis=0)


def _adaln_kernel(c_ref, w_ref, b_ref, o_ref):
    c = c_ref[...]
    a = (c * _sigmoid(c)).astype(bf16)
    o_ref[...] = _dot(a, w_ref[...].astype(bf16)) + b_ref[...]


def _adaln(c8, w, b):
    n = w.shape[1]
    tn = 1536
    return pl.pallas_call(
        _adaln_kernel,
        grid=(n // tn,),
        in_specs=[_const_spec(c8.shape),
                  pl.BlockSpec((D_MODEL, tn), lambda j: (0, j)),
                  pl.BlockSpec((1, tn), lambda j: (0, j))],
        out_specs=pl.BlockSpec((c8.shape[0], tn), lambda j: (0, j)),
        out_shape=jax.ShapeDtypeStruct((c8.shape[0], n), f32),
        compiler_params=pltpu.CompilerParams(dimension_semantics=("arbitrary",),
                                             vmem_limit_bytes=VMEM_LIMIT),
        name="adaln",
    )(c8, w, b)


def _bias_kernel(tab_ref, o_ref):
    h = pl.program_id(0)
    t = pl.program_id(1)
    blk = MAX_DISTANCE
    nblk = A_TILE // blk
    far = tab_ref[h, N_BUCKETS - 1]

    def block(r0, c0, shift):
        row = lax.broadcasted_iota(jnp.int32, (blk, blk), 0) + r0
        col = lax.broadcasted_iota(jnp.int32, (blk, blk), 1) + c0
        dist = row - col + shift
        n = jnp.maximum(dist, 0)
        max_exact = N_BUCKETS // 2
        nf = jnp.maximum(n, 1).astype(f32)
        large = max_exact + (jnp.log(nf / max_exact) / math.log(MAX_DISTANCE / max_exact)
                             * (N_BUCKETS - max_exact)).astype(jnp.int32)
        large = jnp.minimum(large, N_BUCKETS - 1)
        bucket = jnp.where(n < max_exact, n, large)
        val = jnp.zeros((blk, blk), f32)
        for b in range(N_BUCKETS - 1):
            val = jnp.where(bucket == b, tab_ref[h, b] - far, val)
        return jnp.where(dist >= 0, val * LOG2E, NEG_BIG)

    @pl.when(t == 0)
    def _():
        o_ref[0] = jnp.zeros((A_TILE, A_TILE), f32)
        o_ref[0, 0:blk, A_TILE - blk:A_TILE] = block(0, A_TILE - blk, A_TILE)

    @pl.when(t == 1)
    def _():
        for i in range(nblk):
            rows = slice(i * blk, (i + 1) * blk)
            if i >= 2:
                o_ref[0, rows, 0:(i - 1) * blk] = jnp.zeros((blk, (i - 1) * blk), f32)
            if i >= 1:
                o_ref[0, rows, (i - 1) * blk:i * blk] = block(i * blk, (i - 1) * blk, 0)
            o_ref[0, rows, i * blk:(i + 1) * blk] = block(i * blk, i * blk, 0)
            if i < nblk - 1:
                o_ref[0, rows, (i + 1) * blk:A_TILE] = jnp.full((blk, A_TILE - (i + 1) * blk), NEG_BIG, f32)


def _bias_tiles(tab):
    return pl.pallas_call(
        _bias_kernel,
        grid=(A_HEADS, 2),
        in_specs=[pl.BlockSpec(memory_space=pltpu.SMEM)],
        out_specs=pl.BlockSpec((1, A_TILE, A_TILE), lambda h, t: (h, 0, t)),
        out_shape=jax.ShapeDtypeStruct((A_HEADS, A_TILE, 2 * A_TILE), f32),
        compiler_params=pltpu.CompilerParams(dimension_semantics=("arbitrary", "arbitrary"),
                                             vmem_limit_bytes=VMEM_LIMIT),
        name="bias_tiles",
    )(tab)


C_MQK, C_MV, C_MO, C_AQ, C_AK, C_AV, C_GM, C_GA, C_GATE, C_END = (
    0, 1024, 1536, 2048, 2560, 3072, 3584, 4608, 5632, 5888)


def _inproj_kernel(x_ref, mod_ref, g_ref, w_ref, grp_ref, qg_ref, kg_ref,
                   mqk_ref, mv_ref, mo_ref, qn_ref, kn_ref, av_ref, gm_ref, ga_ref, gate_ref):
    x = x_ref[...]
    mod = mod_ref[0]
    ms = jnp.mean(x * x, axis=-1, keepdims=True)
    y = x * lax.rsqrt(ms + RMS_EPS) * g_ref[...]
    hb = (y * (1.0 + mod[1:2]) + mod[0:1]).astype(bf16)

    def proj(c0, c1):
        return _dot(hb, w_ref[:, c0:c1])

    def qknorm(a, gain_ref):
        sq = (a * a).astype(bf16)
        gw = grp_ref.shape[0]
        msq = jnp.concatenate([_dot(sq[:, c:c + gw], grp_ref[...]) for c in range(0, a.shape[1], gw)], axis=1)
        return (a * lax.rsqrt(msq + RMS_EPS) * gain_ref[...]).astype(bf16)

    mqk_ref[:, 0:512] = proj(C_MQK, C_MQK + 512).astype(bf16)
    mqk_ref[:, 512:1024] = proj(C_MQK + 512, C_MV).astype(bf16)
    mv_ref[...] = proj(C_MV, C_MO).astype(bf16)
    mo_ref[...] = _sigmoid(proj(C_MO, C_AQ)).astype(bf16)
    qn_ref[...] = qknorm(proj(C_AQ, C_AK), qg_ref)
    kn_ref[...] = qknorm(proj(C_AK, C_AV), kg_ref)
    av_ref[...] = proj(C_AV, C_GM).astype(bf16)
    gm_ref[:, 0:512] = proj(C_GM, C_GM + 512).astype(bf16)
    gm_ref[:, 512:1024] = proj(C_GM + 512, C_GA).astype(bf16)
    ga_ref[:, 0:512] = proj(C_GA, C_GA + 512).astype(bf16)
    ga_ref[:, 512:1024] = proj(C_GA + 512, C_GATE).astype(bf16)
    gate_ref[...] = proj(C_GATE, C_END)


def _inproj(x2, mod3, g1, w, grp, qg, kg, tiles_per_batch):
    n = x2.shape[0]
    tm = TM_PROJ
    tok = lambda width: pl.BlockSpec((tm, width), lambda i: (i, 0))
    outs = [(1024, bf16), (512, bf16), (512, bf16), (512, bf16), (512, bf16), (512, bf16),
            (1024, bf16), (1024, bf16), (2 * LANES, f32)]
    return pl.pallas_call(
        _inproj_kernel,
        grid=(n // tm,),
        in_specs=[tok(D_MODEL),
                  pl.BlockSpec((1, 6, D_MODEL), lambda i: (i // tiles_per_batch, 0, 0)),
                  _const_spec(g1.shape), _const_spec(w.shape), _const_spec(grp.shape),
                  _const_spec(qg.shape), _const_spec(kg.shape)],
        out_specs=[tok(wd) for wd, _ in outs],
        out_shape=[jax.ShapeDtypeStruct((n, wd), dt) for wd, dt in outs],
        compiler_params=pltpu.CompilerParams(dimension_semantics=("arbitrary",),
                                             vmem_limit_bytes=VMEM_LIMIT),
        name="inproj",
    )(x2, mod3, g1, w, grp, qg, kg)


def _mlstm_kernel(mqk_ref, mv_ref, mo_ref, gate_ref, cw_ref, cb_ref, bi_ref, bfg_ref, mg_ref,
                  o_ref, c_st, m_st, tail_ref):
    nb = mqk_ref.shape[0]
    L = M_CHUNK
    d = M_HEAD_DIM

    @pl.when(pl.program_id(0) == 0)
    def _():
        c_st[...] = jnp.zeros_like(c_st)
        m_st[...] = jnp.zeros_like(m_st)
        tail_ref[...] = jnp.zeros_like(tail_ref)

    row = lax.broadcasted_iota(jnp.int32, (L, L), 0)
    col = lax.broadcasted_iota(jnp.int32, (L, L), 1)
    causal = row >= col
    tri = jnp.where(causal, 1.0, 0.0).astype(bf16)
    ones = jnp.ones((L, d), bf16)
    cw = cw_ref[...]

    for b in range(nb):
        x = mqk_ref[b].astype(f32)
        prev8 = tail_ref[b]
        y = (cw[3:4] * x + cw[2:3] * _shift_rows(x, prev8, 1) + cw[1:2] * _shift_rows(x, prev8, 2)
             + cw[0:1] * _shift_rows(x, prev8, 3) + cb_ref[...])
        tail_ref[b] = x[L - SUBLANES:, :]
        qk = y * _sigmoid(y)

        gi = gate_ref[b, :, 0:LANES] + bi_ref[...]
        gf = gate_ref[b, :, LANES:2 * LANES] + bfg_ref[...]
        logf = jnp.minimum(gf, 0.0) - jnp.log(1.0 + jnp.exp(-jnp.abs(gf)))
        logf_hi = logf.astype(bf16)
        logf_lo = (logf - logf_hi.astype(f32)).astype(bf16)
        bcum = _dot(tri, logf_hi) + _dot(tri, logf_lo)
        r = gi - bcum
        r_t = r.T
        b_last = bcum[L - 1:L, :]
        g = b_last + r
        m_loc = jnp.max(g, axis=0, keepdims=True)

        for h in range(M_HEADS):
            st = b * M_HEADS + h
            hs = slice(h * d, (h + 1) * d)
            qh = qk[:, hs].astype(bf16)
            kh = qk[:, M_WIDTH + h * d:M_WIDTH + (h + 1) * d] * (d ** -0.5)
            vaug = jnp.concatenate([mv_ref[b, :, hs], ones], axis=1)
            m_in = m_st[st][0:1, 0:1]
            b_col = bcum[:, h:h + 1]
            dmat = jnp.where(causal, b_col + r_t[h:h + 1, :], NEG_BIG)
            a_t = b_col + m_in
            m_t = jnp.maximum(a_t, jnp.max(dmat, axis=1, keepdims=True))
            inter_w = jnp.exp(a_t - m_t)
            s = _dot_nt(qh, kh.astype(bf16)) * jnp.exp(dmat - m_t)
            tot = _dot(s.astype(bf16), vaug) + inter_w * _dot(qh, c_st[st].astype(bf16))
            den = jnp.maximum(jnp.abs(tot[:, d:]), jnp.exp(-m_t))
            hh = tot[:, :d] / den

            bl = b_last[:, h:h + 1]
            m_new = jnp.maximum(bl + m_in, m_loc[:, h:h + 1])
            w_col = jnp.exp(g[:, h:h + 1] - m_new)
            kw_t = (kh * w_col).T.astype(bf16)
            c_st[st] = jnp.exp(bl + m_in - m_new) * c_st[st] + _dot(kw_t, vaug)
            m_st[st] = jnp.broadcast_to(m_new, (SUBLANES, LANES))

            hn = hh * lax.rsqrt(jnp.mean(hh * hh, axis=-1, keepdims=True) + RMS_EPS) * mg_ref[:, hs]
            o_ref[b, :, hs] = (mo_ref[b, :, hs].astype(f32) * hn).astype(bf16)


def _mlstm(mqk, mv, mo, gates, cw, cb, bi, bfg, mg, batch, seq):
    L = M_CHUNK
    tok = lambda width: pl.BlockSpec((batch, L, width), lambda c: (0, c, 0))
    per_batch = lambda a: a.reshape(batch, seq, a.shape[-1])
    out = pl.pallas_call(
        _mlstm_kernel,
        grid=(seq // L,),
        in_specs=[tok(2 * M_WIDTH), tok(M_WIDTH), tok(M_WIDTH), tok(2 * LANES),
                  _const_spec(cw.shape), _const_spec(cb.shape), _const_spec(bi.shape),
                  _const_spec(bfg.shape), _const_spec(mg.shape)],
        out_specs=tok(M_WIDTH),
        out_shape=jax.ShapeDtypeStruct((batch, seq, M_WIDTH), bf16),
        scratch_shapes=[pltpu.VMEM((batch * M_HEADS, M_HEAD_DIM, 2 * M_HEAD_DIM), f32),
                        pltpu.VMEM((batch * M_HEADS, SUBLANES, LANES), f32),
                        pltpu.VMEM((batch, SUBLANES, 2 * M_WIDTH), f32)],
        compiler_params=pltpu.CompilerParams(dimension_semantics=("arbitrary",),
                                             vmem_limit_bytes=VMEM_LIMIT),
        name="mlstm",
    )(per_batch(mqk), per_batch(mv), per_batch(mo), per_batch(gates), cw, cb, bi, bfg, mg)
    return out.reshape(batch * seq, M_WIDTH)


def _attn_kernel(tab_ref, q_ref, k_ref, v_ref, bias_ref, lam_ref, ng_ref, qg_ref, o_ref,
                 vaug_ref, q2_ref, acc_ref, m_ref, mfix_ref, kmax_ref, safe_ref, *, lam_init):
    T = A_TILE
    dv = A_V_DIM
    hg = pl.program_id(1)
    qi = pl.program_id(2)
    heads = range(A_HPS)
    sel = jnp.where((lax.broadcasted_iota(jnp.int32, (LANES, 2 * LANES), 0) < A_QK_DIM)
                    == (lax.broadcasted_iota(jnp.int32, (LANES, 2 * LANES), 1) < LANES), 1.0, 0.0).astype(bf16)

    def sq_norms(x):
        xf = x.astype(f32)
        return _dot((xf * xf).astype(bf16), sel)

    def lanes_of(j):
        return slice(j * LANES, (j + 1) * LANES)

    bmax, bmin = [], []
    for j in heads:
        hi = jnp.float32(0.0)
        lo = jnp.float32(0.0)
        for b in range(N_BUCKETS - 1):
            rel = (tab_ref[hg * A_HPS + j, b] - tab_ref[hg * A_HPS + j, N_BUCKETS - 1]) * LOG2E
            hi = jnp.maximum(hi, rel)
            lo = jnp.minimum(lo, rel)
        bmax.append(hi)
        bmin.append(lo)

    @pl.when(qi == 0)
    def _():
        ok = None
        for j in heads:
            vaug_ref[j, :, 0:dv] = v_ref[:, lanes_of(j)]
            vaug_ref[j, :, dv:2 * dv] = jnp.ones((v_ref.shape[0], dv), bf16)
            kmax = jnp.zeros((1, 2 * LANES), f32)
            for c in range(k_ref.shape[0] // A_NORM_ROWS):
                kn2 = sq_norms(k_ref[c * A_NORM_ROWS:(c + 1) * A_NORM_ROWS, lanes_of(j)])
                kmax = jnp.maximum(kmax, jnp.max(kn2, axis=0, keepdims=True))
            kmax_ref[j] = kmax
            qmax = math.sqrt(A_QK_DIM) * jnp.max(jnp.abs(qg_ref[:, lanes_of(j)]))
            span = 2.0 * qmax * jnp.sqrt(jnp.max(kmax)) + (bmax[j] - bmin[j])
            ok_j = span < A_SAFE_RANGE
            ok = ok_j if ok is None else jnp.logical_and(ok, ok_j)
        safe_ref[0] = ok.astype(jnp.int32)

    for j in heads:
        q = q_ref[:, lanes_of(j)]
        lane = lax.broadcasted_iota(jnp.int32, q.shape, 1)
        zero = jnp.zeros_like(q)
        q2_ref[j, 0:T, :] = jnp.where(lane < A_QK_DIM, q, zero)
        q2_ref[j, T:2 * T, :] = jnp.where(lane >= A_QK_DIM, q, zero)
        bound = jnp.sqrt(sq_norms(q) * kmax_ref[j])
        mfix_ref[j, 0:T, :] = bound[:, 0:LANES] + bmax[j]
        mfix_ref[j, T:2 * T, :] = bound[:, LANES:2 * LANES] + bmax[j]
    safe = safe_ref[0] == 1

    def step(start, nk, near, fixed_shift):
        start = pl.multiple_of(start, T)
        for rb in range(2 * T // A_ROWS):
            rows = slice(rb * A_ROWS, (rb + 1) * A_ROWS)
            brow = (rb * A_ROWS) % T
            nkb = nk - (T - brow - A_ROWS) if near else nk
            for j in heads:
                kt = k_ref[pl.ds(start, nkb), lanes_of(j)]
                va = vaug_ref[j, pl.ds(start, nkb), :]
                s = _dot_nt(q2_ref[j, rows, :], kt)
                if near:
                    s = s + bias_ref[j, brow:brow + A_ROWS, 2 * T - nk:2 * T - nk + nkb]
                if fixed_shift:
                    mrow = mfix_ref[j, rows, :]
                    p = jnp.exp2(s - jnp.concatenate([mrow] * (nkb // LANES), axis=1)).astype(bf16)
                    if near:
                        acc_ref[j, rows, :] = _dot(p, va)
                    else:
                        acc_ref[j, rows, :] += _dot(p, va)
                elif near:
                    m_new = jnp.max(s, axis=1, keepdims=True)
                    acc_ref[j, rows, :] = _dot(jnp.exp2(s - m_new).astype(bf16), va)
                    m_ref[j, rows, :] = m_new
                else:
                    m_old = m_ref[j, rows, :]
                    m_new = jnp.maximum(m_old, jnp.max(s, axis=1, keepdims=True))
                    p = jnp.exp2(s - m_new).astype(bf16)
                    acc_ref[j, rows, :] = jnp.exp2(m_old - m_new) * acc_ref[j, rows, :] + _dot(p, va)
                    m_ref[j, rows, :] = m_new

    def sweep(fixed_shift):
        @pl.when(qi >= 1)
        def _():
            step((qi - 1) * T, 2 * T, True, fixed_shift)

        @pl.when(qi == 0)
        def _():
            step(0, T, True, fixed_shift)

        nfar = jnp.maximum(qi - 1, 0)
        nquad = nfar // 4

        def far_body(i, carry):
            step(i * (4 * T), 2 * T, False, fixed_shift)
            step(i * (4 * T) + 2 * T, 2 * T, False, fixed_shift)
            return carry

        lax.fori_loop(0, nquad, far_body, 0)

        @pl.when(nfar % 4 >= 2)
        def _():
            step(nquad * (4 * T), 2 * T, False, fixed_shift)

        @pl.when(nfar % 2 == 1)
        def _():
            step((nfar - 1) * T, T, False, fixed_shift)

    @pl.when(safe)
    def _():
        sweep(True)

    @pl.when(jnp.logical_not(safe))
    def _():
        sweep(False)

    al = lam_ref[...]
    lam = (jnp.exp(jnp.sum(al[0:1] * al[1:2], keepdims=True))
           - jnp.exp(jnp.sum(al[2:3] * al[3:4], keepdims=True)) + lam_init)
    for j in heads:
        acc = acc_ref[j]
        o = acc[:, 0:dv] / acc[:, dv:2 * dv]
        ha = o[0:T] - lam * o[T:2 * T]
        hn = ha * lax.rsqrt(jnp.mean(ha * ha, axis=-1, keepdims=True) + RMS_EPS) * ng_ref[:, lanes_of(j)]
        o_ref[:, lanes_of(j)] = (hn * (1.0 - lam_init)).astype(bf16)


def _diffattn(tab, qn, kn, av, bias, lam_par, ng, qg, batch, seq, lam_init):
    T = A_TILE
    nq = seq // T
    wide = A_HPS * LANES
    return pl.pallas_call(
        functools.partial(_attn_kernel, lam_init=lam_init),
        grid=(batch, A_HEADS // A_HPS, nq),
        in_specs=[pl.BlockSpec(memory_space=pltpu.SMEM),
                  pl.BlockSpec((T, wide), lambda b, h, i: (b * nq + i, h)),
                  pl.BlockSpec((seq, wide), lambda b, h, i: (b, h)),
                  pl.BlockSpec((seq, wide), lambda b, h, i: (b, h)),
                  pl.BlockSpec((A_HPS, T, 2 * T), lambda b, h, i: (h, 0, 0)),
                  _const_spec(lam_par.shape),
                  pl.BlockSpec((1, wide), lambda b, h, i: (0, h)),
                  pl.BlockSpec((1, wide), lambda b, h, i: (0, h))],
        out_specs=pl.BlockSpec((T, wide), lambda b, h, i: (b * nq + i, h)),
        out_shape=jax.ShapeDtypeStruct((batch * seq, A_WIDTH), bf16),
        scratch_shapes=[pltpu.VMEM((A_HPS, seq, 2 * A_V_DIM), bf16),
                        pltpu.VMEM((A_HPS, 2 * T, LANES), bf16),
                        pltpu.VMEM((A_HPS, 2 * T, 2 * A_V_DIM), f32),
                        pltpu.VMEM((A_HPS, 2 * T, 1), f32),
                        pltpu.VMEM((A_HPS, 2 * T, LANES), f32),
                        pltpu.VMEM((A_HPS, 1, 2 * LANES), f32),
                        pltpu.SMEM((1,), jnp.int32)],
        compiler_params=pltpu.CompilerParams(
            dimension_semantics=("arbitrary", "arbitrary", "arbitrary"),
            vmem_limit_bytes=VMEM_LIMIT),
        name="diffattn",
    )(tab, qn, kn, av, bias, lam_par, ng, qg)


def _ffn_kernel(x_ref, mod_ref, hm_ref, ha_ref, gm_ref, ga_ref, wm_ref, wa_ref, wo_ref,
                g_ref, wup_ref, cw_ref, cb_ref, wd_ref,
                o_ref, hv_ref, hg_ref, act_ref, stage_ref, *, tiles_per_batch):
    tm = x_ref.shape[0]
    halo = FFN_CONV - 1

    @pl.when(pl.program_id(0) % tiles_per_batch == 0)
    def _():
        hv_ref[...] = jnp.zeros_like(hv_ref)
        hg_ref[...] = jnp.zeros_like(hg_ref)

    mod = mod_ref[0]
    y = (_sigmoid(gm_ref[...].astype(f32)) * _dot(hm_ref[...], wm_ref[...])
         + _sigmoid(ga_ref[...].astype(f32)) * _dot(ha_ref[...], wa_ref[...]))
    x1 = x_ref[...] + mod[2:3] * _dot(y.astype(bf16), wo_ref[...])
    x = _to_token_strided(x1, stage_ref)
    ms = jnp.mean(x * x, axis=-1, keepdims=True)
    y = x * lax.rsqrt(ms + RMS_EPS) * g_ref[...]
    hb = (y * (1.0 + mod[4:5]) + mod[3:4]).astype(bf16)

    def conv(u, prev, w, b):
        return w[2:3] * u + w[1:2] * _shift_tokens(u, prev, 1) + w[0:1] * _shift_tokens(u, prev, 2) + b

    for j in range(N_FCHUNK):
        cols = slice(j * F_CHUNK, (j + 1) * F_CHUNK)
        gcols = slice(D_FF + j * F_CHUNK, D_FF + (j + 1) * F_CHUNK)
        uv = _dot(hb, wup_ref[:, cols])
        ug = _dot(hb, wup_ref[:, gcols])
        cv = conv(uv, hv_ref[j], cw_ref[:, cols], cb_ref[:, cols])
        cg = conv(ug, hg_ref[j], cw_ref[:, gcols], cb_ref[:, gcols])
        hv_ref[j] = uv[tm - halo * SUBLANES:, :]
        hg_ref[j] = ug[tm - halo * SUBLANES:, :]
        act_ref[:, cols] = ((cg * _sigmoid(cg)) * cv).astype(bf16)
    o_ref[...] = _from_token_strided(x + mod[5:6] * _dot(act_ref[...], wd_ref[...]), stage_ref)


def _ffn(x2, mod3, hm, ha, gm, ga, wm, wa, wo, g2, wup, cw, cb, wd, tiles_per_batch):
    n = x2.shape[0]
    tm = TM_PROJ
    tok = lambda width: pl.BlockSpec((tm, width), lambda i: (i, 0))
    return pl.pallas_call(
        functools.partial(_ffn_kernel, tiles_per_batch=tiles_per_batch),
        grid=(n // tm,),
        in_specs=[tok(D_MODEL),
                  pl.BlockSpec((1, 6, D_MODEL), lambda i: (i // tiles_per_batch, 0, 0)),
                  tok(M_WIDTH), tok(A_WIDTH), tok(D_MODEL), tok(D_MODEL),
                  _const_spec(wm.shape), _const_spec(wa.shape), _const_spec(wo.shape),
                  _const_spec(g2.shape), _const_spec(wup.shape), _const_spec(cw.shape),
                  _const_spec(cb.shape), _const_spec(wd.shape)],
        out_specs=tok(D_MODEL),
        out_shape=jax.ShapeDtypeStruct((n, D_MODEL), f32),
        scratch_shapes=[pltpu.VMEM((N_FCHUNK, (FFN_CONV - 1) * SUBLANES, F_CHUNK), f32),
                        pltpu.VMEM((N_FCHUNK, (FFN_CONV - 1) * SUBLANES, F_CHUNK), f32),
                        pltpu.VMEM((tm, D_FF), bf16),
                        pltpu.VMEM((D_MODEL // LANES, tm + SUBLANES * SUBLANES, LANES), f32)],
        compiler_params=pltpu.CompilerParams(dimension_semantics=("arbitrary",),
                                             vmem_limit_bytes=VMEM_LIMIT),
        name="ffn",
    )(x2, mod3, hm, ha, gm, ga, wm, wa, wo, g2, wup, cw, cb, wd)


def _pad_cols(a, width):
    return jnp.pad(a, ((0, 0), (0, width - a.shape[1])))


def _layer(x2, c8, batch, seq, layer, w_ada, b_ada, norm1_g, w_in, m_conv_w, m_conv_b, m_igate_b,
           m_fgate_b, m_norm_g, a_qnorm_g, a_knorm_g, a_lambda, a_norm_g, tab, bias, w_branch_m,
           w_branch_a, w_out, norm2_g, w_up, ffn_conv_w, ffn_conv_b, w_down):
    tiles_per_batch = seq // TM_PROJ
    mod3 = _adaln(c8, w_ada, b_ada.reshape(1, -1))[:batch].reshape(batch, 6, D_MODEL)

    o = 0
    parts = {}
    w_in = w_in.astype(bf16)
    for name, size in (("mqk", 2 * M_WIDTH), ("mv", M_WIDTH), ("mo", M_WIDTH), ("mi", M_HEADS),
                       ("mf", M_HEADS), ("aq", A_WIDTH), ("ak", A_WIDTH), ("av", A_WIDTH),
                       ("gm", D_MODEL), ("ga", D_MODEL)):
        parts[name] = w_in[:, o:o + size]
        o += size

    def per_head(w):
        return w.reshape(D_MODEL, 2, A_HEADS, A_QK_DIM).transpose(0, 2, 1, 3).reshape(D_MODEL, A_WIDTH)

    w_cat = jnp.concatenate(
        [parts["mqk"], parts["mv"], parts["mo"], per_head(parts["aq"]), per_head(parts["ak"]),
         parts["av"], parts["gm"], parts["ga"], _pad_cols(parts["mi"], LANES),
         _pad_cols(parts["mf"], LANES)], axis=1)
    gid = jnp.arange(2 * LANES) // A_QK_DIM
    grp = jnp.where(gid[:, None] == gid[None, :], 1.0 / A_QK_DIM, 0.0).astype(bf16)
    qg = (jnp.tile(a_qnorm_g, A_WIDTH // A_QK_DIM) * (A_QK_DIM ** -0.5 * LOG2E)).reshape(1, A_WIDTH)
    kg = jnp.tile(a_knorm_g, A_WIDTH // A_QK_DIM).reshape(1, A_WIDTH)

    mqk, mv, mo, qn, kn, av, gm, ga, gates = _inproj(
        x2, mod3, norm1_g.reshape(1, -1), w_cat, grp, qg, kg, tiles_per_batch)

    hm = _mlstm(mqk, mv, mo, gates, m_conv_w, m_conv_b.reshape(1, -1),
                _pad_cols(m_igate_b.reshape(1, -1), LANES), _pad_cols(m_fgate_b.reshape(1, -1), LANES),
                m_norm_g.reshape(1, -1), batch, seq)

    lam_init = 0.8 - 0.6 * math.exp(-0.3 * layer)
    ha = _diffattn(tab, qn, kn, av, bias, a_lambda, a_norm_g.reshape(1, -1), qg, batch, seq, lam_init)

    return _ffn(x2, mod3, hm, ha, gm, ga, w_branch_m.astype(bf16), w_branch_a.astype(bf16), w_out.astype(bf16),
                norm2_g.reshape(1, -1), w_up.astype(bf16), ffn_conv_w, ffn_conv_b.reshape(1, -1),
                w_down.astype(bf16), tiles_per_batch)


def kernel(x, c, w_ada, b_ada, norm1_g, w_in, m_conv_w, m_conv_b, m_igate_b, m_fgate_b, m_norm_g,
           a_qnorm_g, a_knorm_g, a_lambda, a_norm_g, rel_bias, w_branch_m, w_branch_a, w_out, norm2_g,
           w_up, ffn_conv_w, ffn_conv_b, w_down):
    batch, seq, _ = x.shape
    depth = w_ada.shape[0]
    x2 = x.reshape(batch * seq, D_MODEL)
    c8 = jnp.pad(c, ((0, SUBLANES - batch), (0, 0)))
    tab = rel_bias.astype(f32).T
    bias = _bias_tiles(tab)
    for l in range(depth):
        x2 = _layer(x2, c8, batch, seq, l, w_ada[l], b_ada[l], norm1_g[l], w_in[l], m_conv_w[l],
                    m_conv_b[l], m_igate_b[l], m_fgate_b[l], m_norm_g[l], a_qnorm_g[l], a_knorm_g[l],
                    a_lambda[l], a_norm_g[l], tab, bias, w_branch_m[l], w_branch_a[l], w_out[l], norm2_g[l],
                    w_up[l], ffn_conv_w[l], ffn_conv_b[l], w_down[l])
    return x2.reshape(batch, seq, D_MODEL)
```

```python
import functools
import math

import jax
import jax.numpy as jnp
from jax import lax
from jax.experimental import pallas as pl
from jax.experimental.pallas import tpu as pltpu

D_MODEL = 1024
M_HEADS = 4
M_HEAD_DIM = 128
M_WIDTH = M_HEADS * M_HEAD_DIM
M_CONV = 4
A_HEADS = 4
A_QK_DIM = 64
A_V_DIM = 2 * A_QK_DIM
A_WIDTH = A_HEADS * A_V_DIM
N_BUCKETS = 32
MAX_DISTANCE = 128
D_FF = 2816
FFN_CONV = 3
RMS_EPS = 1e-6
LOG2E = math.log2(math.e)
NEG_BIG = -1e30

LANES = 128
SUBLANES = 8
VMEM_LIMIT = 56 * 1024 * 1024

TM_PROJ = 512
M_CHUNK = 256
A_TILE = 512
A_ROWS = 256
A_HPS = 2
A_NORM_ROWS = 1024
A_SAFE_RANGE = 90.0
F_CHUNK = 256
N_FCHUNK = D_FF // F_CHUNK

bf16 = jnp.bfloat16
f32 = jnp.float32


def _dot(a, b):
    return jnp.dot(a, b, preferred_element_type=f32)


def _dot_nt(a, b):
    return lax.dot_general(a, b, (((1,), (1,)), ((), ())), preferred_element_type=f32)


def _sigmoid(x):
    return 1.0 / (1.0 + jnp.exp(-x))


def _const_spec(shape):
    nd = len(shape)
    return pl.BlockSpec(shape, lambda *_: (0,) * nd)


def _shift_rows(u, prev8, k):
    r = pltpu.roll(u, k, 0)
    rp = pltpu.roll(prev8, k, 0)
    row = lax.broadcasted_iota(jnp.int32, (SUBLANES, u.shape[1]), 0)
    first = jnp.where(row < k, rp, r[:SUBLANES])
    return jnp.concatenate([first, r[SUBLANES:]], axis=0)


def _to_token_strided(x, stage_ref):
    ng = x.shape[0] // SUBLANES
    pitch = ng + SUBLANES
    nl = x.shape[1] // LANES
    for c in range(nl):
        for sgm in range(SUBLANES):
            stage_ref[c, sgm * pitch:sgm * pitch + ng, :] = x[sgm * ng:(sgm + 1) * ng, c * LANES:(c + 1) * LANES]
    return jnp.concatenate(
        [jnp.concatenate([stage_ref[c, pl.ds(v, SUBLANES, stride=pitch), :] for c in range(nl)], axis=1)
         for v in range(ng)], axis=0)


def _from_token_strided(xs, stage_ref):
    ng = xs.shape[0] // SUBLANES
    pitch = ng + SUBLANES
    nl = xs.shape[1] // LANES
    for c in range(nl):
        for v in range(ng):
            stage_ref[c, pl.ds(v, SUBLANES, stride=pitch), :] = xs[v * SUBLANES:(v + 1) * SUBLANES,
                                                                   c * LANES:(c + 1) * LANES]
    return jnp.concatenate(
        [jnp.concatenate([stage_ref[c, sgm * pitch:sgm * pitch + ng, :] for c in range(nl)], axis=1)
         for sgm in range(SUBLANES)], axis=0)


def _shift_tokens(u, prev, k):
    rows = u.shape[0]
    ng = rows // SUBLANES
    halo = prev.shape[0] // SUBLANES
    first = lax.broadcasted_iota(jnp.int32, (SUBLANES, u.shape[1]), 0) == 0

    def wrapped(g):
        cur = u[g * SUBLANES:(g + 1) * SUBLANES, :]
        old = prev[(g - (ng - halo)) * SUBLANES:(g - (ng - halo) + 1) * SUBLANES, :]
        return jnp.where(first, pltpu.roll(old, 1, 0), pltpu.roll(cur, 1, 0))

    return jnp.concatenate([wrapped(ng - k + v) for v in range(k)] + [u[:rows - k * SUBLANES, :]], axis=0)


def _adaln_kernel(c_ref, w_ref, b_ref, o_ref):
    c = c_ref[...]
    a = (c * _sigmoid(c)).astype(bf16)
    o_ref[...] = _dot(a, w_ref[...].astype(bf16)) + b_ref[...]


def _adaln(c8, w, b):
    n = w.shape[1]
    tn = 1536
    return pl.pallas_call(
        _adaln_kernel,
        grid=(n // tn,),
        in_specs=[_const_spec(c8.shape),
                  pl.BlockSpec((D_MODEL, tn), lambda j: (0, j)),
                  pl.BlockSpec((1, tn), lambda j: (0, j))],
        out_specs=pl.BlockSpec((c8.shape[0], tn), lambda j: (0, j)),
        out_shape=jax.ShapeDtypeStruct((c8.shape[0], n), f32),
        compiler_params=pltpu.CompilerParams(dimension_semantics=("arbitrary",),
                                             vmem_limit_bytes=VMEM_LIMIT),
        name="adaln",
    )(c8, w, b)


def _bias_kernel(tab_ref, o_ref):
    h = pl.program_id(0)
    t = pl.program_id(1)
    blk = MAX_DISTANCE
    nblk = A_TILE // blk
    far = tab_ref[h, N_BUCKETS - 1]

    def block(r0, c0, shift):
        row = lax.broadcasted_iota(jnp.int32, (blk, blk), 0) + r0
        col = lax.broadcasted_iota(jnp.int32, (blk, blk), 1) + c0
        dist = row - col + shift
        n = jnp.maximum(dist, 0)
        max_exact = N_BUCKETS // 2
        nf = jnp.maximum(n, 1).astype(f32)
        large = max_exact + (jnp.log(nf / max_exact) / math.log(MAX_DISTANCE / max_exact)
                             * (N_BUCKETS - max_exact)).astype(jnp.int32)
        large = jnp.minimum(large, N_BUCKETS - 1)
        bucket = jnp.where(n < max_exact, n, large)
        val = jnp.zeros((blk, blk), f32)
        for b in range(N_BUCKETS - 1):
            val = jnp.where(bucket == b, tab_ref[h, b] - far, val)
        return jnp.where(dist >= 0, val * LOG2E, NEG_BIG)

    @pl.when(t == 0)
    def _():
        o_ref[0] = jnp.zeros((A_TILE, A_TILE), f32)
        o_ref[0, 0:blk, A_TILE - blk:A_TILE] = block(0, A_TILE - blk, A_TILE)

    @pl.when(t == 1)
    def _():
        for i in range(nblk):
            rows = slice(i * blk, (i + 1) * blk)
            if i >= 2:
                o_ref[0, rows, 0:(i - 1) * blk] = jnp.zeros((blk, (i - 1) * blk), f32)
            if i >= 1:
                o_ref[0, rows, (i - 1) * blk:i * blk] = block(i * blk, (i - 1) * blk, 0)
            o_ref[0, rows, i * blk:(i + 1) * blk] = block(i * blk, i * blk, 0)
            if i < nblk - 1:
                o_ref[0, rows, (i + 1) * blk:A_TILE] = jnp.full((blk, A_TILE - (i + 1) * blk), NEG_BIG, f32)


def _bias_tiles(tab):
    return pl.pallas_call(
        _bias_kernel,
        grid=(A_HEADS, 2),
        in_specs=[pl.BlockSpec(memory_space=pltpu.SMEM)],
        out_specs=pl.BlockSpec((1, A_TILE, A_TILE), lambda h, t: (h, 0, t)),
        out_shape=jax.ShapeDtypeStruct((A_HEADS, A_TILE, 2 * A_TILE), f32),
        compiler_params=pltpu.CompilerParams(dimension_semantics=("arbitrary", "arbitrary"),
                                             vmem_limit_bytes=VMEM_LIMIT),
        name="bias_tiles",
    )(tab)


C_MQK, C_MV, C_MO, C_AQ, C_AK, C_AV, C_GM, C_GA, C_GATE, C_END = (
    0, 1024, 1536, 2048, 2560, 3072, 3584, 4608, 5632, 5888)


def _inproj_kernel(x_ref, mod_ref, g_ref, w_ref, grp_ref, qg_ref, kg_ref,
                   mqk_ref, mv_ref, mo_ref, qn_ref, kn_ref, av_ref, gm_ref, ga_ref, gate_ref):
    x = x_ref[...]
    mod = mod_ref[0]
    ms = jnp.mean(x * x, axis=-1, keepdims=True)
    y = x * lax.rsqrt(ms + RMS_EPS) * g_ref[...]
    hb = (y * (1.0 + mod[1:2]) + mod[0:1]).astype(bf16)

    def proj(c0, c1):
        return _dot(hb, w_ref[:, c0:c1])

    def qknorm(a, gain_ref):
        sq = (a * a).astype(bf16)
        gw = grp_ref.shape[0]
        msq = jnp.concatenate([_dot(sq[:, c:c + gw], grp_ref[...]) for c in range(0, a.shape[1], gw)], axis=1)
        return (a * lax.rsqrt(msq + RMS_EPS) * gain_ref[...]).astype(bf16)

    mqk_ref[:, 0:512] = proj(C_MQK, C_MQK + 512).astype(bf16)
    mqk_ref[:, 512:1024] = proj(C_MQK + 512, C_MV).astype(bf16)
    mv_ref[...] = proj(C_MV, C_MO).astype(bf16)
    mo_ref[...] = _sigmoid(proj(C_MO, C_AQ)).astype(bf16)
    qn_ref[...] = qknorm(proj(C_AQ, C_AK), qg_ref)
    kn_ref[...] = qknorm(proj(C_AK, C_AV), kg_ref)
    av_ref[...] = proj(C_AV, C_GM).astype(bf16)
    gm_ref[:, 0:512] = proj(C_GM, C_GM + 512).astype(bf16)
    gm_ref[:, 512:1024] = proj(C_GM + 512, C_GA).astype(bf16)
    ga_ref[:, 0:512] = proj(C_GA, C_GA + 512).astype(bf16)
    ga_ref[:, 512:1024] = proj(C_GA + 512, C_GATE).astype(bf16)
    gate_ref[...] = proj(C_GATE, C_END)


def _inproj(x2, mod3, g1, w, grp, qg, kg, tiles_per_batch):
    n = x2.shape[0]
    tm = TM_PROJ
    tok = lambda width: pl.BlockSpec((tm, width), lambda i: (i, 0))
    outs = [(1024, bf16), (512, bf16), (512, bf16), (512, bf16), (512, bf16), (512, bf16),
            (1024, bf16), (1024, bf16), (2 * LANES, f32)]
    return pl.pallas_call(
        _inproj_kernel,
        grid=(n // tm,),
        in_specs=[tok(D_MODEL),
                  pl.BlockSpec((1, 6, D_MODEL), lambda i: (i // tiles_per_batch, 0, 0)),
                  _const_spec(g1.shape), _const_spec(w.shape), _const_spec(grp.shape),
                  _const_spec(qg.shape), _const_spec(kg.shape)],
        out_specs=[tok(wd) for wd, _ in outs],
        out_shape=[jax.ShapeDtypeStruct((n, wd), dt) for wd, dt in outs],
        compiler_params=pltpu.CompilerParams(dimension_semantics=("arbitrary",),
                                             vmem_limit_bytes=VMEM_LIMIT),
        name="inproj",
    )(x2, mod3, g1, w, grp, qg, kg)


def _mlstm_kernel(mqk_ref, mv_ref, mo_ref, gate_ref, cw_ref, cb_ref, bi_ref, bfg_ref, mg_ref,
                  o_ref, c_st, m_st, tail_ref):
    nb = mqk_ref.shape[0]
    L = M_CHUNK
    d = M_HEAD_DIM

    @pl.when(pl.program_id(0) == 0)
    def _():
        c_st[...] = jnp.zeros_like(c_st)
        m_st[...] = jnp.zeros_like(m_st)
        tail_ref[...] = jnp.zeros_like(tail_ref)

    row = lax.broadcasted_iota(jnp.int32, (L, L), 0)
    col = lax.broadcasted_iota(jnp.int32, (L, L), 1)
    causal = row >= col
    tri = jnp.where(causal, 1.0, 0.0).astype(bf16)
    ones = jnp.ones((L, d), bf16)
    cw = cw_ref[...]

    for b in range(nb):
        x = mqk_ref[b].astype(f32)
        prev8 = tail_ref[b]
        y = (cw[3:4] * x + cw[2:3] * _shift_rows(x, prev8, 1) + cw[1:2] * _shift_rows(x, prev8, 2)
             + cw[0:1] * _shift_rows(x, prev8, 3) + cb_ref[...])
        tail_ref[b] = x[L - SUBLANES:, :]
        qk = y * _sigmoid(y)

        gi = gate_ref[b, :, 0:LANES] + bi_ref[...]
        gf = gate_ref[b, :, LANES:2 * LANES] + bfg_ref[...]
        logf = jnp.minimum(gf, 0.0) - jnp.log(1.0 + jnp.exp(-jnp.abs(gf)))
        logf_hi = logf.astype(bf16)
        logf_lo = (logf - logf_hi.astype(f32)).astype(bf16)
        bcum = _dot(tri, logf_hi) + _dot(tri, logf_lo)
        r = gi - bcum
        r_t = r.T
        b_last = bcum[L - 1:L, :]
        g = b_last + r
        m_loc = jnp.max(g, axis=0, keepdims=True)

        for h in range(M_HEADS):
            st = b * M_HEADS + h
            hs = slice(h * d, (h + 1) * d)
            qh = qk[:, hs].astype(bf16)
            kh = qk[:, M_WIDTH + h * d:M_WIDTH + (h + 1) * d] * (d ** -0.5)
            vaug = jnp.concatenate([mv_ref[b, :, hs], ones], axis=1)
            m_in = m_st[st][0:1, 0:1]
            b_col = bcum[:, h:h + 1]
            dmat = jnp.where(causal, b_col + r_t[h:h + 1, :], NEG_BIG)
            a_t = b_col + m_in
            m_t = jnp.maximum(a_t, jnp.max(dmat, axis=1, keepdims=True))
            inter_w = jnp.exp(a_t - m_t)
            s = _dot_nt(qh, kh.astype(bf16)) * jnp.exp(dmat - m_t)
            tot = _dot(s.astype(bf16), vaug) + inter_w * _dot(qh, c_st[st].astype(bf16))
            den = jnp.maximum(jnp.abs(tot[:, d:]), jnp.exp(-m_t))
            hh = tot[:, :d] / den

            bl = b_last[:, h:h + 1]
            m_new = jnp.maximum(bl + m_in, m_loc[:, h:h + 1])
            w_col = jnp.exp(g[:, h:h + 1] - m_new)
            kw_t = (kh * w_col).T.astype(bf16)
            c_st[st] = jnp.exp(bl + m_in - m_new) * c_st[st] + _dot(kw_t, vaug)
            m_st[st] = jnp.broadcast_to(m_new, (SUBLANES, LANES))

            hn = hh * lax.rsqrt(jnp.mean(hh * hh, axis=-1, keepdims=True) + RMS_EPS) * mg_ref[:, hs]
            o_ref[b, :, hs] = (mo_ref[b, :, hs].astype(f32) * hn).astype(bf16)


def _mlstm(mqk, mv, mo, gates, cw, cb, bi, bfg, mg, batch, seq):
    L = M_CHUNK
    tok = lambda width: pl.BlockSpec((batch, L, width), lambda c: (0, c, 0))
    per_batch = lambda a: a.reshape(batch, seq, a.shape[-1])
    out = pl.pallas_call(
        _mlstm_kernel,
        grid=(seq // L,),
        in_specs=[tok(2 * M_WIDTH), tok(M_WIDTH), tok(M_WIDTH), tok(2 * LANES),
                  _const_spec(cw.shape), _const_spec(cb.shape), _const_spec(bi.shape),
                  _const_spec(bfg.shape), _const_spec(mg.shape)],
        out_specs=tok(M_WIDTH),
        out_shape=jax.ShapeDtypeStruct((batch, seq, M_WIDTH), bf16),
        scratch_shapes=[pltpu.VMEM((batch * M_HEADS, M_HEAD_DIM, 2 * M_HEAD_DIM), f32),
                        pltpu.VMEM((batch * M_HEADS, SUBLANES, LANES), f32),
                        pltpu.VMEM((batch, SUBLANES, 2 * M_WIDTH), f32)],
        compiler_params=pltpu.CompilerParams(dimension_semantics=("arbitrary",),
                                             vmem_limit_bytes=VMEM_LIMIT),
        name="mlstm",
    )(per_batch(mqk), per_batch(mv), per_batch(mo), per_batch(gates), cw, cb, bi, bfg, mg)
    return out.reshape(batch * seq, M_WIDTH)


def _attn_kernel(tab_ref, q_ref, k_ref, v_ref, bias_ref, lam_ref, ng_ref, qg_ref, o_ref,
                 vaug_ref, q2_ref, acc_ref, m_ref, shift_ref, safe_ref, *, lam_init):
    T = A_TILE
    dv = A_V_DIM
    hg = pl.program_id(1)
    qi = pl.program_id(2)
    heads = range(A_HPS)
    sel = jnp.where((lax.broadcasted_iota(jnp.int32, (LANES, 2 * LANES), 0) < A_QK_DIM)
                    == (lax.broadcasted_iota(jnp.int32, (LANES, 2 * LANES), 1) < LANES), 1.0, 0.0).astype(bf16)

    def sq_norms(x):
        xf = x.astype(f32)
        return _dot((xf * xf).astype(bf16), sel)

    def lanes_of(j):
        return slice(j * LANES, (j + 1) * LANES)

    bmax, bmin = [], []
    for j in heads:
        hi = jnp.float32(0.0)
        lo = jnp.float32(0.0)
        for b in range(N_BUCKETS - 1):
            rel = (tab_ref[hg * A_HPS + j, b] - tab_ref[hg * A_HPS + j, N_BUCKETS - 1]) * LOG2E
            hi = jnp.maximum(hi, rel)
            lo = jnp.minimum(lo, rel)
        bmax.append(hi)
        bmin.append(lo)

    @pl.when(qi == 0)
    def _():
        ok = None
        for j in heads:
            vaug_ref[j, :, 0:dv] = v_ref[:, lanes_of(j)]
            vaug_ref[j, :, dv:2 * dv] = jnp.ones((v_ref.shape[0], dv), bf16)
            kmax = jnp.zeros((1, 2 * LANES), f32)
            for c in range(k_ref.shape[0] // A_NORM_ROWS):
                kn2 = sq_norms(k_ref[c * A_NORM_ROWS:(c + 1) * A_NORM_ROWS, lanes_of(j)])
                kmax = jnp.maximum(kmax, jnp.max(kn2, axis=0, keepdims=True))
            qmax = math.sqrt(A_QK_DIM) * jnp.max(jnp.abs(qg_ref[:, lanes_of(j)]))
            reach = qmax * jnp.sqrt(jnp.max(kmax))
            shift_ref[j] = reach + bmax[j]
            span = 2.0 * reach + (bmax[j] - bmin[j])
            ok_j = span < A_SAFE_RANGE
            ok = ok_j if ok is None else jnp.logical_and(ok, ok_j)
        safe_ref[0] = ok.astype(jnp.int32)

    for j in heads:
        q = q_ref[:, lanes_of(j)]
        lane = lax.broadcasted_iota(jnp.int32, q.shape, 1)
        zero = jnp.zeros_like(q)
        q2_ref[j, 0:T, :] = jnp.where(lane < A_QK_DIM, q, zero)
        q2_ref[j, T:2 * T, :] = jnp.where(lane >= A_QK_DIM, q, zero)
    safe = safe_ref[0] == 1

    def step(start, nk, near, fixed_shift):
        start = pl.multiple_of(start, T)
        for rb in range(2 * T // A_ROWS):
            rows = slice(rb * A_ROWS, (rb + 1) * A_ROWS)
            brow = (rb * A_ROWS) % T
            nkb = nk - (T - brow - A_ROWS) if near else nk
            for j in heads:
                kt = k_ref[pl.ds(start, nkb), lanes_of(j)]
                va = vaug_ref[j, pl.ds(start, nkb), :]
                s = _dot_nt(q2_ref[j, rows, :], kt)
                if near:
                    s = s + bias_ref[j, brow:brow + A_ROWS, 2 * T - nk:2 * T - nk + nkb]
                if fixed_shift:
                    p = jnp.exp2(s - shift_ref[j]).astype(bf16)
                    if near:
                        acc_ref[j, rows, :] = _dot(p, va)
                    else:
                        acc_ref[j, rows, :] += _dot(p, va)
                elif near:
                    m_new = jnp.max(s, axis=1, keepdims=True)
                    acc_ref[j, rows, :] = _dot(jnp.exp2(s - m_new).astype(bf16), va)
                    m_ref[j, rows, :] = m_new
                else:
                    m_old = m_ref[j, rows, :]
                    m_new = jnp.maximum(m_old, jnp.max(s, axis=1, keepdims=True))
                    p = jnp.exp2(s - m_new).astype(bf16)
                    acc_ref[j, rows, :] = jnp.exp2(m_old - m_new) * acc_ref[j, rows, :] + _dot(p, va)
                    m_ref[j, rows, :] = m_new

    def sweep(fixed_shift):
        @pl.when(qi >= 1)
        def _():
            step((qi - 1) * T, 2 * T, True, fixed_shift)

        @pl.when(qi == 0)
        def _():
            step(0, T, True, fixed_shift)

        nfar = jnp.maximum(qi - 1, 0)
        nquad = nfar // 4

        def far_body(i, carry):
            step(i * (4 * T), 2 * T, False, fixed_shift)
            step(i * (4 * T) + 2 * T, 2 * T, False, fixed_shift)
            return carry

        lax.fori_loop(0, nquad, far_body, 0)

        @pl.when(nfar % 4 >= 2)
        def _():
            step(nquad * (4 * T), 2 * T, False, fixed_shift)

        @pl.when(nfar % 2 == 1)
        def _():
            step((nfar - 1) * T, T, False, fixed_shift)

    @pl.when(safe)
    def _():
        sweep(True)

    @pl.when(jnp.logical_not(safe))
    def _():
        sweep(False)

    al = lam_ref[...]
    lam = (jnp.exp(jnp.sum(al[0:1] * al[1:2], keepdims=True))
           - jnp.exp(jnp.sum(al[2:3] * al[3:4], keepdims=True)) + lam_init)
    for j in heads:
        acc = acc_ref[j]
        o = acc[:, 0:dv] / acc[:, dv:2 * dv]
        ha = o[0:T] - lam * o[T:2 * T]
        hn = ha * lax.rsqrt(jnp.mean(ha * ha, axis=-1, keepdims=True) + RMS_EPS) * ng_ref[:, lanes_of(j)]
        o_ref[:, lanes_of(j)] = (hn * (1.0 - lam_init)).astype(bf16)


def _diffattn(tab, qn, kn, av, bias, lam_par, ng, qg, batch, seq, lam_init):
    T = A_TILE
    nq = seq // T
    wide = A_HPS * LANES
    return pl.pallas_call(
        functools.partial(_attn_kernel, lam_init=lam_init),
        grid=(batch, A_HEADS // A_HPS, nq),
        in_specs=[pl.BlockSpec(memory_space=pltpu.SMEM),
                  pl.BlockSpec((T, wide), lambda b, h, i: (b * nq + i, h)),
                  pl.BlockSpec((seq, wide), lambda b, h, i: (b, h)),
                  pl.BlockSpec((seq, wide), lambda b, h, i: (b, h)),
                  pl.BlockSpec((A_HPS, T, 2 * T), lambda b, h, i: (h, 0, 0)),
                  _const_spec(lam_par.shape),
                  pl.BlockSpec((1, wide), lambda b, h, i: (0, h)),
                  pl.BlockSpec((1, wide), lambda b, h, i: (0, h))],
        out_specs=pl.BlockSpec((T, wide), lambda b, h, i: (b * nq + i, h)),
        out_shape=jax.ShapeDtypeStruct((batch * seq, A_WIDTH), bf16),
        scratch_shapes=[pltpu.VMEM((A_HPS, seq, 2 * A_V_DIM), bf16),
                        pltpu.VMEM((A_HPS, 2 * T, LANES), bf16),
                        pltpu.VMEM((A_HPS, 2 * T, 2 * A_V_DIM), f32),
                        pltpu.VMEM((A_HPS, 2 * T, 1), f32),
                        pltpu.SMEM((A_HPS,), f32),
                        pltpu.SMEM((1,), jnp.int32)],
        compiler_params=pltpu.CompilerParams(
            dimension_semantics=("arbitrary", "arbitrary", "arbitrary"),
            vmem_limit_bytes=VMEM_LIMIT),
        name="diffattn",
    )(tab, qn, kn, av, bias, lam_par, ng, qg)


def _ffn_kernel(x_ref, mod_ref, hm_ref, ha_ref, gm_ref, ga_ref, wm_ref, wa_ref, wo_ref,
                g_ref, wup_ref, cw_ref, cb_ref, wd_ref,
                o_ref, hv_ref, hg_ref, act_ref, stage_ref, *, tiles_per_batch):
    tm = x_ref.shape[0]
    halo = FFN_CONV - 1

    @pl.when(pl.program_id(0) % tiles_per_batch == 0)
    def _():
        hv_ref[...] = jnp.zeros_like(hv_ref)
        hg_ref[...] = jnp.zeros_like(hg_ref)

    mod = mod_ref[0]
    y = (_sigmoid(gm_ref[...].astype(f32)) * _dot(hm_ref[...], wm_ref[...])
         + _sigmoid(ga_ref[...].astype(f32)) * _dot(ha_ref[...], wa_ref[...]))
    x1 = x_ref[...] + mod[2:3] * _dot(y.astype(bf16), wo_ref[...])
    x = _to_token_strided(x1, stage_ref)
    ms = jnp.mean(x * x, axis=-1, keepdims=True)
    y = x * lax.rsqrt(ms + RMS_EPS) * g_ref[...]
    hb = (y * (1.0 + mod[4:5]) + mod[3:4]).astype(bf16)

    def conv(u, prev, w, b):
        return w[2:3] * u + w[1:2] * _shift_tokens(u, prev, 1) + w[0:1] * _shift_tokens(u, prev, 2) + b

    for j in range(N_FCHUNK):
        cols = slice(j * F_CHUNK, (j + 1) * F_CHUNK)
        gcols = slice(D_FF + j * F_CHUNK, D_FF + (j + 1) * F_CHUNK)
        uv = _dot(hb, wup_ref[:, cols])
        ug = _dot(hb, wup_ref[:, gcols])
        cv = conv(uv, hv_ref[j], cw_ref[:, cols], cb_ref[:, cols])
        cg = conv(ug, hg_ref[j], cw_ref[:, gcols], cb_ref[:, gcols])
        hv_ref[j] = uv[tm - halo * SUBLANES:, :]
        hg_ref[j] = ug[tm - halo * SUBLANES:, :]
        act_ref[:, cols] = ((cg * _sigmoid(cg)) * cv).astype(bf16)
    o_ref[...] = _from_token_strided(x + mod[5:6] * _dot(act_ref[...], wd_ref[...]), stage_ref)


def _ffn(x2, mod3, hm, ha, gm, ga, wm, wa, wo, g2, wup, cw, cb, wd, tiles_per_batch):
    n = x2.shape[0]
    tm = TM_PROJ
    tok = lambda width: pl.BlockSpec((tm, width), lambda i: (i, 0))
    return pl.pallas_call(
        functools.partial(_ffn_kernel, tiles_per_batch=tiles_per_batch),
        grid=(n // tm,),
        in_specs=[tok(D_MODEL),
                  pl.BlockSpec((1, 6, D_MODEL), lambda i: (i // tiles_per_batch, 0, 0)),
                  tok(M_WIDTH), tok(A_WIDTH), tok(D_MODEL), tok(D_MODEL),
                  _const_spec(wm.shape), _const_spec(wa.shape), _const_spec(wo.shape),
                  _const_spec(g2.shape), _const_spec(wup.shape), _const_spec(cw.shape),
                  _const_spec(cb.shape), _const_spec(wd.shape)],
        out_specs=tok(D_MODEL),
        out_shape=jax.ShapeDtypeStruct((n, D_MODEL), f32),
        scratch_shapes=[pltpu.VMEM((N_FCHUNK, (FFN_CONV - 1) * SUBLANES, F_CHUNK), f32),
                        pltpu.VMEM((N_FCHUNK, (FFN_CONV - 1) * SUBLANES, F_CHUNK), f32),
                        pltpu.VMEM((tm, D_FF), bf16),
                        pltpu.VMEM((D_MODEL // LANES, tm + SUBLANES * SUBLANES, LANES), f32)],
        compiler_params=pltpu.CompilerParams(dimension_semantics=("arbitrary",),
                                             vmem_limit_bytes=VMEM_LIMIT),
        name="ffn",
    )(x2, mod3, hm, ha, gm, ga, wm, wa, wo, g2, wup, cw, cb, wd)


def _pad_cols(a, width):
    return jnp.pad(a, ((0, 0), (0, width - a.shape[1])))


def _layer(x2, c8, batch, seq, layer, w_ada, b_ada, norm1_g, w_in, m_conv_w, m_conv_b, m_igate_b,
           m_fgate_b, m_norm_g, a_qnorm_g, a_knorm_g, a_lambda, a_norm_g, tab, bias, w_branch_m,
           w_branch_a, w_out, norm2_g, w_up, ffn_conv_w, ffn_conv_b, w_down):
    tiles_per_batch = seq // TM_PROJ
    mod3 = _adaln(c8, w_ada, b_ada.reshape(1, -1))[:batch].reshape(batch, 6, D_MODEL)

    o = 0
    parts = {}
    w_in = w_in.astype(bf16)
    for name, size in (("mqk", 2 * M_WIDTH), ("mv", M_WIDTH), ("mo", M_WIDTH), ("mi", M_HEADS),
                       ("mf", M_HEADS), ("aq", A_WIDTH), ("ak", A_WIDTH), ("av", A_WIDTH),
                       ("gm", D_MODEL), ("ga", D_MODEL)):
        parts[name] = w_in[:, o:o + size]
        o += size

    def per_head(w):
        return w.reshape(D_MODEL, 2, A_HEADS, A_QK_DIM).transpose(0, 2, 1, 3).reshape(D_MODEL, A_WIDTH)

    w_cat = jnp.concatenate(
        [parts["mqk"], parts["mv"], parts["mo"], per_head(parts["aq"]), per_head(parts["ak"]),
         parts["av"], parts["gm"], parts["ga"], _pad_cols(parts["mi"], LANES),
         _pad_cols(parts["mf"], LANES)], axis=1)
    gid = jnp.arange(2 * LANES) // A_QK_DIM
    grp = jnp.where(gid[:, None] == gid[None, :], 1.0 / A_QK_DIM, 0.0).astype(bf16)
    qg = (jnp.tile(a_qnorm_g, A_WIDTH // A_QK_DIM) * (A_QK_DIM ** -0.5 * LOG2E)).reshape(1, A_WIDTH)
    kg = jnp.tile(a_knorm_g, A_WIDTH // A_QK_DIM).reshape(1, A_WIDTH)

    mqk, mv, mo, qn, kn, av, gm, ga, gates = _inproj(
        x2, mod3, norm1_g.reshape(1, -1), w_cat, grp, qg, kg, tiles_per_batch)

    hm = _mlstm(mqk, mv, mo, gates, m_conv_w, m_conv_b.reshape(1, -1),
                _pad_cols(m_igate_b.reshape(1, -1), LANES), _pad_cols(m_fgate_b.reshape(1, -1), LANES),
                m_norm_g.reshape(1, -1), batch, seq)

    lam_init = 0.8 - 0.6 * math.exp(-0.3 * layer)
    ha = _diffattn(tab, qn, kn, av, bias, a_lambda, a_norm_g.reshape(1, -1), qg, batch, seq, lam_init)

    return _ffn(x2, mod3, hm, ha, gm, ga, w_branch_m.astype(bf16), w_branch_a.astype(bf16), w_out.astype(bf16),
                norm2_g.reshape(1, -1), w_up.astype(bf16), ffn_conv_w, ffn_conv_b.reshape(1, -1),
                w_down.astype(bf16), tiles_per_batch)


def kernel(x, c, w_ada, b_ada, norm1_g, w_in, m_conv_w, m_conv_b, m_igate_b, m_fgate_b, m_norm_g,
           a_qnorm_g, a_knorm_g, a_lambda, a_norm_g, rel_bias, w_branch_m, w_branch_a, w_out, norm2_g,
           w_up, ffn_conv_w, ffn_conv_b, w_down):
    batch, seq, _ = x.shape
    depth = w_ada.shape[0]
    x2 = x.reshape(batch * seq, D_MODEL)
    c8 = jnp.pad(c, ((0, SUBLANES - batch), (0, 0)))
    tab = rel_bias.astype(f32).T
    bias = _bias_tiles(tab)
    for l in range(depth):
        x2 = _layer(x2, c8, batch, seq, l, w_ada[l], b_ada[l], norm1_g[l], w_in[l], m_conv_w[l],
                    m_conv_b[l], m_igate_b[l], m_fgate_b[l], m_norm_g[l], a_qnorm_g[l], a_knorm_g[l],
                    a_lambda[l], a_norm_g[l], tab, bias, w_branch_m[l], w_branch_a[l], w_out[l], norm2_g[l],
                    w_up[l], ffn_conv_w[l], ffn_conv_b[l], w_down[l])
    return x2.reshape(batch, seq, D_MODEL)
```

```python
import functools
import math

import jax
import jax.numpy as jnp
from jax import lax
from jax.experimental import pallas as pl
from jax.experimental.pallas import tpu as pltpu

D_MODEL = 1024
M_HEADS = 4
M_HEAD_DIM = 128
M_WIDTH = M_HEADS * M_HEAD_DIM
M_CONV = 4
A_HEADS = 4
A_QK_DIM = 64
A_V_DIM = 2 * A_QK_DIM
A_WIDTH = A_HEADS * A_V_DIM
N_BUCKETS = 32
MAX_DISTANCE = 128
D_FF = 2816
FFN_CONV = 3
RMS_EPS = 1e-6
LOG2E = math.log2(math.e)
NEG_BIG = -1e30

LANES = 128
SUBLANES = 8
VMEM_LIMIT = 56 * 1024 * 1024

TM_PROJ = 512
TM_INPROJ = 1024
M_CHUNK = 256
A_TILE = 512
A_ROWS = 256
A_HPS = 2
A_NORM_ROWS = 1024
A_SAFE_RANGE = 90.0
F_CHUNK = 256
N_FCHUNK = D_FF // F_CHUNK

bf16 = jnp.bfloat16
f32 = jnp.float32


def _dot(a, b):
    return jnp.dot(a, b, preferred_element_type=f32)


def _dot_nt(a, b):
    return lax.dot_general(a, b, (((1,), (1,)), ((), ())), preferred_element_type=f32)


def _sigmoid(x):
    return 1.0 / (1.0 + jnp.exp(-x))


def _const_spec(shape):
    nd = len(shape)
    return pl.BlockSpec(shape, lambda *_: (0,) * nd)


def _shift_rows(u, prev8, k):
    r = pltpu.roll(u, k, 0)
    rp = pltpu.roll(prev8, k, 0)
    row = lax.broadcasted_iota(jnp.int32, (SUBLANES, u.shape[1]), 0)
    first = jnp.where(row < k, rp, r[:SUBLANES])
    return jnp.concatenate([first, r[SUBLANES:]], axis=0)


def _to_token_strided(x, stage_ref):
    ng = x.shape[0] // SUBLANES
    pitch = ng + SUBLANES
    nl = x.shape[1] // LANES
    for c in range(nl):
        for sgm in range(SUBLANES):
            stage_ref[c, sgm * pitch:sgm * pitch + ng, :] = x[sgm * ng:(sgm + 1) * ng, c * LANES:(c + 1) * LANES]
    return jnp.concatenate(
        [jnp.concatenate([stage_ref[c, pl.ds(v, SUBLANES, stride=pitch), :] for c in range(nl)], axis=1)
         for v in range(ng)], axis=0)


def _from_token_strided(xs, stage_ref):
    ng = xs.shape[0] // SUBLANES
    pitch = ng + SUBLANES
    nl = xs.shape[1] // LANES
    for c in range(nl):
        for v in range(ng):
            stage_ref[c, pl.ds(v, SUBLANES, stride=pitch), :] = xs[v * SUBLANES:(v + 1) * SUBLANES,
                                                                   c * LANES:(c + 1) * LANES]
    return jnp.concatenate(
        [jnp.concatenate([stage_ref[c, sgm * pitch:sgm * pitch + ng, :] for c in range(nl)], axis=1)
         for sgm in range(SUBLANES)], axis=0)


def _shift_tokens(u, prev, k):
    rows = u.shape[0]
    ng = rows // SUBLANES
    halo = prev.shape[0] // SUBLANES
    first = lax.broadcasted_iota(jnp.int32, (SUBLANES, u.shape[1]), 0) == 0

    def wrapped(g):
        cur = u[g * SUBLANES:(g + 1) * SUBLANES, :]
        old = prev[(g - (ng - halo)) * SUBLANES:(g - (ng - halo) + 1) * SUBLANES, :]
        return jnp.where(first, pltpu.roll(old, 1, 0), pltpu.roll(cur, 1, 0))

    return jnp.concatenate([wrapped(ng - k + v) for v in range(k)] + [u[:rows - k * SUBLANES, :]], axis=0)


def _adaln_kernel(c_ref, w_ref, b_ref, o_ref):
    c = c_ref[...]
    a = (c * _sigmoid(c)).astype(bf16)
    o_ref[...] = _dot(a, w_ref[...].astype(bf16)) + b_ref[...]


def _adaln(c8, w, b):
    n = w.shape[1]
    tn = 1536
    return pl.pallas_call(
        _adaln_kernel,
        grid=(n // tn,),
        in_specs=[_const_spec(c8.shape),
                  pl.BlockSpec((D_MODEL, tn), lambda j: (0, j)),
                  pl.BlockSpec((1, tn), lambda j: (0, j))],
        out_specs=pl.BlockSpec((c8.shape[0], tn), lambda j: (0, j)),
        out_shape=jax.ShapeDtypeStruct((c8.shape[0], n), f32),
        compiler_params=pltpu.CompilerParams(dimension_semantics=("arbitrary",),
                                             vmem_limit_bytes=VMEM_LIMIT),
        name="adaln",
    )(c8, w, b)


def _bias_kernel(tab_ref, o_ref):
    h = pl.program_id(0)
    t = pl.program_id(1)
    blk = MAX_DISTANCE
    nblk = A_TILE // blk
    far = tab_ref[h, N_BUCKETS - 1]

    def block(r0, c0, shift):
        row = lax.broadcasted_iota(jnp.int32, (blk, blk), 0) + r0
        col = lax.broadcasted_iota(jnp.int32, (blk, blk), 1) + c0
        dist = row - col + shift
        n = jnp.maximum(dist, 0)
        max_exact = N_BUCKETS // 2
        nf = jnp.maximum(n, 1).astype(f32)
        large = max_exact + (jnp.log(nf / max_exact) / math.log(MAX_DISTANCE / max_exact)
                             * (N_BUCKETS - max_exact)).astype(jnp.int32)
        large = jnp.minimum(large, N_BUCKETS - 1)
        bucket = jnp.where(n < max_exact, n, large)
        val = jnp.zeros((blk, blk), f32)
        for b in range(N_BUCKETS - 1):
            val = jnp.where(bucket == b, tab_ref[h, b] - far, val)
        return jnp.where(dist >= 0, val * LOG2E, NEG_BIG)

    @pl.when(t == 0)
    def _():
        o_ref[0] = jnp.zeros((A_TILE, A_TILE), f32)
        o_ref[0, 0:blk, A_TILE - blk:A_TILE] = block(0, A_TILE - blk, A_TILE)

    @pl.when(t == 1)
    def _():
        for i in range(nblk):
            rows = slice(i * blk, (i + 1) * blk)
            if i >= 2:
                o_ref[0, rows, 0:(i - 1) * blk] = jnp.zeros((blk, (i - 1) * blk), f32)
            if i >= 1:
                o_ref[0, rows, (i - 1) * blk:i * blk] = block(i * blk, (i - 1) * blk, 0)
            o_ref[0, rows, i * blk:(i + 1) * blk] = block(i * blk, i * blk, 0)
            if i < nblk - 1:
                o_ref[0, rows, (i + 1) * blk:A_TILE] = jnp.full((blk, A_TILE - (i + 1) * blk), NEG_BIG, f32)


def _bias_tiles(tab):
    return pl.pallas_call(
        _bias_kernel,
        grid=(A_HEADS, 2),
        in_specs=[pl.BlockSpec(memory_space=pltpu.SMEM)],
        out_specs=pl.BlockSpec((1, A_TILE, A_TILE), lambda h, t: (h, 0, t)),
        out_shape=jax.ShapeDtypeStruct((A_HEADS, A_TILE, 2 * A_TILE), f32),
        compiler_params=pltpu.CompilerParams(dimension_semantics=("arbitrary", "arbitrary"),
                                             vmem_limit_bytes=VMEM_LIMIT),
        name="bias_tiles",
    )(tab)


C_MQK, C_MV, C_MO, C_AQ, C_AK, C_AV, C_GM, C_GA, C_GATE, C_END = (
    0, 1024, 1536, 2048, 2560, 3072, 3584, 4608, 5632, 5888)


def _inproj_kernel(x_ref, mod_ref, g_ref, w_ref, grp_ref, qg_ref, kg_ref,
                   mqk_ref, mv_ref, mo_ref, qn_ref, kn_ref, av_ref, gm_ref, ga_ref, gate_ref):
    x = x_ref[...]
    mod = mod_ref[0]
    ms = jnp.mean(x * x, axis=-1, keepdims=True)
    y = x * lax.rsqrt(ms + RMS_EPS) * g_ref[...]
    hb = (y * (1.0 + mod[1:2]) + mod[0:1]).astype(bf16)

    def proj(c0, c1):
        return _dot(hb, w_ref[:, c0:c1])

    def qknorm(a, gain_ref):
        sq = (a * a).astype(bf16)
        gw = grp_ref.shape[0]
        msq = jnp.concatenate([_dot(sq[:, c:c + gw], grp_ref[...]) for c in range(0, a.shape[1], gw)], axis=1)
        return (a * lax.rsqrt(msq + RMS_EPS) * gain_ref[...]).astype(bf16)

    mqk_ref[:, 0:512] = proj(C_MQK, C_MQK + 512).astype(bf16)
    mqk_ref[:, 512:1024] = proj(C_MQK + 512, C_MV).astype(bf16)
    mv_ref[...] = proj(C_MV, C_MO).astype(bf16)
    mo_ref[...] = _sigmoid(proj(C_MO, C_AQ)).astype(bf16)
    qn_ref[...] = qknorm(proj(C_AQ, C_AK), qg_ref)
    kn_ref[...] = qknorm(proj(C_AK, C_AV), kg_ref)
    av_ref[...] = proj(C_AV, C_GM).astype(bf16)
    gm_ref[:, 0:512] = proj(C_GM, C_GM + 512).astype(bf16)
    gm_ref[:, 512:1024] = proj(C_GM + 512, C_GA).astype(bf16)
    ga_ref[:, 0:512] = proj(C_GA, C_GA + 512).astype(bf16)
    ga_ref[:, 512:1024] = proj(C_GA + 512, C_GATE).astype(bf16)
    gate_ref[...] = proj(C_GATE, C_END)


def _inproj(x2, mod3, g1, w, grp, qg, kg, tiles_per_batch):
    n = x2.shape[0]
    tm = TM_INPROJ
    tiles_per_batch = tiles_per_batch * TM_PROJ // tm
    tok = lambda width: pl.BlockSpec((tm, width), lambda i: (i, 0))
    outs = [(1024, bf16), (512, bf16), (512, bf16), (512, bf16), (512, bf16), (512, bf16),
            (1024, bf16), (1024, bf16), (2 * LANES, f32)]
    w_spec = pl.BlockSpec(w.shape, lambda i: (0, 0), pipeline_mode=pl.Buffered(1))
    return pl.pallas_call(
        _inproj_kernel,
        grid=(n // tm,),
        in_specs=[tok(D_MODEL),
                  pl.BlockSpec((1, 6, D_MODEL), lambda i: (i // tiles_per_batch, 0, 0)),
                  _const_spec(g1.shape), w_spec, _const_spec(grp.shape),
                  _const_spec(qg.shape), _const_spec(kg.shape)],
        out_specs=[tok(wd) for wd, _ in outs],
        out_shape=[jax.ShapeDtypeStruct((n, wd), dt) for wd, dt in outs],
        compiler_params=pltpu.CompilerParams(dimension_semantics=("arbitrary",),
                                             vmem_limit_bytes=VMEM_LIMIT),
        name="inproj",
    )(x2, mod3, g1, w, grp, qg, kg)


def _mlstm_kernel(mqk_ref, mv_ref, mo_ref, gate_ref, cw_ref, cb_ref, bi_ref, bfg_ref, mg_ref,
                  o_ref, c_st, m_st, tail_ref):
    nb = mqk_ref.shape[0]
    L = M_CHUNK
    d = M_HEAD_DIM

    @pl.when(pl.program_id(0) == 0)
    def _():
        c_st[...] = jnp.zeros_like(c_st)
        m_st[...] = jnp.zeros_like(m_st)
        tail_ref[...] = jnp.zeros_like(tail_ref)

    row = lax.broadcasted_iota(jnp.int32, (L, L), 0)
    col = lax.broadcasted_iota(jnp.int32, (L, L), 1)
    causal = row >= col
    tri = jnp.where(causal, 1.0, 0.0).astype(bf16)
    ones = jnp.ones((L, d), bf16)
    cw = cw_ref[...]

    for b in range(nb):
        x = mqk_ref[b].astype(f32)
        prev8 = tail_ref[b]
        y = (cw[3:4] * x + cw[2:3] * _shift_rows(x, prev8, 1) + cw[1:2] * _shift_rows(x, prev8, 2)
             + cw[0:1] * _shift_rows(x, prev8, 3) + cb_ref[...])
        tail_ref[b] = x[L - SUBLANES:, :]
        qk = y * _sigmoid(y)

        gi = gate_ref[b, :, 0:LANES] + bi_ref[...]
        gf = gate_ref[b, :, LANES:2 * LANES] + bfg_ref[...]
        logf = jnp.minimum(gf, 0.0) - jnp.log(1.0 + jnp.exp(-jnp.abs(gf)))
        logf_hi = logf.astype(bf16)
        logf_lo = (logf - logf_hi.astype(f32)).astype(bf16)
        bcum = _dot(tri, logf_hi) + _dot(tri, logf_lo)
        r = gi - bcum
        r_t = r.T
        b_last = bcum[L - 1:L, :]
        g = b_last + r
        m_loc = jnp.max(g, axis=0, keepdims=True)

        for h in range(M_HEADS):
            st = b * M_HEADS + h
            hs = slice(h * d, (h + 1) * d)
            qh = qk[:, hs].astype(bf16)
            kh = qk[:, M_WIDTH + h * d:M_WIDTH + (h + 1) * d] * (d ** -0.5)
            vaug = jnp.concatenate([mv_ref[b, :, hs], ones], axis=1)
            m_in = m_st[st][0:1, 0:1]
            b_col = bcum[:, h:h + 1]
            dmat = jnp.where(causal, b_col + r_t[h:h + 1, :], NEG_BIG)
            a_t = b_col + m_in
            m_t = jnp.maximum(a_t, jnp.max(dmat, axis=1, keepdims=True))
            inter_w = jnp.exp(a_t - m_t)
            s = _dot_nt(qh, kh.astype(bf16)) * jnp.exp(dmat - m_t)
            tot = _dot(s.astype(bf16), vaug) + inter_w * _dot(qh, c_st[st].astype(bf16))
            den = jnp.maximum(jnp.abs(tot[:, d:]), jnp.exp(-m_t))
            hh = tot[:, :d] / den

            bl = b_last[:, h:h + 1]
            m_new = jnp.maximum(bl + m_in, m_loc[:, h:h + 1])
            w_col = jnp.exp(g[:, h:h + 1] - m_new)
            kw_t = (kh * w_col).T.astype(bf16)
            c_st[st] = jnp.exp(bl + m_in - m_new) * c_st[st] + _dot(kw_t, vaug)
            m_st[st] = jnp.broadcast_to(m_new, (SUBLANES, LANES))

            hn = hh * lax.rsqrt(jnp.mean(hh * hh, axis=-1, keepdims=True) + RMS_EPS) * mg_ref[:, hs]
            o_ref[b, :, hs] = (mo_ref[b, :, hs].astype(f32) * hn).astype(bf16)


def _mlstm(mqk, mv, mo, gates, cw, cb, bi, bfg, mg, batch, seq):
    L = M_CHUNK
    tok = lambda width: pl.BlockSpec((batch, L, width), lambda c: (0, c, 0))
    per_batch = lambda a: a.reshape(batch, seq, a.shape[-1])
    out = pl.pallas_call(
        _mlstm_kernel,
        grid=(seq // L,),
        in_specs=[tok(2 * M_WIDTH), tok(M_WIDTH), tok(M_WIDTH), tok(2 * LANES),
                  _const_spec(cw.shape), _const_spec(cb.shape), _const_spec(bi.shape),
                  _const_spec(bfg.shape), _const_spec(mg.shape)],
        out_specs=tok(M_WIDTH),
        out_shape=jax.ShapeDtypeStruct((batch, seq, M_WIDTH), bf16),
        scratch_shapes=[pltpu.VMEM((batch * M_HEADS, M_HEAD_DIM, 2 * M_HEAD_DIM), f32),
                        pltpu.VMEM((batch * M_HEADS, SUBLANES, LANES), f32),
                        pltpu.VMEM((batch, SUBLANES, 2 * M_WIDTH), f32)],
        compiler_params=pltpu.CompilerParams(dimension_semantics=("arbitrary",),
                                             vmem_limit_bytes=VMEM_LIMIT),
        name="mlstm",
    )(per_batch(mqk), per_batch(mv), per_batch(mo), per_batch(gates), cw, cb, bi, bfg, mg)
    return out.reshape(batch * seq, M_WIDTH)


def _attn_kernel(tab_ref, q_ref, k_ref, v_ref, bias_ref, lam_ref, ng_ref, qg_ref, o_ref,
                 vaug_ref, q2_ref, acc_ref, m_ref, shift_ref, safe_ref, *, lam_init):
    T = A_TILE
    dv = A_V_DIM
    hg = pl.program_id(1)
    qi = pl.program_id(2)
    heads = range(A_HPS)
    sel = jnp.where((lax.broadcasted_iota(jnp.int32, (LANES, 2 * LANES), 0) < A_QK_DIM)
                    == (lax.broadcasted_iota(jnp.int32, (LANES, 2 * LANES), 1) < LANES), 1.0, 0.0).astype(bf16)

    def sq_norms(x):
        xf = x.astype(f32)
        return _dot((xf * xf).astype(bf16), sel)

    def lanes_of(j):
        return slice(j * LANES, (j + 1) * LANES)

    bmax, bmin = [], []
    for j in heads:
        hi = jnp.float32(0.0)
        lo = jnp.float32(0.0)
        for b in range(N_BUCKETS - 1):
            rel = (tab_ref[hg * A_HPS + j, b] - tab_ref[hg * A_HPS + j, N_BUCKETS - 1]) * LOG2E
            hi = jnp.maximum(hi, rel)
            lo = jnp.minimum(lo, rel)
        bmax.append(hi)
        bmin.append(lo)

    @pl.when(qi == 0)
    def _():
        ok = None
        for j in heads:
            vaug_ref[j, :, 0:dv] = v_ref[:, lanes_of(j)]
            vaug_ref[j, :, dv:2 * dv] = jnp.ones((v_ref.shape[0], dv), bf16)
            kmax = jnp.zeros((1, 2 * LANES), f32)
            for c in range(k_ref.shape[0] // A_NORM_ROWS):
                kn2 = sq_norms(k_ref[c * A_NORM_ROWS:(c + 1) * A_NORM_ROWS, lanes_of(j)])
                kmax = jnp.maximum(kmax, jnp.max(kn2, axis=0, keepdims=True))
            qmax = math.sqrt(A_QK_DIM) * jnp.max(jnp.abs(qg_ref[:, lanes_of(j)]))
            reach = qmax * jnp.sqrt(jnp.max(kmax))
            shift_ref[j] = reach + bmax[j]
            span = 2.0 * reach + (bmax[j] - bmin[j])
            ok_j = span < A_SAFE_RANGE
            ok = ok_j if ok is None else jnp.logical_and(ok, ok_j)
        safe_ref[0] = ok.astype(jnp.int32)

    for j in heads:
        q = q_ref[:, lanes_of(j)]
        lane = lax.broadcasted_iota(jnp.int32, q.shape, 1)
        zero = jnp.zeros_like(q)
        q2_ref[j, 0:T, :] = jnp.where(lane < A_QK_DIM, q, zero)
        q2_ref[j, T:2 * T, :] = jnp.where(lane >= A_QK_DIM, q, zero)
    safe = safe_ref[0] == 1

    def step(start, nk, near, fixed_shift):
        start = pl.multiple_of(start, T)
        for rb in range(2 * T // A_ROWS):
            rows = slice(rb * A_ROWS, (rb + 1) * A_ROWS)
            brow = (rb * A_ROWS) % T
            nkb = nk - (T - brow - A_ROWS) if near else nk
            for j in heads:
                kt = k_ref[pl.ds(start, nkb), lanes_of(j)]
                va = vaug_ref[j, pl.ds(start, nkb), :]
                s = _dot_nt(q2_ref[j, rows, :], kt)
                if near:
                    s = s + bias_ref[j, brow:brow + A_ROWS, 2 * T - nk:2 * T - nk + nkb]
                if fixed_shift:
                    p = jnp.exp2(s - shift_ref[j]).astype(bf16)
                    if near:
                        acc_ref[j, rows, :] = _dot(p, va)
                    else:
                        acc_ref[j, rows, :] += _dot(p, va)
                elif near:
                    m_new = jnp.max(s, axis=1, keepdims=True)
                    acc_ref[j, rows, :] = _dot(jnp.exp2(s - m_new).astype(bf16), va)
                    m_ref[j, rows, :] = m_new
                else:
                    m_old = m_ref[j, rows, :]
                    m_new = jnp.maximum(m_old, jnp.max(s, axis=1, keepdims=True))
                    p = jnp.exp2(s - m_new).astype(bf16)
                    acc_ref[j, rows, :] = jnp.exp2(m_old - m_new) * acc_ref[j, rows, :] + _dot(p, va)
                    m_ref[j, rows, :] = m_new

    def sweep(fixed_shift):
        @pl.when(qi >= 1)
        def _():
            step((qi - 1) * T, 2 * T, True, fixed_shift)

        @pl.when(qi == 0)
        def _():
            step(0, T, True, fixed_shift)

        nfar = jnp.maximum(qi - 1, 0)
        nquad = nfar // 4

        def far_body(i, carry):
            step(i * (4 * T), 2 * T, False, fixed_shift)
            step(i * (4 * T) + 2 * T, 2 * T, False, fixed_shift)
            return carry

        lax.fori_loop(0, nquad, far_body, 0)

        @pl.when(nfar % 4 >= 2)
        def _():
            step(nquad * (4 * T), 2 * T, False, fixed_shift)

        @pl.when(nfar % 2 == 1)
        def _():
            step((nfar - 1) * T, T, False, fixed_shift)

    @pl.when(safe)
    def _():
        sweep(True)

    @pl.when(jnp.logical_not(safe))
    def _():
        sweep(False)

    al = lam_ref[...]
    lam = (jnp.exp(jnp.sum(al[0:1] * al[1:2], keepdims=True))
           - jnp.exp(jnp.sum(al[2:3] * al[3:4], keepdims=True)) + lam_init)
    for j in heads:
        acc = acc_ref[j]
        o = acc[:, 0:dv] / acc[:, dv:2 * dv]
        ha = o[0:T] - lam * o[T:2 * T]
        hn = ha * lax.rsqrt(jnp.mean(ha * ha, axis=-1, keepdims=True) + RMS_EPS) * ng_ref[:, lanes_of(j)]
        o_ref[:, lanes_of(j)] = (hn * (1.0 - lam_init)).astype(bf16)


def _diffattn(tab, qn, kn, av, bias, lam_par, ng, qg, batch, seq, lam_init):
    T = A_TILE
    nq = seq // T
    wide = A_HPS * LANES
    return pl.pallas_call(
        functools.partial(_attn_kernel, lam_init=lam_init),
        grid=(batch, A_HEADS // A_HPS, nq),
        in_specs=[pl.BlockSpec(memory_space=pltpu.SMEM),
                  pl.BlockSpec((T, wide), lambda b, h, i: (b * nq + i, h)),
                  pl.BlockSpec((seq, wide), lambda b, h, i: (b, h)),
                  pl.BlockSpec((seq, wide), lambda b, h, i: (b, h)),
                  pl.BlockSpec((A_HPS, T, 2 * T), lambda b, h, i: (h, 0, 0)),
                  _const_spec(lam_par.shape),
                  pl.BlockSpec((1, wide), lambda b, h, i: (0, h)),
                  pl.BlockSpec((1, wide), lambda b, h, i: (0, h))],
        out_specs=pl.BlockSpec((T, wide), lambda b, h, i: (b * nq + i, h)),
        out_shape=jax.ShapeDtypeStruct((batch * seq, A_WIDTH), bf16),
        scratch_shapes=[pltpu.VMEM((A_HPS, seq, 2 * A_V_DIM), bf16),
                        pltpu.VMEM((A_HPS, 2 * T, LANES), bf16),
                        pltpu.VMEM((A_HPS, 2 * T, 2 * A_V_DIM), f32),
                        pltpu.VMEM((A_HPS, 2 * T, 1), f32),
                        pltpu.SMEM((A_HPS,), f32),
                        pltpu.SMEM((1,), jnp.int32)],
        compiler_params=pltpu.CompilerParams(
            dimension_semantics=("arbitrary", "arbitrary", "arbitrary"),
            vmem_limit_bytes=VMEM_LIMIT),
        name="diffattn",
    )(tab, qn, kn, av, bias, lam_par, ng, qg)


def _ffn_kernel(x_ref, mod_ref, hm_ref, ha_ref, gm_ref, ga_ref, wm_ref, wa_ref, wo_ref,
                g_ref, wup_ref, cw_ref, cb_ref, wd_ref,
                o_ref, hv_ref, hg_ref, act_ref, stage_ref, *, tiles_per_batch):
    tm = x_ref.shape[0]
    halo = FFN_CONV - 1

    @pl.when(pl.program_id(0) % tiles_per_batch == 0)
    def _():
        hv_ref[...] = jnp.zeros_like(hv_ref)
        hg_ref[...] = jnp.zeros_like(hg_ref)

    mod = mod_ref[0]
    y = (_sigmoid(gm_ref[...].astype(f32)) * _dot(hm_ref[...], wm_ref[...])
         + _sigmoid(ga_ref[...].astype(f32)) * _dot(ha_ref[...], wa_ref[...]))
    x1 = x_ref[...] + mod[2:3] * _dot(y.astype(bf16), wo_ref[...])
    x = _to_token_strided(x1, stage_ref)
    ms = jnp.mean(x * x, axis=-1, keepdims=True)
    y = x * lax.rsqrt(ms + RMS_EPS) * g_ref[...]
    hb = (y * (1.0 + mod[4:5]) + mod[3:4]).astype(bf16)

    def conv(u, prev, w, b):
        return w[2:3] * u + w[1:2] * _shift_tokens(u, prev, 1) + w[0:1] * _shift_tokens(u, prev, 2) + b

    for j in range(N_FCHUNK):
        cols = slice(j * F_CHUNK, (j + 1) * F_CHUNK)
        gcols = slice(D_FF + j * F_CHUNK, D_FF + (j + 1) * F_CHUNK)
        uv = _dot(hb, wup_ref[:, cols])
        ug = _dot(hb, wup_ref[:, gcols])
        cv = conv(uv, hv_ref[j], cw_ref[:, cols], cb_ref[:, cols])
        cg = conv(ug, hg_ref[j], cw_ref[:, gcols], cb_ref[:, gcols])
        hv_ref[j] = uv[tm - halo * SUBLANES:, :]
        hg_ref[j] = ug[tm - halo * SUBLANES:, :]
        act_ref[:, cols] = ((cg * _sigmoid(cg)) * cv).astype(bf16)
    o_ref[...] = _from_token_strided(x + mod[5:6] * _dot(act_ref[...], wd_ref[...]), stage_ref)


def _ffn(x2, mod3, hm, ha, gm, ga, wm, wa, wo, g2, wup, cw, cb, wd, tiles_per_batch):
    n = x2.shape[0]
    tm = TM_PROJ
    tok = lambda width: pl.BlockSpec((tm, width), lambda i: (i, 0))
    return pl.pallas_call(
        functools.partial(_ffn_kernel, tiles_per_batch=tiles_per_batch),
        grid=(n // tm,),
        in_specs=[tok(D_MODEL),
                  pl.BlockSpec((1, 6, D_MODEL), lambda i: (i // tiles_per_batch, 0, 0)),
                  tok(M_WIDTH), tok(A_WIDTH), tok(D_MODEL), tok(D_MODEL),
                  _const_spec(wm.shape), _const_spec(wa.shape), _const_spec(wo.shape),
                  _const_spec(g2.shape), _const_spec(wup.shape), _const_spec(cw.shape),
                  _const_spec(cb.shape), _const_spec(wd.shape)],
        out_specs=tok(D_MODEL),
        out_shape=jax.ShapeDtypeStruct((n, D_MODEL), f32),
        scratch_shapes=[pltpu.VMEM((N_FCHUNK, (FFN_CONV - 1) * SUBLANES, F_CHUNK), f32),
                        pltpu.VMEM((N_FCHUNK, (FFN_CONV - 1) * SUBLANES, F_CHUNK), f32),
                        pltpu.VMEM((tm, D_FF), bf16),
                        pltpu.VMEM((D_MODEL // LANES, tm + SUBLANES * SUBLANES, LANES), f32)],
        compiler_params=pltpu.CompilerParams(dimension_semantics=("arbitrary",),
                                             vmem_limit_bytes=VMEM_LIMIT),
        name="ffn",
    )(x2, mod3, hm, ha, gm, ga, wm, wa, wo, g2, wup, cw, cb, wd)


def _pad_cols(a, width):
    return jnp.pad(a, ((0, 0), (0, width - a.shape[1])))


def _layer(x2, c8, batch, seq, layer, w_ada, b_ada, norm1_g, w_in, m_conv_w, m_conv_b, m_igate_b,
           m_fgate_b, m_norm_g, a_qnorm_g, a_knorm_g, a_lambda, a_norm_g, tab, bias, w_branch_m,
           w_branch_a, w_out, norm2_g, w_up, ffn_conv_w, ffn_conv_b, w_down):
    tiles_per_batch = seq // TM_PROJ
    mod3 = _adaln(c8, w_ada, b_ada.reshape(1, -1))[:batch].reshape(batch, 6, D_MODEL)

    o = 0
    parts = {}
    w_in = w_in.astype(bf16)
    for name, size in (("mqk", 2 * M_WIDTH), ("mv", M_WIDTH), ("mo", M_WIDTH), ("mi", M_HEADS),
                       ("mf", M_HEADS), ("aq", A_WIDTH), ("ak", A_WIDTH), ("av", A_WIDTH),
                       ("gm", D_MODEL), ("ga", D_MODEL)):
        parts[name] = w_in[:, o:o + size]
        o += size

    def per_head(w):
        return w.reshape(D_MODEL, 2, A_HEADS, A_QK_DIM).transpose(0, 2, 1, 3).reshape(D_MODEL, A_WIDTH)

    w_cat = jnp.concatenate(
        [parts["mqk"], parts["mv"], parts["mo"], per_head(parts["aq"]), per_head(parts["ak"]),
         parts["av"], parts["gm"], parts["ga"], _pad_cols(parts["mi"], LANES),
         _pad_cols(parts["mf"], LANES)], axis=1)
    gid = jnp.arange(2 * LANES) // A_QK_DIM
    grp = jnp.where(gid[:, None] == gid[None, :], 1.0 / A_QK_DIM, 0.0).astype(bf16)
    qg = (jnp.tile(a_qnorm_g, A_WIDTH // A_QK_DIM) * (A_QK_DIM ** -0.5 * LOG2E)).reshape(1, A_WIDTH)
    kg = jnp.tile(a_knorm_g, A_WIDTH // A_QK_DIM).reshape(1, A_WIDTH)

    mqk, mv, mo, qn, kn, av, gm, ga, gates = _inproj(
        x2, mod3, norm1_g.reshape(1, -1), w_cat, grp, qg, kg, tiles_per_batch)

    hm = _mlstm(mqk, mv, mo, gates, m_conv_w, m_conv_b.reshape(1, -1),
                _pad_cols(m_igate_b.reshape(1, -1), LANES), _pad_cols(m_fgate_b.reshape(1, -1), LANES),
                m_norm_g.reshape(1, -1), batch, seq)

    lam_init = 0.8 - 0.6 * math.exp(-0.3 * layer)
    ha = _diffattn(tab, qn, kn, av, bias, a_lambda, a_norm_g.reshape(1, -1), qg, batch, seq, lam_init)

    return _ffn(x2, mod3, hm, ha, gm, ga, w_branch_m.astype(bf16), w_branch_a.astype(bf16), w_out.astype(bf16),
                norm2_g.reshape(1, -1), w_up.astype(bf16), ffn_conv_w, ffn_conv_b.reshape(1, -1),
                w_down.astype(bf16), tiles_per_batch)


def kernel(x, c, w_ada, b_ada, norm1_g, w_in, m_conv_w, m_conv_b, m_igate_b, m_fgate_b, m_norm_g,
           a_qnorm_g, a_knorm_g, a_lambda, a_norm_g, rel_bias, w_branch_m, w_branch_a, w_out, norm2_g,
           w_up, ffn_conv_w, ffn_conv_b, w_down):
    batch, seq, _ = x.shape
    depth = w_ada.shape[0]
    x2 = x.reshape(batch * seq, D_MODEL)
    c8 = jnp.pad(c, ((0, SUBLANES - batch), (0, 0)))
    tab = rel_bias.astype(f32).T
    bias = _bias_tiles(tab)
    for l in range(depth):
        x2 = _layer(x2, c8, batch, seq, l, w_ada[l], b_ada[l], norm1_g[l], w_in[l], m_conv_w[l],
                    m_conv_b[l], m_igate_b[l], m_fgate_b[l], m_norm_g[l], a_qnorm_g[l], a_knorm_g[l],
                    a_lambda[l], a_norm_g[l], tab, bias, w_branch_m[l], w_branch_a[l], w_out[l], norm2_g[l],
                    w_up[l], ffn_conv_w[l], ffn_conv_b[l], w_down[l])
    return x2.reshape(batch, seq, D_MODEL)
```
